```python
import math
import jax, jax.numpy as jnp
from jax import lax
import numpy as np

D_MODEL = 2048
BATCH = 4
SEQ = 2048
DEPTH = 4

GRID_W = 64
CTX_LEN = 256
N_ADA = 9
D_FF = 5504
GLA_HEADS = 4
GLA_DK = 128
GLA_DV = 256
GLA_KT = GLA_HEADS * GLA_DK
GLA_VT = GLA_HEADS * GLA_DV
GLA_RANK = 16
GLA_TAU = 16.0
GLA_CHUNK = 64
FNET_GROUPS = 4
FNET_GW = 128
FNET_W = FNET_GROUPS * FNET_GW
HY_W = 512
HY_ORDER = 2
HY_BANDS = 16
HY_EMB = 1 + 2 * HY_BANDS
HY_HID = 64
HY_SHORT = 3
HY_FAST = 0.3
HY_SLOW = 1.5
HY_TARGET = 1e-2
N_BRANCH = 3
EPS = 1e-6
PROJ_SIZES = (GLA_KT, GLA_KT, GLA_VT, 2 * GLA_RANK, GLA_VT, FNET_W, 3 * HY_W, N_BRANCH * D_MODEL)
D_IN = 2 * GLA_KT + 2 * GLA_VT + 2 * GLA_RANK + FNET_W + 3 * HY_W + N_BRANCH * D_MODEL
GLA_STATE_COLS = 2 * GLA_KT + GLA_VT + 2 * GLA_RANK

kernel_name = "hybrid_gla_fnet_hyena_prefix_dit"


def split_cols(t, sizes):
    return jnp.split(t, np.cumsum(sizes)[:-1].tolist(), axis=-1)


def rmsnorm(x, w):
    xf = x.astype(jnp.float32)
    y = xf * lax.rsqrt(jnp.mean(xf * xf, axis=-1, keepdims=True) + EPS)
    return (y * w).astype(x.dtype)


def modulated_norm(x, w, shift, scale):
    return rmsnorm(x, w) * (1 + scale) + shift


def swiglu(h, wi, wo):
    a, g = jnp.split(h @ wi, 2, axis=-1)
    return (jax.nn.silu(g) * a) @ wo


def gla_scan(q, k, v, logg, s0):
    B, L, H, dk = q.shape
    dv = v.shape[-1]
    n = L // GLA_CHUNK

    def chunks(t):
        return t.reshape(B, n, GLA_CHUNK, H, t.shape[-1]).transpose(0, 3, 1, 2, 4)

    qc, kc, vc, gc = chunks(q), chunks(k), chunks(v), chunks(logg)
    gcum = jnp.cumsum(gc.astype(jnp.float32), axis=3)
    gtot = gcum[..., -1:, :]
    qf = qc.astype(jnp.float32) * jnp.exp(gcum)
    kf = kc.astype(jnp.float32) * jnp.exp(-gcum)
    kend = kc.astype(jnp.float32) * jnp.exp(gtot - gcum)
    vf = vc.astype(jnp.float32)
    mask = jnp.tril(jnp.ones((GLA_CHUNK, GLA_CHUNK), dtype=bool))
    att = jnp.where(mask, jnp.einsum('bhnid,bhnjd->bhnij', qf, kf), 0.0)
    o_intra = jnp.einsum('bhnij,bhnjv->bhniv', att, vf)
    kv = jnp.einsum('bhncd,bhncv->bhndv', kend, vf)
    decay = jnp.exp(gtot[..., 0, :])

    def step(s, inp):
        dec, kvn = inp
        return dec[..., None] * s + kvn, s

    s_final, s_prev = lax.scan(step, s0, (jnp.moveaxis(decay, 2, 0), jnp.moveaxis(kv, 2, 0)))
    s_prev = jnp.moveaxis(s_prev, 0, 2)
    o_inter = jnp.einsum('bhncd,bhndv->bhncv', qf, s_prev)
    o = (o_intra + o_inter).transpose(0, 2, 3, 1, 4).reshape(B, L, H, dv)
    return o, s_final


def gla_branch(q, k, v, glow, w2, gb, s0f, s0b):
    B, L = q.shape[:2]
    q = q.reshape(B, L, GLA_HEADS, GLA_DK) * (GLA_DK ** -0.5)
    k = k.reshape(B, L, GLA_HEADS, GLA_DK)
    v = v.reshape(B, L, GLA_HEADS, GLA_DV)
    logits = jnp.einsum('blzr,zrk->blzk', glow.reshape(B, L, 2, GLA_RANK), w2) + gb
    logg = (jax.nn.log_sigmoid(logits.astype(jnp.float32)) / GLA_TAU).reshape(B, L, 2, GLA_HEADS, GLA_DK)
    of, sf = gla_scan(q, k, v, logg[:, :, 0], s0f)
    flip = lambda t: jnp.flip(t, axis=1)
    ob, sb = gla_scan(flip(q), flip(k), flip(v), flip(logg[:, :, 1]), s0b)
    return of + flip(ob), sf, sb


def gla_output(o, r, w):
    B, L = o.shape[:2]
    of = o.astype(jnp.float32)
    of = of * lax.rsqrt(jnp.mean(of * of, axis=-1, keepdims=True) + EPS)
    return (of.reshape(B, L, GLA_VT) * w * jax.nn.silu(r.astype(jnp.float32))).astype(r.dtype)


def fnet_mix(u):
    B, L = u.shape[:2]
    ug = u.reshape(B, L, FNET_GROUPS, FNET_GW).astype(jnp.float32)
    y = jnp.fft.fft2(ug, axes=(1, 3), norm='ortho').real
    return y.reshape(B, L, FNET_W).astype(u.dtype)


def short_conv(u, w, b, n_seg):
    B, L, C = u.shape
    us = u.reshape(B, n_seg, L // n_seg, C)
    up = jnp.pad(us, ((0, 0), (0, 0), (1, 1), (0, 0)))
    y = up[:, :, :-2] * w[0] + up[:, :, 1:-1] * w[1] + up[:, :, 2:] * w[2] + b
    return y.reshape(B, L, C)


def hyena_filters(L, w1, b1, w2, b2, w3, freq):
    t = jnp.linspace(0.0, 1.0, L, dtype=jnp.float32)[:, None]
    t_idx = jnp.arange(L, dtype=jnp.float32)[:, None]
    bands = jnp.linspace(1e-4, HY_BANDS - 1, HY_BANDS, dtype=jnp.float32)[None, :]
    ang = 2.0 * math.pi * t_idx * bands / L
    z = jnp.concatenate([t, jnp.cos(ang), -jnp.sin(ang)], axis=-1)
    h = jnp.sin(freq * (z @ w1 + b1))
    h = jnp.sin(freq * (h @ w2 + b2))
    h = (h @ w3).astype(jnp.float32).reshape(L, HY_ORDER, 2, HY_W)
    deltas = jnp.linspace(math.log(HY_TARGET) / HY_FAST, math.log(HY_TARGET) / HY_SLOW, HY_W, dtype=jnp.float32)
    h = h * jnp.exp(-t * jnp.abs(deltas))[:, None, None, :]
    return h * lax.rsqrt(jnp.sum(h * h, axis=(0, 2), keepdims=True) + EPS)


def long_conv_bidir(u, h_f, h_b, bias):
    L = u.shape[1]
    kfull = jnp.concatenate([h_f, jnp.flip(h_b, axis=0)], axis=0)
    kf = jnp.fft.rfft(kfull, n=2 * L, axis=0)
    uf = jnp.fft.rfft(u.astype(jnp.float32), n=2 * L, axis=1)
    y = jnp.fft.irfft(uf * kf, n=2 * L, axis=1)[:, :L]
    return (y + u.astype(jnp.float32) * bias).astype(u.dtype)


def hyena_mix(uh, lp, n_seg):
    L = uh.shape[1]
    u = short_conv(uh, lp['hy_conv_w'], lp['hy_conv_b'], n_seg)
    v, x1, x2 = jnp.split(u, 3, axis=-1)
    h = hyena_filters(L, lp['hy_f1_w'], lp['hy_f1_b'], lp['hy_f2_w'], lp['hy_f2_b'], lp['hy_f3_w'], lp['hy_freq'])
    z = x1 * long_conv_bidir(v, h[:, 0, 0], h[:, 0, 1], lp['hy_bias'][0])
    return x2 * long_conv_bidir(z, h[:, 1, 0], h[:, 1, 1], lp['hy_bias'][1])


def mix_stream(xn, lp, s0f, s0b, n_seg):
    B, L, _ = xn.shape
    q, k, v, glow, r, uf, uh, gl = split_cols(xn @ lp['w_in'], PROJ_SIZES)
    o, sf, sb = gla_branch(q, k, v, glow, lp['gla_w2'], lp['gla_gb'], s0f, s0b)
    y_gla = gla_output(o, r, lp['gla_norm_w'])
    y_fnet = fnet_mix(uf)
    y_hy = hyena_mix(uh, lp, n_seg)
    g = jax.nn.sigmoid(gl.astype(jnp.float32)).reshape(B, L, N_BRANCH, D_MODEL)
    merged = (g[:, :, 0] * (y_gla @ lp['w_br_gla'])
              + g[:, :, 1] * (y_fnet @ lp['w_br_fnet'])
              + g[:, :, 2] * (y_hy @ lp['w_br_hy']))
    return merged.astype(xn.dtype) @ lp['w_o'], sf, sb


def gla_context_states(cn, lp, s0):
    q, k, v, glow = split_cols(cn @ lp['w_in'][:, :GLA_STATE_COLS], (GLA_KT, GLA_KT, GLA_VT, 2 * GLA_RANK))
    _, sf, sb = gla_branch(q, k, v, glow, lp['gla_w2'], lp['gla_gb'], s0, s0)
    return sf, sb


def setup_inputs(seed: int = 0) -> dict:
    key = jax.random.key(seed)
    ks = jax.random.split(key, 32)
    f32 = jnp.float32
    nrm = lambda k, shape, s: jax.random.normal(k, shape, f32) * s
    D = D_MODEL
    return {
        'x': nrm(ks[0], (BATCH, SEQ, D), 1.0),
        'c': nrm(ks[1], (BATCH, D), 1.0),
        'ctx': nrm(ks[2], (BATCH, CTX_LEN, D), 1.0),
        'c_ctx': nrm(ks[3], (D,), 1.0),
        'ada_w': nrm(ks[4], (DEPTH, D, N_ADA * D), 0.5 * D ** -0.5),
        'ada_b': nrm(ks[5], (DEPTH, N_ADA * D), 0.01),
        'norm_w': 1.0 + nrm(ks[6], (DEPTH, 3, D), 0.01),
        'ffn1_wi': nrm(ks[7], (DEPTH, D, 2 * D_FF), D ** -0.5),
        'ffn1_wo': nrm(ks[8], (DEPTH, D_FF, D), D_FF ** -0.5),
        'ffn2_wi': nrm(ks[9], (DEPTH, D, 2 * D_FF), D ** -0.5),
        'ffn2_wo': nrm(ks[10], (DEPTH, D_FF, D), D_FF ** -0.5),
        'w_in': nrm(ks[11], (DEPTH, D, D_IN), D ** -0.5),
        'gla_w2': nrm(ks[12], (DEPTH, 2, GLA_RANK, GLA_KT), GLA_RANK ** -0.5),
        'gla_gb': nrm(ks[13], (DEPTH, 2, GLA_KT), 0.01),
        'gla_norm_w': 1.0 + nrm(ks[14], (DEPTH, GLA_VT), 0.01),
        'hy_conv_w': nrm(ks[15], (DEPTH, HY_SHORT, 3 * HY_W), HY_SHORT ** -0.5),
        'hy_conv_b': nrm(ks[16], (DEPTH, 3 * HY_W), 0.01),
        'hy_f1_w': nrm(ks[17], (DEPTH, HY_EMB, HY_HID), HY_EMB ** -0.5),
        'hy_f1_b': nrm(ks[18], (DEPTH, HY_HID), 0.01),
        'hy_f2_w': nrm(ks[19], (DEPTH, HY_HID, HY_HID), HY_HID ** -0.5),
        'hy_f2_b': nrm(ks[20], (DEPTH, HY_HID), 0.01),
        'hy_f3_w': nrm(ks[21], (DEPTH, HY_HID, HY_ORDER * 2 * HY_W), HY_HID ** -0.5),
        'hy_freq': 1.0 + nrm(ks[22], (DEPTH, HY_HID), 0.01),
        'hy_bias': nrm(ks[23], (DEPTH, HY_ORDER, HY_W), 1.0),
        'w_br_gla': nrm(ks[24], (DEPTH, GLA_VT, D), GLA_VT ** -0.5),
        'w_br_fnet': nrm(ks[25], (DEPTH, FNET_W, D), FNET_W ** -0.5),
        'w_br_hy': nrm(ks[26], (DEPTH, HY_W, D), HY_W ** -0.5),
        'w_o': nrm(ks[27], (DEPTH, D, D), D ** -0.5),
        'final_norm_w': 1.0 + nrm(ks[28], (D,), 0.01),
    }


def reference(x, c, ctx, c_ctx, ada_w, ada_b, norm_w, ffn1_wi, ffn1_wo, ffn2_wi, ffn2_wo, w_in,
              gla_w2, gla_gb, gla_norm_w, hy_conv_w, hy_conv_b, hy_f1_w, hy_f1_b, hy_f2_w, hy_f2_b,
              hy_f3_w, hy_freq, hy_bias, w_br_gla, w_br_fnet, w_br_hy, w_o, final_norm_w):
    B, L, _ = x.shape
    rows = L // GRID_W
    for l in range(DEPTH):
        lp = {'w_in': w_in[l], 'gla_w2': gla_w2[l], 'gla_gb': gla_gb[l], 'gla_norm_w': gla_norm_w[l],
              'hy_conv_w': hy_conv_w[l], 'hy_conv_b': hy_conv_b[l], 'hy_f1_w': hy_f1_w[l],
              'hy_f1_b': hy_f1_b[l], 'hy_f2_w': hy_f2_w[l], 'hy_f2_b': hy_f2_b[l], 'hy_f3_w': hy_f3_w[l],
              'hy_freq': hy_freq[l], 'hy_bias': hy_bias[l], 'w_br_gla': w_br_gla[l],
              'w_br_fnet': w_br_fnet[l], 'w_br_hy': w_br_hy[l], 'w_o': w_o[l]}
        last = l == DEPTH - 1
        m_x = (jax.nn.silu(c) @ ada_w[l] + ada_b[l]).reshape(B, N_ADA, 1, D_MODEL)
        m_c = (jax.nn.silu(c_ctx) @ ada_w[l] + ada_b[l]).reshape(N_ADA, D_MODEL)
        x = x + 0.5 * m_x[:, 2] * swiglu(modulated_norm(x, norm_w[l, 0], m_x[:, 0], m_x[:, 1]), ffn1_wi[l], ffn1_wo[l])
        ctx = ctx + 0.5 * m_c[2] * swiglu(modulated_norm(ctx, norm_w[l, 0], m_c[0], m_c[1]), ffn1_wi[l], ffn1_wo[l])
        xn = modulated_norm(x, norm_w[l, 1], m_x[:, 3], m_x[:, 4])
        cn = modulated_norm(ctx, norm_w[l, 1], m_c[3], m_c[4])
        s0 = jnp.zeros((B, GLA_HEADS, GLA_DK, GLA_DV), jnp.float32)
        if last:
            sf, sb = gla_context_states(cn, lp, s0)
        else:
            y_c, sf, sb = mix_stream(cn, lp, s0, s0, 1)
            ctx = ctx + m_c[5] * y_c
            ctx = ctx + 0.5 * m_c[8] * swiglu(modulated_norm(ctx, norm_w[l, 2], m_c[6], m_c[7]), ffn2_wi[l], ffn2_wo[l])
        y_x, _, _ = mix_stream(xn, lp, sf, sb, rows)
        x = x + m_x[:, 5] * y_x
        x = x + 0.5 * m_x[:, 8] * swiglu(modulated_norm(x, norm_w[l, 2], m_x[:, 6], m_x[:, 7]), ffn2_wi[l], ffn2_wo[l])
    return rmsnorm(x, final_norm_w)
```

```python
import functools
import math

import jax
import jax.numpy as jnp
import numpy as np
from jax import lax
from jax.experimental import pallas as pl
from jax.experimental.pallas import tpu as pltpu

F32 = jnp.float32
BF16 = jnp.bfloat16

D_MODEL = 2048
DEPTH = 4
GRID_W = 64
N_ADA = 9
D_FF = 5504
GLA_HEADS = 4
GLA_DK = 128
GLA_DV = 256
GLA_KT = GLA_HEADS * GLA_DK
GLA_VT = GLA_HEADS * GLA_DV
GLA_RANK = 16
GLA_TAU = 16.0
GLA_CHUNK = 64
FNET_GROUPS = 4
FNET_GW = 128
FNET_W = FNET_GROUPS * FNET_GW
HY_W = 512
HY_BANDS = 16
HY_HID = 64
HY_FAST = 0.3
HY_SLOW = 1.5
HY_TARGET = 1e-2
EPS = 1e-6

LANES = 128
VMEM_LIMIT_BYTES = 56 * 1024 * 1024

D_FF_PAD = 5632
FF_TILE = 512
P_GATE = 0
P_V = 6144
P_R = 7168
P_HY = 8192
P_Q = 9728
P_K = 10240
P_FN = 10752
P_GLOW = 11264
P_TOTAL = 11520
P_TILE = 1280


def _mm(a, b):
    return jnp.dot(a, b, preferred_element_type=F32)


def _mm_nt(a, b):
    return lax.dot_general(a, b, (((1,), (1,)), ((), ())), preferred_element_type=F32)


def _mm_tn(a, b):
    return lax.dot_general(a, b, (((0,), (0,)), ((), ())), preferred_element_type=F32)


def _split2(a):
    hi = a.astype(BF16)
    lo = (a - hi.astype(F32)).astype(BF16)
    return hi, lo


def _mm3(a, b):
    ah, al = _split2(a)
    bh, bl = _split2(b)
    return _mm(ah, bh) + (_mm(ah, bl) + _mm(al, bh))


def _mm_sel(sel_bf16, g):
    g1 = g.astype(BF16)
    r1 = g - g1.astype(F32)
    g2 = r1.astype(BF16)
    g3 = (r1 - g2.astype(F32)).astype(BF16)
    return _mm(sel_bf16, g1) + (_mm(sel_bf16, g2) + _mm(sel_bf16, g3))


def _silu(x):
    return x * jax.nn.sigmoid(x)


def _params(sem):
    return pltpu.CompilerParams(dimension_semantics=sem, vmem_limit_bytes=VMEM_LIMIT_BYTES)


def _modnorm(x, nw, shift, scale):
    ms = jnp.mean(x * x, axis=-1, keepdims=True)
    y = x * lax.rsqrt(ms + EPS) * nw
    return y * (1.0 + scale) + shift


def _ada_kernel(c_ref, w_ref, b_ref, o_ref):
    a = _silu(c_ref[...]).astype(BF16)
    o_ref[...] = _mm(a, w_ref[...].astype(BF16)) + b_ref[...]


def _ada_call(cond, ada_w, ada_b):
    depth, d, n = ada_w.shape
    rows = cond.shape[0]
    tn = 1024
    out = pl.pallas_call(
        _ada_kernel,
        grid=(depth, n // tn),
        in_specs=[
            pl.BlockSpec((rows, d), lambda l, j: (0, 0)),
            pl.BlockSpec((None, d, tn), lambda l, j: (l, 0, j)),
            pl.BlockSpec((None, 1, tn), lambda l, j: (l, 0, j)),
        ],
        out_specs=pl.BlockSpec((None, rows, tn), lambda l, j: (l, 0, j)),
        out_shape=jax.ShapeDtypeStruct((depth, rows, n), F32),
        compiler_params=_params(("arbitrary", "arbitrary")),
        name="ada_mod",
    )(cond, ada_w, ada_b.reshape(depth, 1, n))
    return out.reshape(depth, rows, N_ADA, d)


def _mod_row_map(rows_per_mod, tm, fixed_row):
    if fixed_row is not None:
        return lambda i: fixed_row
    per = rows_per_mod // tm
    return lambda i: i // per


def _ffn_kernel(x_ref, mod_ref, nw_ref, wa_ref, wg_ref, wo_ref, fw_ref, o_ref, xn_ref, *, sub, nf, final):
    f = pl.program_id(1)

    @pl.when(f == 0)
    def _():
        xn = _modnorm(x_ref[...], nw_ref[sub:sub + 1, :], mod_ref[3 * sub:3 * sub + 1, :],
                      mod_ref[3 * sub + 1:3 * sub + 2, :])
        xn_ref[...] = xn.astype(BF16)

    xn = xn_ref[...]
    a = _mm(xn, wa_ref[...])
    g = _mm(xn, wg_ref[...])
    h = (_silu(g) * a).astype(BF16)
    part = _mm(h, wo_ref[...])

    @pl.when(f == 0)
    def _():
        o_ref[...] = part

    @pl.when(f > 0)
    def _():
        o_ref[...] += part

    @pl.when(f == nf - 1)
    def _():
        y = x_ref[...] + 0.5 * mod_ref[3 * sub + 2:3 * sub + 3, :] * o_ref[...]
        if final:
            ms = jnp.mean(y * y, axis=-1, keepdims=True)
            y = y * lax.rsqrt(ms + EPS) * fw_ref[...]
        o_ref[...] = y


def _ffn_call(xs, mod, norm_w, wa, wg, wo, final_w, *, layer, sub, rows_per_mod, fixed_row, final):
    t, d = xs.shape
    tm = 512
    nf = D_FF_PAD // FF_TILE
    row = _mod_row_map(rows_per_mod, tm, fixed_row)
    return pl.pallas_call(
        functools.partial(_ffn_kernel, sub=sub, nf=nf, final=final),
        grid=(t // tm, nf),
        in_specs=[
            pl.BlockSpec((tm, d), lambda i, f: (i, 0)),
            pl.BlockSpec((None, None, N_ADA, d), lambda i, f: (layer, row(i), 0, 0)),
            pl.BlockSpec((None, 3, d), lambda i, f: (layer, 0, 0)),
            pl.BlockSpec((None, d, FF_TILE), lambda i, f: (layer, 0, f)),
            pl.BlockSpec((None, d, FF_TILE), lambda i, f: (layer, 0, f)),
            pl.BlockSpec((None, FF_TILE, d), lambda i, f: (layer, f, 0)),
            pl.BlockSpec((1, d), lambda i, f: (0, 0)),
        ],
        out_specs=pl.BlockSpec((tm, d), lambda i, f: (i, 0)),
        out_shape=jax.ShapeDtypeStruct((t, d), F32),
        scratch_shapes=[pltpu.VMEM((tm, d), BF16)],
        compiler_params=_params(("parallel", "arbitrary")),
        name="ffn",
    )(xs, mod, norm_w, wa, wg, wo, final_w)


def _proj_kernel(x_ref, mod_ref, nw_ref, w_ref, o_ref, xn_ref):
    @pl.when(pl.program_id(1) == 0)
    def _():
        xn = _modnorm(x_ref[...], nw_ref[1:2, :], mod_ref[3:4, :], mod_ref[4:5, :])
        xn_ref[...] = xn.astype(BF16)

    o_ref[...] = _mm(xn_ref[...], w_ref[...]).astype(BF16)


def _proj_call(xs, mod, norm_w, w_p, *, layer, rows_per_mod, fixed_row):
    t, d = xs.shape
    tm = 1024
    row = _mod_row_map(rows_per_mod, tm, fixed_row)
    return pl.pallas_call(
        _proj_kernel,
        grid=(t // tm, P_TOTAL // P_TILE),
        in_specs=[
            pl.BlockSpec((tm, d), lambda i, n: (i, 0)),
            pl.BlockSpec((None, None, N_ADA, d), lambda i, n: (layer, row(i), 0, 0)),
            pl.BlockSpec((None, 3, d), lambda i, n: (layer, 0, 0)),
            pl.BlockSpec((None, d, P_TILE), lambda i, n: (layer, 0, n)),
        ],
        out_specs=pl.BlockSpec((tm, P_TILE), lambda i, n: (i, n)),
        out_shape=jax.ShapeDtypeStruct((t, P_TOTAL), BF16),
        scratch_shapes=[pltpu.VMEM((tm, d), BF16)],
        compiler_params=_params(("parallel", "arbitrary")),
        name="mix_proj",
    )(xs, mod, norm_w, w_p)


def _log_sigmoid(x):
    return jnp.minimum(x, 0.0) - jnp.log1p(jnp.exp(-jnp.abs(x)))


def _gla_kernel(q_ref, k_ref, v_ref, r_ref, gl_ref, w2f_ref, w2b_ref, gbf_ref, gbb_ref, nw_ref,
                s0f_ref, s0b_ref, y_ref, sf_ref, sb_ref, lgf_ref, lgb_ref, of_ref, ob_ref, *, seq):
    c = GLA_CHUNK
    n_chunks = seq // c
    glow = gl_ref[...]
    lgf_ref[...] = _log_sigmoid(_mm(glow, w2f_ref[...]) + gbf_ref[...]) * (1.0 / GLA_TAU)
    lgb_ref[...] = _log_sigmoid(_mm(glow, w2b_ref[...]) + gbb_ref[...]) * (1.0 / GLA_TAU)
    sf_ref[...] = s0f_ref[...]
    sb_ref[...] = s0b_ref[...]

    ri = lax.broadcasted_iota(jnp.int32, (c, c), 0)
    ci = lax.broadcasted_iota(jnp.int32, (c, c), 1)
    lower = ri >= ci
    upper = ri <= ci
    lower_sel = jnp.where(lower, 1.0, 0.0).astype(BF16)
    upper_sel = jnp.where(upper, 1.0, 0.0).astype(BF16)
    q_scale = GLA_DK ** -0.5

    def one_chunk(r0, lg_ref, sel, mask, tot_row, st_ref, o_ref):
        g = lg_ref[pl.ds(r0, c), :]
        gc = _mm_sel(sel, g)
        gt = gc[tot_row:tot_row + 1, :]
        q = q_ref[pl.ds(r0, c), :].astype(F32) * q_scale
        k = k_ref[pl.ds(r0, c), :].astype(F32)
        v = v_ref[pl.ds(r0, c), :]
        qf = (q * jnp.exp(gc)).astype(BF16)
        kf = (k * jnp.exp(-gc)).astype(BF16)
        kend = (k * jnp.exp(gt - gc)).astype(BF16)
        att = jnp.where(mask, _mm_nt(qf, kf), 0.0).astype(BF16)
        st = st_ref[...]
        o_ref[pl.ds(r0, c), :] = _mm(att, v) + _mm_nt(qf, st.astype(BF16))
        st_ref[...] = st * jnp.exp(gt) + _mm_tn(v, kend)

    def body(n, carry):
        one_chunk(pl.multiple_of(n * c, c), lgf_ref, lower_sel, lower, c - 1, sf_ref, of_ref)
        one_chunk(pl.multiple_of((n_chunks - 1 - n) * c, c), lgb_ref, upper_sel, upper, 0, sb_ref, ob_ref)
        return carry

    lax.fori_loop(0, n_chunks, body, 0)

    o = of_ref[...] + ob_ref[...]
    ms = jnp.mean(o * o, axis=-1, keepdims=True)
    y = o * lax.rsqrt(ms + EPS) * nw_ref[...] * _silu(r_ref[...].astype(F32))
    y_ref[...] = y.astype(BF16)


def _gla_call(proj, w2f, w2b, gbf, gbb, gla_nw, s0f, s0b, *, batch, seq):
    t = proj.shape[0]
    h = GLA_HEADS
    qb, kb = P_Q // GLA_DK, P_K // GLA_DK
    vb, rb = P_V // GLA_DV, P_R // GLA_DV
    glb = P_GLOW // LANES
    st_spec = pl.BlockSpec((None, None, GLA_DV, GLA_DK), lambda b, hh: (b, hh, 0, 0))
    st_shape = jax.ShapeDtypeStruct((batch, h, GLA_DV, GLA_DK), F32)
    return pl.pallas_call(
        functools.partial(_gla_kernel, seq=seq),
        grid=(batch, h),
        in_specs=[
            pl.BlockSpec((seq, GLA_DK), lambda b, hh: (b, qb + hh)),
            pl.BlockSpec((seq, GLA_DK), lambda b, hh: (b, kb + hh)),
            pl.BlockSpec((seq, GLA_DV), lambda b, hh: (b, vb + hh)),
            pl.BlockSpec((seq, GLA_DV), lambda b, hh: (b, rb + hh)),
            pl.BlockSpec((seq, LANES), lambda b, hh: (b, glb)),
            pl.BlockSpec((None, LANES, GLA_DK), lambda b, hh: (hh, 0, 0)),
            pl.BlockSpec((None, LANES, GLA_DK), lambda b, hh: (hh, 0, 0)),
            pl.BlockSpec((None, 1, GLA_DK), lambda b, hh: (hh, 0, 0)),
            pl.BlockSpec((None, 1, GLA_DK), lambda b, hh: (hh, 0, 0)),
            pl.BlockSpec((None, 1, GLA_DV), lambda b, hh: (hh, 0, 0)),
            st_spec,
            st_spec,
        ],
        out_specs=[pl.BlockSpec((seq, GLA_DV), lambda b, hh: (b, hh)), st_spec, st_spec],
        out_shape=[jax.ShapeDtypeStruct((t, GLA_VT), BF16), st_shape, st_shape],
        scratch_shapes=[
            pltpu.VMEM((seq, GLA_DK), F32),
            pltpu.VMEM((seq, GLA_DK), F32),
            pltpu.VMEM((seq, GLA_DV), F32),
            pltpu.VMEM((seq, GLA_DV), F32),
        ],
        compiler_params=_params(("parallel", "parallel")),
        name="gla",
    )(proj, proj, proj, proj, proj, w2f, w2b, gbf, gbb, gla_nw, s0f, s0b)


def _fnet_kernel(u_ref, cl_ref, sl_ref, cg_ref, sg_ref, o_ref, *, scale):
    u = u_ref[...]
    p = _mm(cl_ref[...], u).astype(BF16)
    q = _mm(sl_ref[...], u).astype(BF16)
    y = _mm(p, cg_ref[...]) - _mm(q, sg_ref[...])
    o_ref[...] = (y * scale).astype(BF16)


def _fnet_call(proj, cl, sl, cg, sg, *, batch, seq):
    t = proj.shape[0]
    tr = min(512, seq)
    nr = seq // tr
    ub = P_FN // FNET_W
    return pl.pallas_call(
        functools.partial(_fnet_kernel, scale=1.0 / math.sqrt(seq * FNET_GW)),
        grid=(nr, batch),
        in_specs=[
            pl.BlockSpec((seq, FNET_W), lambda r, b: (b, ub)),
            pl.BlockSpec((tr, seq), lambda r, b: (r, 0)),
            pl.BlockSpec((tr, seq), lambda r, b: (r, 0)),
            pl.BlockSpec((FNET_W, FNET_W), lambda r, b: (0, 0)),
            pl.BlockSpec((FNET_W, FNET_W), lambda r, b: (0, 0)),
        ],
        out_specs=pl.BlockSpec((tr, FNET_W), lambda r, b: (b * nr + r, 0)),
        out_shape=jax.ShapeDtypeStruct((t, FNET_W), BF16),
        compiler_params=_params(("arbitrary", "arbitrary")),
        name="fnet",
    )(proj, cl, sl, cg, sg)


def _hy_filter_kernel(w1t_ref, w1c_ref, w1s_ref, b1_ref, w2_ref, b2_ref, w3_ref, fr_ref, o_ref, *, seq):
    ti = lax.broadcasted_iota(jnp.int32, (seq, LANES), 0).astype(F32)
    lane = lax.broadcasted_iota(jnp.int32, (seq, LANES), 1)
    band_step = (HY_BANDS - 1 - 1e-4) / (HY_BANDS - 1)
    bands = jnp.where(lane < HY_BANDS, 1e-4 + lane.astype(F32) * band_step, 0.0)
    ang = 2.0 * math.pi * ti * bands / seq
    t_col = ti[:, 0:1] / (seq - 1.0)
    fr = fr_ref[...]
    pre1 = t_col * w1t_ref[...] + _mm3(jnp.cos(ang), w1c_ref[...]) + _mm3(-jnp.sin(ang), w1s_ref[...]) + b1_ref[...]
    h1 = jnp.sin(fr * pre1)
    h2 = jnp.sin(fr * (_mm3(h1, w2_ref[...]) + b2_ref[...]))
    ch = lax.broadcasted_iota(jnp.int32, (1, HY_W), 1).astype(F32)
    d0 = math.log(HY_TARGET) / HY_FAST
    d1 = math.log(HY_TARGET) / HY_SLOW
    deltas = jnp.abs(d0 + ch * ((d1 - d0) / (HY_W - 1)))
    win = jnp.exp(-t_col * deltas)
    ss = jnp.zeros((1, HY_W), F32)
    for lo in (0, HY_W):
        hd = _mm3(h2, w3_ref[:, lo:lo + HY_W]) * win
        o_ref[:, lo:lo + HY_W] = hd
        ss = ss + jnp.sum(hd * hd, axis=0, keepdims=True)
    inv = lax.rsqrt(ss + EPS)
    for lo in (0, HY_W):
        o_ref[:, lo:lo + HY_W] = o_ref[:, lo:lo + HY_W] * inv


def _hy_filter_call(w1t, w1c, w1s, b1, w2, b2, w3, fr, *, seq):
    hid = HY_HID
    full = lambda shape: pl.BlockSpec(shape, lambda o: (0,) * len(shape))
    return pl.pallas_call(
        functools.partial(_hy_filter_kernel, seq=seq),
        grid=(2,),
        in_specs=[
            full((1, hid)), full((LANES, hid)), full((LANES, hid)), full((1, hid)),
            full((hid, hid)), full((1, hid)),
            pl.BlockSpec((hid, 2 * HY_W), lambda o: (0, o)),
            full((1, hid)),
        ],
        out_specs=pl.BlockSpec((seq, 2 * HY_W), lambda o: (0, o)),
        out_shape=jax.ShapeDtypeStruct((seq, 4 * HY_W), F32),
        compiler_params=_params(("arbitrary",)),
        name="hy_filter",
    )(w1t, w1c, w1s, b1, w2, b2, w3, fr)


def _hy_spec_kernel(h_ref, c_ref, s_ref, rc_ref, rs_ref, o_ref, hb_ref, *, seq, tf):
    f = pl.program_id(1)

    @pl.when(f == 0)
    def _():
        hb_ref[...] = h_ref[...].astype(BF16)

    cm = c_ref[...]
    sm = s_ref[...]
    hfw = hb_ref[:, :HY_W]
    hbw = hb_ref[:, HY_W:]
    pf = _mm(cm, hfw)
    pb = _mm(cm, hbw)
    qf = _mm(sm, hfw)
    qb = _mm(sm, hbw)
    reps = HY_W // LANES
    cf = jnp.concatenate([rc_ref[...]] * reps, axis=1)
    sf = jnp.concatenate([rs_ref[...]] * reps, axis=1)
    kre = pf + cf * pb - sf * qb
    kim = cf * qb + sf * pb - qf
    row0 = (lax.broadcasted_iota(jnp.int32, (tf, HY_W), 0) + f * tf) == 0
    wgt = jnp.where(row0, 0.5 / seq, 1.0 / seq)
    k1 = kre * wgt
    o_ref[0] = k1
    o_ref[1] = jnp.where(row0, 0.0, kim * wgt)
    o_ref[2] = jnp.where(row0, (qf - qb) * (0.5 / seq), k1)


def _hy_spec_call(h, cmat, smat, rot_c, rot_s, *, seq):
    tf = min(512, seq)
    nf = seq // tf
    return pl.pallas_call(
        functools.partial(_hy_spec_kernel, seq=seq, tf=tf),
        grid=(2, nf),
        in_specs=[
            pl.BlockSpec((seq, 2 * HY_W), lambda o, f: (0, o)),
            pl.BlockSpec((tf, seq), lambda o, f: (f, 0)),
            pl.BlockSpec((tf, seq), lambda o, f: (f, 0)),
            pl.BlockSpec((tf, LANES), lambda o, f: (f, 0)),
            pl.BlockSpec((tf, LANES), lambda o, f: (f, 0)),
        ],
        out_specs=pl.BlockSpec((None, 3, tf, HY_W), lambda o, f: (o, 0, f, 0)),
        out_shape=jax.ShapeDtypeStruct((2, 3, seq, HY_W), F32),
        scratch_shapes=[pltpu.VMEM((seq, 2 * HY_W), BF16)],
        compiler_params=_params(("arbitrary", "arbitrary")),
        name="hy_spectrum",
    )(h, cmat, smat, rot_c, rot_s)


def _short_conv(u, w, b, seg):
    n = u.shape[0]
    pos = lax.broadcasted_iota(jnp.int32, u.shape, 0) % seg
    prev = jnp.where(pos == 0, 0.0, pltpu.roll(u, 1, axis=0))
    nxt = jnp.where(pos == seg - 1, 0.0, pltpu.roll(u, n - 1, axis=0))
    return prev * w[0:1, :] + u * w[1:2, :] + nxt * w[2:3, :] + b


def _hy_conv_kernel(u_ref, m_ref, wu_ref, bu_ref, wm_ref, bm_ref, bias_ref, k_ref, cr_ref, sr_ref, cc_ref, sc_ref,
                    o_ref, ub_ref, acc_ref, *, conv_u, seg, nf):
    f = pl.program_id(1)
    b = pl.program_id(2)

    def load_u():
        u = u_ref[...].astype(F32)
        if conv_u:
            u = _short_conv(u, wu_ref[...], bu_ref[...], seg)
        return u

    @pl.when(f == 0)
    def _():
        ub_ref[b] = load_u().astype(BF16)

    ub = ub_ref[b]
    p = _mm(cr_ref[...], ub)
    q = _mm(sr_ref[...], ub)
    k1 = k_ref[0]
    k2 = k_ref[1]
    k3 = k_ref[2]
    av = (p * k1 + q * k2).astype(BF16)
    bv = (q * k3 - p * k2).astype(BF16)
    part = _mm(cc_ref[...], av) + _mm(sc_ref[...], bv)

    @pl.when(f == 0)
    def _():
        acc_ref[b] = part

    @pl.when(f > 0)
    def _():
        acc_ref[b] += part

    @pl.when(f == nf - 1)
    def _():
        u = load_u()
        m = _short_conv(m_ref[...].astype(F32), wm_ref[...], bm_ref[...], seg)
        o_ref[...] = (m * (acc_ref[b] + bias_ref[...] * u)).astype(o_ref.dtype)


def _hy_conv_call(u_arr, u_col, m_arr, m_col, conv_w, conv_b, u_sect, m_sect, bias, kspec, cmat, smat, smat_inv,
                  *, batch, seq, seg, conv_u):
    t = u_arr.shape[0]
    tc = 256
    nch = HY_W // tc
    tf = min(512, seq)
    nf = seq // tf
    ucb, mcb = u_col // tc, m_col // tc
    usb, msb = u_sect * nch, m_sect * nch
    return pl.pallas_call(
        functools.partial(_hy_conv_kernel, conv_u=conv_u, seg=seg, nf=nf),
        grid=(nch, nf, batch),
        in_specs=[
            pl.BlockSpec((seq, tc), lambda ch, f, b: (b, ucb + ch)),
            pl.BlockSpec((seq, tc), lambda ch, f, b: (b, mcb + ch)),
            pl.BlockSpec((3, tc), lambda ch, f, b: (0, usb + ch)),
            pl.BlockSpec((1, tc), lambda ch, f, b: (0, usb + ch)),
            pl.BlockSpec((3, tc), lambda ch, f, b: (0, msb + ch)),
            pl.BlockSpec((1, tc), lambda ch, f, b: (0, msb + ch)),
            pl.BlockSpec((1, tc), lambda ch, f, b: (0, ch)),
            pl.BlockSpec((3, tf, tc), lambda ch, f, b: (0, f, ch)),
            pl.BlockSpec((tf, seq), lambda ch, f, b: (f, 0)),
            pl.BlockSpec((tf, seq), lambda ch, f, b: (f, 0)),
            pl.BlockSpec((seq, tf), lambda ch, f, b: (0, f)),
            pl.BlockSpec((seq, tf), lambda ch, f, b: (0, f)),
        ],
        out_specs=pl.BlockSpec((seq, tc), lambda ch, f, b: (jnp.where(f == nf - 1, b, 0), ch)),
        out_shape=jax.ShapeDtypeStruct((t, HY_W), BF16),
        scratch_shapes=[pltpu.VMEM((batch, seq, tc), BF16), pltpu.VMEM((batch, seq, tc), F32)],
        compiler_params=_params(("arbitrary", "arbitrary", "arbitrary")),
        name="hy_conv",
    )(u_arr, m_arr, conv_w, conv_b, conv_w, conv_b, bias, kspec, cmat, smat, cmat, smat_inv)


def _merge_kernel(x_ref, mod_ref, yg_ref, yf_ref, yh_ref, g0_ref, g1_ref, g2_ref, wg_ref, wf_ref, wh_ref, wo_ref,
                  o_ref, mg_ref):
    @pl.when(pl.program_id(1) == 0)
    def _():
        m = jax.nn.sigmoid(g0_ref[...].astype(F32)) * _mm(yg_ref[...], wg_ref[...])
        m = m + jax.nn.sigmoid(g1_ref[...].astype(F32)) * _mm(yf_ref[...], wf_ref[...])
        m = m + jax.nn.sigmoid(g2_ref[...].astype(F32)) * _mm(yh_ref[...], wh_ref[...])
        mg_ref[...] = m.astype(BF16)

    o_ref[...] = x_ref[...] + mod_ref[5:6, :] * _mm(mg_ref[...], wo_ref[...])


def _merge_call(xs, mod, proj, y_gla, y_fnet, y_hy, wbg, wbf, wbh, wo, *, layer, rows_per_mod, fixed_row):
    t, d = xs.shape
    tm = 512
    tn = 512
    row = _mod_row_map(rows_per_mod, tm, fixed_row)
    gblk = P_GATE // d
    return pl.pallas_call(
        _merge_kernel,
        grid=(t // tm, d // tn),
        in_specs=[
            pl.BlockSpec((tm, tn), lambda i, n: (i, n)),
            pl.BlockSpec((None, None, N_ADA, tn), lambda i, n: (layer, row(i), 0, n)),
            pl.BlockSpec((tm, GLA_VT), lambda i, n: (i, 0)),
            pl.BlockSpec((tm, FNET_W), lambda i, n: (i, 0)),
            pl.BlockSpec((tm, HY_W), lambda i, n: (i, 0)),
            pl.BlockSpec((tm, d), lambda i, n: (i, gblk)),
            pl.BlockSpec((tm, d), lambda i, n: (i, gblk + 1)),
            pl.BlockSpec((tm, d), lambda i, n: (i, gblk + 2)),
            pl.BlockSpec((None, GLA_VT, d), lambda i, n: (layer, 0, 0)),
            pl.BlockSpec((None, FNET_W, d), lambda i, n: (layer, 0, 0)),
            pl.BlockSpec((None, HY_W, d), lambda i, n: (layer, 0, 0)),
            pl.BlockSpec((None, d, tn), lambda i, n: (layer, 0, n)),
        ],
        out_specs=pl.BlockSpec((tm, tn), lambda i, n: (i, n)),
        out_shape=jax.ShapeDtypeStruct((t, d), F32),
        scratch_shapes=[pltpu.VMEM((tm, d), BF16)],
        compiler_params=_params(("parallel", "arbitrary")),
        name="merge_out",
    )(xs, mod, y_gla, y_fnet, y_hy, proj, proj, proj, wbg, wbf, wbh, wo)


def _trig_tables(n, period):
    r = lax.broadcasted_iota(jnp.int32, (n, n), 0)
    c = lax.broadcasted_iota(jnp.int32, (n, n), 1)
    ang = ((r * c) % period).astype(F32) * (2.0 * math.pi / period)
    return jnp.cos(ang), jnp.sin(ang)


def _fnet_tables(seq):
    cl, sl = _trig_tables(seq, seq)
    cg, sg = _trig_tables(FNET_GW, FNET_GW)
    eye = jnp.eye(FNET_GROUPS, dtype=F32)
    return cl.astype(BF16), sl.astype(BF16), jnp.kron(eye, cg).astype(BF16), jnp.kron(eye, sg).astype(BF16)


def _hyena_tables(seq):
    cm, sm = _trig_tables(seq, 2 * seq)
    r = lax.broadcasted_iota(jnp.int32, (seq, seq), 0)
    c = lax.broadcasted_iota(jnp.int32, (seq, seq), 1)
    sm_fwd = jnp.where(r == 0, (1 - 2 * (c % 2)).astype(F32), sm)
    sm_inv = jnp.where(c == 0, (1 - 2 * (r % 2)).astype(F32), sm)
    fr = lax.broadcasted_iota(jnp.int32, (seq, LANES), 0).astype(F32) * (math.pi / seq)
    return cm.astype(BF16), sm_fwd.astype(BF16), sm_inv.astype(BF16), jnp.cos(fr), jnp.sin(fr)


def _permute_w_in(w_in):
    q, k, v, glow, r, uf, uh, gl = jnp.split(
        w_in, np.cumsum((GLA_KT, GLA_KT, GLA_VT, 2 * GLA_RANK, GLA_VT, FNET_W, 3 * HY_W))[:].tolist(), axis=-1)
    pad = jnp.zeros(w_in.shape[:-1] + (P_TOTAL - P_GLOW - 2 * GLA_RANK,), w_in.dtype)
    return jnp.concatenate([gl, v, r, uh, q, k, uf, glow, pad], axis=-1).astype(BF16)


def _split_ffn(wi, wo):
    pad = D_FF_PAD - D_FF
    wa = jnp.pad(wi[..., :D_FF].astype(BF16), ((0, 0), (0, 0), (0, pad)))
    wg = jnp.pad(wi[..., D_FF:].astype(BF16), ((0, 0), (0, 0), (0, pad)))
    wo_p = jnp.pad(wo.astype(BF16), ((0, 0), (0, pad), (0, 0)))
    return wa, wg, wo_p


def _gla_gate_weights(gla_w2, gla_gb):
    depth = gla_w2.shape[0]
    w = gla_w2.reshape(depth, 2, GLA_RANK, GLA_HEADS, GLA_DK).transpose(0, 1, 3, 2, 4)
    zf = jnp.zeros((depth, GLA_HEADS, LANES, GLA_DK), F32)
    w2f = zf.at[:, :, :GLA_RANK].set(w[:, 0]).astype(BF16)
    w2b = zf.at[:, :, GLA_RANK:2 * GLA_RANK].set(w[:, 1]).astype(BF16)
    gb = gla_gb.reshape(depth, 2, GLA_HEADS, 1, GLA_DK)
    return w2f, w2b, gb[:, 0], gb[:, 1]


def kernel(x, c, ctx, c_ctx, ada_w, ada_b, norm_w, ffn1_wi, ffn1_wo, ffn2_wi, ffn2_wo, w_in, gla_w2, gla_gb,
           gla_norm_w, hy_conv_w, hy_conv_b, hy_f1_w, hy_f1_b, hy_f2_w, hy_f2_b, hy_f3_w, hy_freq, hy_bias,
           w_br_gla, w_br_fnet, w_br_hy, w_o, final_norm_w):
    batch, seq, d = x.shape
    ctx_len = ctx.shape[1]
    xs = x.reshape(batch * seq, d)
    cs = ctx.reshape(batch * ctx_len, d)

    cond = jnp.concatenate([c, c_ctx[None, :], jnp.zeros((8 - batch - 1, d), F32)], axis=0)
    mod = _ada_call(cond, ada_w, ada_b)
    ctx_row = batch

    w_p = _permute_w_in(w_in)
    wa1, wg1, wo1 = _split_ffn(ffn1_wi, ffn1_wo)
    wa2, wg2, wo2 = _split_ffn(ffn2_wi, ffn2_wo)
    wbg, wbf, wbh, wob = (w.astype(BF16) for w in (w_br_gla, w_br_fnet, w_br_hy, w_o))
    w2f, w2b, gbf, gbb = _gla_gate_weights(gla_w2, gla_gb)
    gla_nw = gla_norm_w.reshape(DEPTH, GLA_HEADS, 1, GLA_DV)
    final_w = final_norm_w.reshape(1, d)

    fnet_tab = {n: _fnet_tables(n) for n in (seq, ctx_len)}
    hy_tab = {n: _hyena_tables(n) for n in (seq, ctx_len)}

    def mixers(proj, layer, n, seg, s0f, s0b, need_y):
        y_gla, sf, sb = _gla_call(proj, w2f[layer], w2b[layer], gbf[layer], gbb[layer], gla_nw[layer], s0f, s0b,
                                  batch=batch, seq=n)
        if not need_y:
            return None, sf, sb
        y_fnet = _fnet_call(proj, *fnet_tab[n], batch=batch, seq=n)
        cm, sm, sm_inv, rot_c, rot_s = hy_tab[n]
        w1 = hy_f1_w[layer]
        zpad = jnp.zeros((LANES - HY_BANDS, HY_HID), F32)
        h = _hy_filter_call(w1[0:1], jnp.concatenate([w1[1:1 + HY_BANDS], zpad], 0),
                            jnp.concatenate([w1[1 + HY_BANDS:], zpad], 0), hy_f1_b[layer][None, :],
                            hy_f2_w[layer], hy_f2_b[layer][None, :], hy_f3_w[layer], hy_freq[layer][None, :], seq=n)
        kspec = _hy_spec_call(h, cm, sm, rot_c, rot_s, seq=n)
        cw, cb = hy_conv_w[layer], hy_conv_b[layer][None, :]
        z = _hy_conv_call(proj, P_HY, proj, P_HY + HY_W, cw, cb, 0, 1, hy_bias[layer, 0][None, :], kspec[0], cm, sm,
                          sm_inv, batch=batch, seq=n, seg=seg, conv_u=True)
        y_hy = _hy_conv_call(z, 0, proj, P_HY + 2 * HY_W, cw, cb, 0, 2, hy_bias[layer, 1][None, :], kspec[1], cm, sm,
                             sm_inv, batch=batch, seq=n, seg=seg, conv_u=False)
        return (y_gla, y_fnet, y_hy), sf, sb

    s_zero = jnp.zeros((batch, GLA_HEADS, GLA_DV, GLA_DK), F32)
    for layer in range(DEPTH):
        last = layer == DEPTH - 1
        x_kw = dict(layer=layer, rows_per_mod=seq, fixed_row=None)
        c_kw = dict(layer=layer, rows_per_mod=ctx_len, fixed_row=ctx_row)
        xs = _ffn_call(xs, mod, norm_w, wa1, wg1, wo1, final_w, sub=0, final=False, **x_kw)
        cs = _ffn_call(cs, mod, norm_w, wa1, wg1, wo1, final_w, sub=0, final=False, **c_kw)
        proj_c = _proj_call(cs, mod, norm_w, w_p, **c_kw)
        ys_c, sf, sb = mixers(proj_c, layer, ctx_len, ctx_len, s_zero, s_zero, not last)
        if not last:
            cs = _merge_call(cs, mod, proj_c, *ys_c, wbg, wbf, wbh, wob, **c_kw)
            cs = _ffn_call(cs, mod, norm_w, wa2, wg2, wo2, final_w, sub=2, final=False, **c_kw)
        proj_x = _proj_call(xs, mod, norm_w, w_p, **x_kw)
        ys_x, _, _ = mixers(proj_x, layer, seq, GRID_W, sf, sb, True)
        xs = _merge_call(xs, mod, proj_x, *ys_x, wbg, wbf, wbh, wob, **x_kw)
        xs = _ffn_call(xs, mod, norm_w, wa2, wg2, wo2, final_w, sub=2, final=last, **x_kw)
    return xs.reshape(batch, seq, d)
```

```python
import functools
import math

import jax
import jax.numpy as jnp
from jax import lax
from jax.experimental import pallas as pl
from jax.experimental.pallas import tpu as pltpu

F32 = jnp.float32
BF16 = jnp.bfloat16

D_MODEL = 2048
DEPTH = 4
GRID_W = 64
N_ADA = 9
D_FF = 5504
GLA_HEADS = 4
GLA_DK = 128
GLA_DV = 256
GLA_KT = GLA_HEADS * GLA_DK
GLA_VT = GLA_HEADS * GLA_DV
GLA_RANK = 16
GLA_TAU = 16.0
GLA_CHUNK = 64
FNET_GROUPS = 4
FNET_GW = 128
FNET_W = FNET_GROUPS * FNET_GW
HY_W = 512
HY_BANDS = 16
HY_HID = 64
HY_FAST = 0.3
HY_SLOW = 1.5
HY_TARGET = 1e-2
EPS = 1e-6

LANES = 128
VMEM_LIMIT_BYTES = 56 * 1024 * 1024

FF_TILE = 512
FF_STEPS = -(-D_FF // FF_TILE)
FF_LAST = D_FF - FF_TILE

W_GLOW = 2 * GLA_KT + GLA_VT
W_IN_COLS = W_GLOW + 2 * GLA_RANK + GLA_VT + FNET_W + 3 * HY_W + 3 * D_MODEL
P_TILE = 1024
P_Q = 0
P_K = GLA_KT
P_V = 2 * GLA_KT
P_R = W_GLOW
P_FN = P_R + GLA_VT
P_HY = P_FN + FNET_W
P_GATE = P_HY + 3 * HY_W
P_TOTAL = P_GATE + 3 * D_MODEL
P_WIN = P_TILE + LANES


def _mm(a, b):
    return jnp.dot(a, b, preferred_element_type=F32)


def _mm_nt(a, b):
    return lax.dot_general(a, b, (((1,), (1,)), ((), ())), preferred_element_type=F32)


def _mm_tn(a, b):
    return lax.dot_general(a, b, (((0,), (0,)), ((), ())), preferred_element_type=F32)


def _split2(a):
    hi = a.astype(BF16)
    lo = (a - hi.astype(F32)).astype(BF16)
    return hi, lo


def _mm3(a, b):
    ah, al = _split2(a)
    bh, bl = _split2(b)
    return _mm(ah, bh) + (_mm(ah, bl) + _mm(al, bh))


def _silu(x):
    return x * jax.nn.sigmoid(x)


def _params(sem):
    return pltpu.CompilerParams(dimension_semantics=sem, vmem_limit_bytes=VMEM_LIMIT_BYTES)


def _modnorm(x, nw, shift, scale):
    ms = jnp.mean(x * x, axis=-1, keepdims=True)
    y = x * lax.rsqrt(ms + EPS) * nw
    return y * (1.0 + scale) + shift


def _ada_kernel(c_ref, w_ref, b_ref, o_ref):
    a = _silu(c_ref[...]).astype(BF16)
    o_ref[...] = _mm(a, w_ref[...].astype(BF16)) + b_ref[...]


def _ada_call(cond, ada_w, ada_b):
    depth, d, n = ada_w.shape
    rows = cond.shape[0]
    tn = 1024
    out = pl.pallas_call(
        _ada_kernel,
        grid=(depth, n // tn),
        in_specs=[
            pl.BlockSpec((rows, d), lambda l, j: (0, 0)),
            pl.BlockSpec((None, d, tn), lambda l, j: (l, 0, j)),
            pl.BlockSpec((None, 1, tn), lambda l, j: (l, 0, j)),
        ],
        out_specs=pl.BlockSpec((None, rows, tn), lambda l, j: (l, 0, j)),
        out_shape=jax.ShapeDtypeStruct((depth, rows, n), F32),
        compiler_params=_params(("arbitrary", "arbitrary")),
        name="ada_mod",
    )(cond, ada_w, ada_b.reshape(depth, 1, n))
    return out.reshape(depth, rows, N_ADA, d)


def _mod_row_map(rows_per_mod, tm, fixed_row):
    if fixed_row is not None:
        return lambda i: fixed_row
    per = rows_per_mod // tm
    return lambda i: i // per


def _ffn_kernel(x_ref, mod_ref, nw_ref, wa_ref, wg_ref, wo_ref, fw_ref, o_ref, xn_ref, *, sub, final):
    f = pl.program_id(1)
    tm, d = o_ref.shape

    @pl.when(f == 0)
    def _():
        xn = _modnorm(x_ref[...], nw_ref[sub:sub + 1, :], mod_ref[3 * sub:3 * sub + 1, :],
                      mod_ref[3 * sub + 1:3 * sub + 2, :])
        xn_ref[...] = xn.astype(BF16)

    xn = xn_ref[...]
    a = _mm(xn, wa_ref[...])
    g = _mm(xn, wg_ref[...])
    col = lax.broadcasted_iota(jnp.int32, (tm, FF_TILE), 1)
    shared = jnp.where(f == FF_STEPS - 1, FF_STEPS * FF_TILE - D_FF, 0)
    h = jnp.where(col >= shared, _silu(g) * a, 0.0).astype(BF16)

    for c0 in range(0, d, FF_TILE):
        part = _mm(h, wo_ref[:, c0:c0 + FF_TILE])

        @pl.when(f == 0)
        def _():
            o_ref[:, c0:c0 + FF_TILE] = part

        @pl.when(f > 0)
        def _():
            o_ref[:, c0:c0 + FF_TILE] += part

    @pl.when(f == FF_STEPS - 1)
    def _():
        y = x_ref[...] + 0.5 * mod_ref[3 * sub + 2:3 * sub + 3, :] * o_ref[...]
        if final:
            ms = jnp.mean(y * y, axis=-1, keepdims=True)
            y = y * lax.rsqrt(ms + EPS) * fw_ref[...]
        o_ref[...] = y


def _ffn_call(xs, mod, norm_w, wi, wo, final_w, *, layer, sub, rows_per_mod, fixed_row, final):
    t, d = xs.shape
    tm = 1024
    row = _mod_row_map(rows_per_mod, tm, fixed_row)
    back = (FF_STEPS * FF_TILE - D_FF) // LANES
    hid_blk = lambda f: f * (FF_TILE // LANES) - (f // (FF_STEPS - 1)) * back
    hid0 = lambda f: hid_blk(f) * LANES
    gate0 = lambda f: (D_FF // LANES + hid_blk(f)) * LANES
    return pl.pallas_call(
        functools.partial(_ffn_kernel, sub=sub, final=final),
        grid=(t // tm, FF_STEPS),
        in_specs=[
            pl.BlockSpec((tm, d), lambda i, f: (i, 0), pipeline_mode=pl.Buffered(1)),
            pl.BlockSpec((None, None, N_ADA, d), lambda i, f: (layer, row(i), 0, 0)),
            pl.BlockSpec((None, 3, d), lambda i, f: (layer, 0, 0)),
            pl.BlockSpec((None, pl.Element(d), pl.Element(FF_TILE)), lambda i, f: (layer, 0, hid0(f))),
            pl.BlockSpec((None, pl.Element(d), pl.Element(FF_TILE)), lambda i, f: (layer, 0, gate0(f))),
            pl.BlockSpec((None, pl.Element(FF_TILE), pl.Element(d)), lambda i, f: (layer, hid0(f), 0)),
            pl.BlockSpec((1, d), lambda i, f: (0, 0)),
        ],
        out_specs=pl.BlockSpec((tm, d), lambda i, f: (i, 0)),
        out_shape=jax.ShapeDtypeStruct((t, d), F32),
        scratch_shapes=[pltpu.VMEM((tm, d), BF16)],
        compiler_params=_params(("parallel", "arbitrary")),
        name="ffn",
    )(xs, mod, norm_w, wi, wi, wo, final_w)


def _mixnorm_kernel(x_ref, mod_ref, nw_ref, wl_ref, xn_ref, gl_ref):
    xn = _modnorm(x_ref[...], nw_ref[1:2, :], mod_ref[3:4, :], mod_ref[4:5, :]).astype(BF16)
    xn_ref[...] = xn
    gl_ref[...] = _mm(xn, wl_ref[...].astype(BF16)).astype(BF16)


def _mixnorm_call(xs, mod, norm_w, w_in, *, layer, rows_per_mod, fixed_row):
    t, d = xs.shape
    tm = 512
    row = _mod_row_map(rows_per_mod, tm, fixed_row)
    return pl.pallas_call(
        _mixnorm_kernel,
        grid=(t // tm,),
        in_specs=[
            pl.BlockSpec((tm, d), lambda i: (i, 0)),
            pl.BlockSpec((None, None, N_ADA, d), lambda i: (layer, row(i), 0, 0)),
            pl.BlockSpec((None, 3, d), lambda i: (layer, 0, 0)),
            pl.BlockSpec((None, d, LANES), lambda i: (layer, 0, W_GLOW // LANES)),
        ],
        out_specs=[pl.BlockSpec((tm, d), lambda i: (i, 0)), pl.BlockSpec((tm, LANES), lambda i: (i, 0))],
        out_shape=[jax.ShapeDtypeStruct((t, d), BF16), jax.ShapeDtypeStruct((t, LANES), BF16)],
        compiler_params=_params(("parallel",)),
        name="mix_norm",
    )(xs, mod, norm_w, w_in)


P_TILES = P_TOTAL // P_TILE
P_SKIP = 2 * GLA_RANK
P_TAIL = W_IN_COLS % LANES
P_LAST_START = W_IN_COLS - P_TAIL - P_WIN
P_LAST_SHIFT = (P_TILES - 1) * P_TILE + P_SKIP - P_LAST_START


def _proj_window(n):
    back = ((P_TILES - 1) * P_TILE - P_LAST_START) // LANES
    return (n * (P_TILE // LANES) - (n // (P_TILES - 1)) * back) * LANES


def _proj_kernel(xn_ref, w_ref, wt_ref, o_ref, wb_ref):
    n = pl.program_id(0)
    first_shifted = W_GLOW // P_TILE

    @pl.when(pl.program_id(1) == 0)
    def _():
        @pl.when(n < first_shifted)
        def _():
            wb_ref[...] = w_ref[:, 0:P_TILE].astype(BF16)

        @pl.when(jnp.logical_and(n >= first_shifted, n < P_TILES - 1))
        def _():
            wb_ref[...] = w_ref[:, P_SKIP:P_SKIP + P_TILE].astype(BF16)

        @pl.when(n == P_TILES - 1)
        def _():
            wb_ref[:, 0:P_TILE - P_TAIL] = w_ref[:, P_LAST_SHIFT:P_WIN].astype(BF16)
            wb_ref[:, P_TILE - P_TAIL:P_TILE] = wt_ref[:, 0:P_TAIL].astype(BF16)

    o_ref[...] = _mm(xn_ref[...], wb_ref[...]).astype(BF16)


def _proj_call(xn, w_in, *, layer):
    t, d = xn.shape
    tm = min(1024, t)
    return pl.pallas_call(
        _proj_kernel,
        grid=(P_TILES, t // tm),
        in_specs=[
            pl.BlockSpec((tm, d), lambda n, i: (i, 0)),
            pl.BlockSpec((None, pl.Element(d), pl.Element(P_WIN)), lambda n, i: (layer, 0, _proj_window(n))),
            pl.BlockSpec((None, d, LANES), lambda n, i: (layer, 0, W_IN_COLS // LANES)),
        ],
        out_specs=pl.BlockSpec((tm, P_TILE), lambda n, i: (i, n)),
        out_shape=jax.ShapeDtypeStruct((t, P_TOTAL), BF16),
        scratch_shapes=[pltpu.VMEM((d, P_TILE), BF16)],
        compiler_params=_params(("arbitrary", "arbitrary")),
        name="mix_proj",
    )(xn, w_in, w_in)


def _log_sigmoid(x):
    return jnp.minimum(x, 0.0) - jnp.log1p(jnp.exp(-jnp.abs(x)))


def _chunk_scan(x, reverse):
    n = x.shape[0]
    pos = lax.broadcasted_iota(jnp.int32, x.shape, 0) % GLA_CHUNK
    s = 1
    while s < GLA_CHUNK:
        if reverse:
            x = x + jnp.where(pos < GLA_CHUNK - s, pltpu.roll(x, n - s, axis=0), 0.0)
        else:
            x = x + jnp.where(pos >= s, pltpu.roll(x, s, axis=0), 0.0)
        s *= 2
    return x


def _gla_kernel(q_ref, k_ref, v_ref, r_ref, gl_ref, w2f_ref, w2b_ref, gbf_ref, gbb_ref, nw_ref,
                s0f_ref, s0b_ref, y_ref, sf_ref, sb_ref, qd_ref, kd_ref, ke_ref, dec_ref, of_ref, ob_ref, *, seq):
    c = GLA_CHUNK
    n_chunks = seq // c
    glow = gl_ref[...]
    q = q_ref[...].astype(F32) * GLA_DK ** -0.5
    k = k_ref[...].astype(F32)
    for d, (w2_ref, gb_ref) in enumerate(((w2f_ref, gbf_ref), (w2b_ref, gbb_ref))):
        lg = _log_sigmoid(_mm(glow, w2_ref[...]) + gb_ref[...]) * (1.0 / GLA_TAU)
        gc = _chunk_scan(lg, reverse=d == 1)
        gc3 = gc.reshape(n_chunks, c, GLA_DK)
        gt = gc3[:, 0:1, :] if d == 1 else gc3[:, c - 1:c, :]
        qd_ref[d] = (q * jnp.exp(gc)).astype(BF16)
        kd_ref[d] = (k * jnp.exp(-gc)).astype(BF16)
        ke_ref[d] = (k.reshape(n_chunks, c, GLA_DK) * jnp.exp(gt - gc3)).reshape(seq, GLA_DK).astype(BF16)
        dec_ref[d] = jnp.exp(gt)
    sf_ref[...] = s0f_ref[...]
    sb_ref[...] = s0b_ref[...]

    ri = lax.broadcasted_iota(jnp.int32, (c, c), 0)
    ci = lax.broadcasted_iota(jnp.int32, (c, c), 1)

    def one_chunk(d, n, mask, st_ref, o_ref):
        r0 = pl.multiple_of(n * c, c)
        qd = qd_ref[d, pl.ds(r0, c), :]
        v = v_ref[pl.ds(r0, c), :]
        att = jnp.where(mask, _mm_nt(qd, kd_ref[d, pl.ds(r0, c), :]), 0.0).astype(BF16)
        st = st_ref[...]
        o_ref[pl.ds(r0, c), :] = _mm(att, v) + _mm_nt(qd, st.astype(BF16))
        st_ref[...] = st * dec_ref[d, n] + _mm_tn(v, ke_ref[d, pl.ds(r0, c), :])

    def body(n, carry):
        one_chunk(0, n, ri >= ci, sf_ref, of_ref)
        one_chunk(1, n_chunks - 1 - n, ri <= ci, sb_ref, ob_ref)
        return carry

    lax.fori_loop(0, n_chunks, body, 0, unroll=4)

    o = of_ref[...] + ob_ref[...]
    ms = jnp.mean(o * o, axis=-1, keepdims=True)
    y = o * lax.rsqrt(ms + EPS) * nw_ref[...] * _silu(r_ref[...].astype(F32))
    y_ref[...] = y.astype(BF16)


def _gla_call(proj, glow, w2f, w2b, gbf, gbb, gla_nw, s0f, s0b, *, batch, seq):
    t = proj.shape[0]
    h = GLA_HEADS
    qb, kb = P_Q // GLA_DK, P_K // GLA_DK
    vb, rb = P_V // GLA_DV, P_R // GLA_DV
    st_spec = pl.BlockSpec((None, None, GLA_DV, GLA_DK), lambda b, hh: (b, hh, 0, 0))
    st_shape = jax.ShapeDtypeStruct((batch, h, GLA_DV, GLA_DK), F32)
    return pl.pallas_call(
        functools.partial(_gla_kernel, seq=seq),
        grid=(batch, h),
        in_specs=[
            pl.BlockSpec((seq, GLA_DK), lambda b, hh: (b, qb + hh)),
            pl.BlockSpec((seq, GLA_DK), lambda b, hh: (b, kb + hh)),
            pl.BlockSpec((seq, GLA_DV), lambda b, hh: (b, vb + hh)),
            pl.BlockSpec((seq, GLA_DV), lambda b, hh: (b, rb + hh)),
            pl.BlockSpec((seq, LANES), lambda b, hh: (b, 0)),
            pl.BlockSpec((None, LANES, GLA_DK), lambda b, hh: (hh, 0, 0)),
            pl.BlockSpec((None, LANES, GLA_DK), lambda b, hh: (hh, 0, 0)),
            pl.BlockSpec((None, 1, GLA_DK), lambda b, hh: (hh, 0, 0)),
            pl.BlockSpec((None, 1, GLA_DK), lambda b, hh: (hh, 0, 0)),
            pl.BlockSpec((None, 1, GLA_DV), lambda b, hh: (hh, 0, 0)),
            st_spec,
            st_spec,
        ],
        out_specs=[pl.BlockSpec((seq, GLA_DV), lambda b, hh: (b, hh)), st_spec, st_spec],
        out_shape=[jax.ShapeDtypeStruct((t, GLA_VT), BF16), st_shape, st_shape],
        scratch_shapes=[
            pltpu.VMEM((2, seq, GLA_DK), BF16),
            pltpu.VMEM((2, seq, GLA_DK), BF16),
            pltpu.VMEM((2, seq, GLA_DK), BF16),
            pltpu.VMEM((2, seq // GLA_CHUNK, 1, GLA_DK), F32),
            pltpu.VMEM((seq, GLA_DV), F32),
            pltpu.VMEM((seq, GLA_DV), F32),
        ],
        compiler_params=_params(("parallel", "parallel")),
        name="gla",
    )(proj, proj, proj, proj, glow, w2f, w2b, gbf, gbb, gla_nw, s0f, s0b)


def _fnet_kernel(u_ref, cl_ref, sl_ref, cg_ref, sg_ref, o_ref, *, scale):
    u = u_ref[...]
    p = _mm(cl_ref[...], u).astype(BF16)
    q = _mm(sl_ref[...], u).astype(BF16)
    y = _mm(p, cg_ref[...]) - _mm(q, sg_ref[...])
    o_ref[...] = (y * scale).astype(BF16)


def _fnet_call(proj, cl, sl, cg, sg, *, batch, seq):
    t = proj.shape[0]
    tr = min(512, seq)
    nr = seq // tr
    ub = P_FN // FNET_W
    return pl.pallas_call(
        functools.partial(_fnet_kernel, scale=1.0 / math.sqrt(seq * FNET_GW)),
        grid=(nr, batch),
        in_specs=[
            pl.BlockSpec((seq, FNET_W), lambda r, b: (b, ub)),
            pl.BlockSpec((tr, seq), lambda r, b: (r, 0)),
            pl.BlockSpec((tr, seq), lambda r, b: (r, 0)),
            pl.BlockSpec((FNET_W, FNET_W), lambda r, b: (0, 0)),
            pl.BlockSpec((FNET_W, FNET_W), lambda r, b: (0, 0)),
        ],
        out_specs=pl.BlockSpec((tr, FNET_W), lambda r, b: (b * nr + r, 0)),
        out_shape=jax.ShapeDtypeStruct((t, FNET_W), BF16),
        compiler_params=_params(("arbitrary", "arbitrary")),
        name="fnet",
    )(proj, cl, sl, cg, sg)


def _hy_filter_kernel(w1t_ref, w1c_ref, w1s_ref, b1_ref, w2_ref, b2_ref, w3_ref, fr_ref, o_ref, *, seq):
    ti = lax.broadcasted_iota(jnp.int32, (seq, LANES), 0).astype(F32)
    lane = lax.broadcasted_iota(jnp.int32, (seq, LANES), 1)
    band_step = (HY_BANDS - 1 - 1e-4) / (HY_BANDS - 1)
    bands = jnp.where(lane < HY_BANDS, 1e-4 + lane.astype(F32) * band_step, 0.0)
    ang = 2.0 * math.pi * ti * bands / seq
    t_col = ti[:, 0:1] / (seq - 1.0)
    fr = fr_ref[...]
    pre1 = t_col * w1t_ref[...] + _mm3(jnp.cos(ang), w1c_ref[...]) + _mm3(-jnp.sin(ang), w1s_ref[...]) + b1_ref[...]
    h1 = jnp.sin(fr * pre1)
    h2 = jnp.sin(fr * (_mm3(h1, w2_ref[...]) + b2_ref[...]))
    ch = lax.broadcasted_iota(jnp.int32, (1, HY_W), 1).astype(F32)
    d0 = math.log(HY_TARGET) / HY_FAST
    d1 = math.log(HY_TARGET) / HY_SLOW
    deltas = jnp.abs(d0 + ch * ((d1 - d0) / (HY_W - 1)))
    win = jnp.exp(-t_col * deltas)
    ss = jnp.zeros((1, HY_W), F32)
    for lo in (0, HY_W):
        hd = _mm3(h2, w3_ref[:, lo:lo + HY_W]) * win
        o_ref[:, lo:lo + HY_W] = hd
        ss = ss + jnp.sum(hd * hd, axis=0, keepdims=True)
    inv = lax.rsqrt(ss + EPS)
    for lo in (0, HY_W):
        o_ref[:, lo:lo + HY_W] = o_ref[:, lo:lo + HY_W] * inv


def _hy_filter_call(w1t, w1c, w1s, b1, w2, b2, w3, fr, *, seq):
    hid = HY_HID
    full = lambda shape: pl.BlockSpec(shape, lambda o: (0,) * len(shape))
    return pl.pallas_call(
        functools.partial(_hy_filter_kernel, seq=seq),
        grid=(2,),
        in_specs=[
            full((1, hid)), full((LANES, hid)), full((LANES, hid)), full((1, hid)),
            full((hid, hid)), full((1, hid)),
            pl.BlockSpec((hid, 2 * HY_W), lambda o: (0, o)),
            full((1, hid)),
        ],
        out_specs=pl.BlockSpec((seq, 2 * HY_W), lambda o: (0, o)),
        out_shape=jax.ShapeDtypeStruct((seq, 4 * HY_W), F32),
        compiler_params=_params(("arbitrary",)),
        name="hy_filter",
    )(w1t, w1c, w1s, b1, w2, b2, w3, fr)


def _hy_spec_kernel(h_ref, c_ref, s_ref, rc_ref, rs_ref, o_ref, hb_ref, *, seq, tf):
    f = pl.program_id(1)

    @pl.when(f == 0)
    def _():
        hb_ref[...] = h_ref[...].astype(BF16)

    cm = c_ref[...]
    sm = s_ref[...]
    hfw = hb_ref[:, :HY_W]
    hbw = hb_ref[:, HY_W:]
    pf = _mm(cm, hfw)
    pb = _mm(cm, hbw)
    qf = _mm(sm, hfw)
    qb = _mm(sm, hbw)
    reps = HY_W // LANES
    cf = jnp.concatenate([rc_ref[...]] * reps, axis=1)
    sf = jnp.concatenate([rs_ref[...]] * reps, axis=1)
    kre = pf + cf * pb - sf * qb
    kim = cf * qb + sf * pb - qf
    row0 = (lax.broadcasted_iota(jnp.int32, (tf, HY_W), 0) + f * tf) == 0
    wgt = jnp.where(row0, 0.5 / seq, 1.0 / seq)
    k1 = kre * wgt
    o_ref[0] = k1
    o_ref[1] = jnp.where(row0, 0.0, kim * wgt)
    o_ref[2] = jnp.where(row0, (qf - qb) * (0.5 / seq), k1)


def _hy_spec_call(h, cmat, smat, rot_c, rot_s, *, seq):
    tf = min(512, seq)
    nf = seq // tf
    return pl.pallas_call(
        functools.partial(_hy_spec_kernel, seq=seq, tf=tf),
        grid=(2, nf),
        in_specs=[
            pl.BlockSpec((seq, 2 * HY_W), lambda o, f: (0, o)),
            pl.BlockSpec((tf, seq), lambda o, f: (f, 0)),
            pl.BlockSpec((tf, seq), lambda o, f: (f, 0)),
            pl.BlockSpec((tf, LANES), lambda o, f: (f, 0)),
            pl.BlockSpec((tf, LANES), lambda o, f: (f, 0)),
        ],
        out_specs=pl.BlockSpec((None, 3, tf, HY_W), lambda o, f: (o, 0, f, 0)),
        out_shape=jax.ShapeDtypeStruct((2, 3, seq, HY_W), F32),
        scratch_shapes=[pltpu.VMEM((seq, 2 * HY_W), BF16)],
        compiler_params=_params(("arbitrary", "arbitrary")),
        name="hy_spectrum",
    )(h, cmat, smat, rot_c, rot_s)


def _short_conv(u, w, b, seg):
    n = u.shape[0]
    pos = lax.broadcasted_iota(jnp.int32, u.shape, 0) % seg
    prev = jnp.where(pos == 0, 0.0, pltpu.roll(u, 1, axis=0))
    nxt = jnp.where(pos == seg - 1, 0.0, pltpu.roll(u, n - 1, axis=0))
    return prev * w[0:1, :] + u * w[1:2, :] + nxt * w[2:3, :] + b


def _hy_conv_kernel(u_ref, m_ref, wu_ref, bu_ref, wm_ref, bm_ref, bias_ref, k_ref, cr_ref, sr_ref, cc_ref, sc_ref,
                    o_ref, ub_ref, acc_ref, *, conv_u, seg, nf):
    f = pl.program_id(1)
    b = pl.program_id(2)

    def load_u():
        u = u_ref[...].astype(F32)
        if conv_u:
            u = _short_conv(u, wu_ref[...], bu_ref[...], seg)
        return u

    @pl.when(f == 0)
    def _():
        ub_ref[b] = load_u().astype(BF16)

    ub = ub_ref[b]
    p = _mm(cr_ref[...], ub)
    q = _mm(sr_ref[...], ub)
    k1 = k_ref[0]
    k2 = k_ref[1]
    k3 = k_ref[2]
    av = (p * k1 + q * k2).astype(BF16)
    bv = (q * k3 - p * k2).astype(BF16)
    part = _mm(cc_ref[...], av) + _mm(sc_ref[...], bv)

    @pl.when(f == 0)
    def _():
        acc_ref[b] = part

    @pl.when(f > 0)
    def _():
        acc_ref[b] += part

    @pl.when(f == nf - 1)
    def _():
        u = load_u()
        m = _short_conv(m_ref[...].astype(F32), wm_ref[...], bm_ref[...], seg)
        o_ref[...] = (m * (acc_ref[b] + bias_ref[...] * u)).astype(o_ref.dtype)


def _hy_conv_call(u_arr, u_col, m_arr, m_col, conv_w, conv_b, u_sect, m_sect, bias, kspec, cmat, smat, smat_inv,
                  *, batch, seq, seg, conv_u):
    t = u_arr.shape[0]
    tc = 256
    nch = HY_W // tc
    tf = min(512, seq)
    nf = seq // tf
    ucb, mcb = u_col // tc, m_col // tc
    usb, msb = u_sect * nch, m_sect * nch
    return pl.pallas_call(
        functools.partial(_hy_conv_kernel, conv_u=conv_u, seg=seg, nf=nf),
        grid=(nch, nf, batch),
        in_specs=[
            pl.BlockSpec((seq, tc), lambda ch, f, b: (b, ucb + ch)),
            pl.BlockSpec((seq, tc), lambda ch, f, b: (b, mcb + ch)),
            pl.BlockSpec((3, tc), lambda ch, f, b: (0, usb + ch)),
            pl.BlockSpec((1, tc), lambda ch, f, b: (0, usb + ch)),
            pl.BlockSpec((3, tc), lambda ch, f, b: (0, msb + ch)),
            pl.BlockSpec((1, tc), lambda ch, f, b: (0, msb + ch)),
            pl.BlockSpec((1, tc), lambda ch, f, b: (0, ch)),
            pl.BlockSpec((3, tf, tc), lambda ch, f, b: (0, f, ch)),
            pl.BlockSpec((tf, seq), lambda ch, f, b: (f, 0)),
            pl.BlockSpec((tf, seq), lambda ch, f, b: (f, 0)),
            pl.BlockSpec((seq, tf), lambda ch, f, b: (0, f)),
            pl.BlockSpec((seq, tf), lambda ch, f, b: (0, f)),
        ],
        out_specs=pl.BlockSpec((seq, tc), lambda ch, f, b: (jnp.where(f == nf - 1, b, 0), ch)),
        out_shape=jax.ShapeDtypeStruct((t, HY_W), BF16),
        scratch_shapes=[pltpu.VMEM((batch, seq, tc), BF16), pltpu.VMEM((batch, seq, tc), F32)],
        compiler_params=_params(("arbitrary", "arbitrary", "arbitrary")),
        name="hy_conv",
    )(u_arr, m_arr, conv_w, conv_b, conv_w, conv_b, bias, kspec, cmat, smat, cmat, smat_inv)


def _merge_kernel(x_ref, mod_ref, yg_ref, yf_ref, yh_ref, g0_ref, g1_ref, g2_ref, wg_ref, wf_ref, wh_ref, wo_ref,
                  o_ref, mg_ref):
    @pl.when(pl.program_id(1) == 0)
    def _():
        m = jax.nn.sigmoid(g0_ref[...].astype(F32)) * _mm(yg_ref[...], wg_ref[...])
        m = m + jax.nn.sigmoid(g1_ref[...].astype(F32)) * _mm(yf_ref[...], wf_ref[...])
        m = m + jax.nn.sigmoid(g2_ref[...].astype(F32)) * _mm(yh_ref[...], wh_ref[...])
        mg_ref[...] = m.astype(BF16)

    o_ref[...] = x_ref[...] + mod_ref[5:6, :] * _mm(mg_ref[...], wo_ref[...])


def _merge_call(xs, mod, proj, y_gla, y_fnet, y_hy, wbg, wbf, wbh, wo, *, layer, rows_per_mod, fixed_row):
    t, d = xs.shape
    tm = 512
    tn = 512
    row = _mod_row_map(rows_per_mod, tm, fixed_row)
    gate = lambda j: pl.BlockSpec((pl.Element(tm), pl.Element(d)), lambda i, n: (i * tm, P_GATE + j * d))
    return pl.pallas_call(
        _merge_kernel,
        grid=(t // tm, d // tn),
        in_specs=[
            pl.BlockSpec((tm, tn), lambda i, n: (i, n)),
            pl.BlockSpec((None, None, N_ADA, tn), lambda i, n: (layer, row(i), 0, n)),
            pl.BlockSpec((tm, GLA_VT), lambda i, n: (i, 0)),
            pl.BlockSpec((tm, FNET_W), lambda i, n: (i, 0)),
            pl.BlockSpec((tm, HY_W), lambda i, n: (i, 0)),
            gate(0),
            gate(1),
            gate(2),
            pl.BlockSpec((None, GLA_VT, d), lambda i, n: (layer, 0, 0)),
            pl.BlockSpec((None, FNET_W, d), lambda i, n: (layer, 0, 0)),
            pl.BlockSpec((None, HY_W, d), lambda i, n: (layer, 0, 0)),
            pl.BlockSpec((None, d, tn), lambda i, n: (layer, 0, n)),
        ],
        out_specs=pl.BlockSpec((tm, tn), lambda i, n: (i, n)),
        out_shape=jax.ShapeDtypeStruct((t, d), F32),
        scratch_shapes=[pltpu.VMEM((tm, d), BF16)],
        compiler_params=_params(("parallel", "arbitrary")),
        name="merge_out",
    )(xs, mod, y_gla, y_fnet, y_hy, proj, proj, proj, wbg, wbf, wbh, wo)


def _trig_tables(n, period):
    r = lax.broadcasted_iota(jnp.int32, (n, n), 0)
    c = lax.broadcasted_iota(jnp.int32, (n, n), 1)
    ang = ((r * c) % period).astype(F32) * (2.0 * math.pi / period)
    return jnp.cos(ang), jnp.sin(ang)


def _fnet_tables(seq):
    cl, sl = _trig_tables(seq, seq)
    cg, sg = _trig_tables(FNET_GW, FNET_GW)
    eye = jnp.eye(FNET_GROUPS, dtype=F32)
    return cl.astype(BF16), sl.astype(BF16), jnp.kron(eye, cg).astype(BF16), jnp.kron(eye, sg).astype(BF16)


def _hyena_tables(seq):
    cm, sm = _trig_tables(seq, 2 * seq)
    r = lax.broadcasted_iota(jnp.int32, (seq, seq), 0)
    c = lax.broadcasted_iota(jnp.int32, (seq, seq), 1)
    sm_fwd = jnp.where(r == 0, (1 - 2 * (c % 2)).astype(F32), sm)
    sm_inv = jnp.where(c == 0, (1 - 2 * (r % 2)).astype(F32), sm)
    fr = lax.broadcasted_iota(jnp.int32, (seq, LANES), 0).astype(F32) * (math.pi / seq)
    return cm.astype(BF16), sm_fwd.astype(BF16), sm_inv.astype(BF16), jnp.cos(fr), jnp.sin(fr)


def _gla_gate_weights(gla_w2, gla_gb):
    depth = gla_w2.shape[0]
    w = gla_w2.reshape(depth, 2, GLA_RANK, GLA_HEADS, GLA_DK).transpose(0, 1, 3, 2, 4)
    zf = jnp.zeros((depth, GLA_HEADS, LANES, GLA_DK), F32)
    w2f = zf.at[:, :, :GLA_RANK].set(w[:, 0]).astype(BF16)
    w2b = zf.at[:, :, GLA_RANK:2 * GLA_RANK].set(w[:, 1]).astype(BF16)
    gb = gla_gb.reshape(depth, 2, GLA_HEADS, 1, GLA_DK)
    return w2f, w2b, gb[:, 0], gb[:, 1]


def kernel(x, c, ctx, c_ctx, ada_w, ada_b, norm_w, ffn1_wi, ffn1_wo, ffn2_wi, ffn2_wo, w_in, gla_w2, gla_gb,
           gla_norm_w, hy_conv_w, hy_conv_b, hy_f1_w, hy_f1_b, hy_f2_w, hy_f2_b, hy_f3_w, hy_freq, hy_bias,
           w_br_gla, w_br_fnet, w_br_hy, w_o, final_norm_w):
    batch, seq, d = x.shape
    ctx_len = ctx.shape[1]
    xs = x.reshape(batch * seq, d)
    cs = ctx.reshape(batch * ctx_len, d)

    cond = jnp.concatenate([c, c_ctx[None, :], jnp.zeros((8 - batch - 1, d), F32)], axis=0)
    mod = _ada_call(cond, ada_w, ada_b)
    ctx_row = batch

    wi1, wo1, wi2, wo2 = (w.astype(BF16) for w in (ffn1_wi, ffn1_wo, ffn2_wi, ffn2_wo))
    wbg, wbf, wbh, wob = (w.astype(BF16) for w in (w_br_gla, w_br_fnet, w_br_hy, w_o))
    w2f, w2b, gbf, gbb = _gla_gate_weights(gla_w2, gla_gb)
    gla_nw = gla_norm_w.reshape(DEPTH, GLA_HEADS, 1, GLA_DV)
    final_w = final_norm_w.reshape(1, d)

    fnet_tab = {n: _fnet_tables(n) for n in (seq, ctx_len)}
    hy_tab = {n: _hyena_tables(n) for n in (seq, ctx_len)}

    def mixers(proj, glow, layer, n, seg, s0f, s0b, need_y):
        y_gla, sf, sb = _gla_call(proj, glow, w2f[layer], w2b[layer], gbf[layer], gbb[layer], gla_nw[layer],
                                  s0f, s0b, batch=batch, seq=n)
        if not need_y:
            return None, sf, sb
        y_fnet = _fnet_call(proj, *fnet_tab[n], batch=batch, seq=n)
        cm, sm, sm_inv, rot_c, rot_s = hy_tab[n]
        w1 = hy_f1_w[layer]
        zpad = jnp.zeros((LANES - HY_BANDS, HY_HID), F32)
        h = _hy_filter_call(w1[0:1], jnp.concatenate([w1[1:1 + HY_BANDS], zpad], 0),
                            jnp.concatenate([w1[1 + HY_BANDS:], zpad], 0), hy_f1_b[layer][None, :],
                            hy_f2_w[layer], hy_f2_b[layer][None, :], hy_f3_w[layer], hy_freq[layer][None, :], seq=n)
        kspec = _hy_spec_call(h, cm, sm, rot_c, rot_s, seq=n)
        cw, cb = hy_conv_w[layer], hy_conv_b[layer][None, :]
        z = _hy_conv_call(proj, P_HY, proj, P_HY + HY_W, cw, cb, 0, 1, hy_bias[layer, 0][None, :], kspec[0], cm, sm,
                          sm_inv, batch=batch, seq=n, seg=seg, conv_u=True)
        y_hy = _hy_conv_call(z, 0, proj, P_HY + 2 * HY_W, cw, cb, 0, 2, hy_bias[layer, 1][None, :], kspec[1], cm, sm,
                             sm_inv, batch=batch, seq=n, seg=seg, conv_u=False)
        return (y_gla, y_fnet, y_hy), sf, sb

    s_zero = jnp.zeros((batch, GLA_HEADS, GLA_DV, GLA_DK), F32)
    for layer in range(DEPTH):
        last = layer == DEPTH - 1
        x_kw = dict(layer=layer, rows_per_mod=seq, fixed_row=None)
        c_kw = dict(layer=layer, rows_per_mod=ctx_len, fixed_row=ctx_row)
        xs = _ffn_call(xs, mod, norm_w, wi1, wo1, final_w, sub=0, final=False, **x_kw)
        cs = _ffn_call(cs, mod, norm_w, wi1, wo1, final_w, sub=0, final=False, **c_kw)
        cn, glow_c = _mixnorm_call(cs, mod, norm_w, w_in, **c_kw)
        proj_c = _proj_call(cn, w_in, layer=layer)
        ys_c, sf, sb = mixers(proj_c, glow_c, layer, ctx_len, ctx_len, s_zero, s_zero, not last)
        if not last:
            cs = _merge_call(cs, mod, proj_c, *ys_c, wbg, wbf, wbh, wob, **c_kw)
            cs = _ffn_call(cs, mod, norm_w, wi2, wo2, final_w, sub=2, final=False, **c_kw)
        xn, glow_x = _mixnorm_call(xs, mod, norm_w, w_in, **x_kw)
        proj_x = _proj_call(xn, w_in, layer=layer)
        ys_x, _, _ = mixers(proj_x, glow_x, layer, seq, GRID_W, sf, sb, True)
        xs = _merge_call(xs, mod, proj_x, *ys_x, wbg, wbf, wbh, wob, **x_kw)
        xs = _ffn_call(xs, mod, norm_w, wi2, wo2, final_w, sub=2, final=last, **x_kw)
    return xs.reshape(batch, seq, d)
```

```python
import functools
import math

import jax
import jax.numpy as jnp
from jax import lax
from jax.experimental import pallas as pl
from jax.experimental.pallas import tpu as pltpu

F32 = jnp.float32
BF16 = jnp.bfloat16

D_MODEL = 2048
DEPTH = 4
GRID_W = 64
N_ADA = 9
D_FF = 5504
GLA_HEADS = 4
GLA_DK = 128
GLA_DV = 256
GLA_KT = GLA_HEADS * GLA_DK
GLA_VT = GLA_HEADS * GLA_DV
GLA_RANK = 16
GLA_TAU = 16.0
GLA_CHUNK = 64
FNET_GROUPS = 4
FNET_GW = 128
FNET_W = FNET_GROUPS * FNET_GW
HY_W = 512
HY_BANDS = 16
HY_HID = 64
HY_FAST = 0.3
HY_SLOW = 1.5
HY_TARGET = 1e-2
EPS = 1e-6

LANES = 128
VMEM_LIMIT_BYTES = 56 * 1024 * 1024

FF_TILE = 512
FF_STEPS = -(-D_FF // FF_TILE)

W_GLOW = 2 * GLA_KT + GLA_VT
W_IN_COLS = W_GLOW + 2 * GLA_RANK + GLA_VT + FNET_W + 3 * HY_W + 3 * D_MODEL
P_TILE = 1024
P_Q = 0
P_K = GLA_KT
P_V = 2 * GLA_KT
P_R = W_GLOW
P_FN = P_R + GLA_VT
P_HY = P_FN + FNET_W
P_GATE = P_HY + 3 * HY_W
P_TOTAL = P_GATE + 3 * D_MODEL


def _mm(a, b):
    return jnp.dot(a, b, preferred_element_type=F32)


def _mm_nt(a, b):
    return lax.dot_general(a, b, (((1,), (1,)), ((), ())), preferred_element_type=F32)


def _mm_tn(a, b):
    return lax.dot_general(a, b, (((0,), (0,)), ((), ())), preferred_element_type=F32)


def _split2(a):
    hi = a.astype(BF16)
    lo = (a - hi.astype(F32)).astype(BF16)
    return hi, lo


def _mm3(a, b):
    ah, al = _split2(a)
    bh, bl = _split2(b)
    return _mm(ah, bh) + (_mm(ah, bl) + _mm(al, bh))


def _silu(x):
    return x * jax.nn.sigmoid(x)


def _params(sem):
    return pltpu.CompilerParams(dimension_semantics=sem, vmem_limit_bytes=VMEM_LIMIT_BYTES)


def _modnorm(x, nw, shift, scale):
    ms = jnp.mean(x * x, axis=-1, keepdims=True)
    y = x * lax.rsqrt(ms + EPS) * nw
    return y * (1.0 + scale) + shift


def _ada_kernel(c_ref, w_ref, b_ref, o_ref):
    a = _silu(c_ref[...]).astype(BF16)
    o_ref[...] = _mm(a, w_ref[...].astype(BF16)) + b_ref[...]


def _ada_call(cond, ada_w, ada_b):
    depth, d, n = ada_w.shape
    rows = cond.shape[0]
    tn = 1024
    out = pl.pallas_call(
        _ada_kernel,
        grid=(depth, n // tn),
        in_specs=[
            pl.BlockSpec((rows, d), lambda l, j: (0, 0)),
            pl.BlockSpec((None, d, tn), lambda l, j: (l, 0, j)),
            pl.BlockSpec((None, 1, tn), lambda l, j: (l, 0, j)),
        ],
        out_specs=pl.BlockSpec((None, rows, tn), lambda l, j: (l, 0, j)),
        out_shape=jax.ShapeDtypeStruct((depth, rows, n), F32),
        compiler_params=_params(("arbitrary", "arbitrary")),
        name="ada_mod",
    )(cond, ada_w, ada_b.reshape(depth, 1, n))
    return out.reshape(depth, rows, N_ADA, d)


def _mod_row_map(rows_per_mod, tm, fixed_row):
    if fixed_row is not None:
        return lambda i: fixed_row
    per = rows_per_mod // tm
    return lambda i: i // per


def _ffn_kernel(x_ref, mod_ref, nw_ref, wa_ref, wg_ref, wo_ref, fw_ref, o_ref, xn_ref, *, sub, final):
    f = pl.program_id(1)
    tm, d = o_ref.shape

    @pl.when(f == 0)
    def _():
        xn = _modnorm(x_ref[...], nw_ref[sub:sub + 1, :], mod_ref[3 * sub:3 * sub + 1, :],
                      mod_ref[3 * sub + 1:3 * sub + 2, :])
        xn_ref[...] = xn.astype(BF16)
        o_ref[...] = jnp.zeros((tm, d), F32)

    xn = xn_ref[...]
    a = _mm(xn, wa_ref[...])
    g = _mm(xn, wg_ref[...])
    col = lax.broadcasted_iota(jnp.int32, (tm, FF_TILE), 1)
    shared = jnp.where(f == FF_STEPS - 1, FF_STEPS * FF_TILE - D_FF, 0)
    h = jnp.where(col >= shared, _silu(g) * a, 0.0).astype(BF16)
    o_ref[...] += _mm(h, wo_ref[...])

    @pl.when(f == FF_STEPS - 1)
    def _():
        y = x_ref[...] + 0.5 * mod_ref[3 * sub + 2:3 * sub + 3, :] * o_ref[...]
        if final:
            ms = jnp.mean(y * y, axis=-1, keepdims=True)
            y = y * lax.rsqrt(ms + EPS) * fw_ref[...]
        o_ref[...] = y


def _ffn_call(xs, mod, norm_w, wi, wo, final_w, *, layer, sub, rows_per_mod, fixed_row, final):
    t, d = xs.shape
    tm = 512
    row = _mod_row_map(rows_per_mod, tm, fixed_row)
    back = (FF_STEPS * FF_TILE - D_FF) // LANES
    hid_blk = lambda f: f * (FF_TILE // LANES) - (f // (FF_STEPS - 1)) * back
    hid0 = lambda f: hid_blk(f) * LANES
    gate0 = lambda f: (D_FF // LANES + hid_blk(f)) * LANES
    return pl.pallas_call(
        functools.partial(_ffn_kernel, sub=sub, final=final),
        grid=(t // tm, FF_STEPS),
        in_specs=[
            pl.BlockSpec((tm, d), lambda i, f: (i, 0)),
            pl.BlockSpec((None, None, N_ADA, d), lambda i, f: (layer, row(i), 0, 0)),
            pl.BlockSpec((None, 3, d), lambda i, f: (layer, 0, 0)),
            pl.BlockSpec((None, pl.Element(d), pl.Element(FF_TILE)), lambda i, f: (layer, 0, hid0(f))),
            pl.BlockSpec((None, pl.Element(d), pl.Element(FF_TILE)), lambda i, f: (layer, 0, gate0(f))),
            pl.BlockSpec((None, pl.Element(FF_TILE), pl.Element(d)), lambda i, f: (layer, hid0(f), 0)),
            pl.BlockSpec((1, d), lambda i, f: (0, 0)),
        ],
        out_specs=pl.BlockSpec((tm, d), lambda i, f: (i, 0)),
        out_shape=jax.ShapeDtypeStruct((t, d), F32),
        scratch_shapes=[pltpu.VMEM((tm, d), BF16)],
        compiler_params=_params(("parallel", "arbitrary")),
        name="ffn",
    )(xs, mod, norm_w, wi, wi, wo, final_w)


def _mixnorm_kernel(x_ref, mod_ref, nw_ref, wl_ref, xn_ref, gl_ref):
    xn = _modnorm(x_ref[...], nw_ref[1:2, :], mod_ref[3:4, :], mod_ref[4:5, :]).astype(BF16)
    xn_ref[...] = xn
    gl_ref[...] = _mm_nt(xn, wl_ref[...].astype(BF16)).astype(BF16)


def _mixnorm_call(xs, mod, norm_w, w_t, *, layer, rows_per_mod, fixed_row):
    t, d = xs.shape
    tm = 512
    row = _mod_row_map(rows_per_mod, tm, fixed_row)
    return pl.pallas_call(
        _mixnorm_kernel,
        grid=(t // tm,),
        in_specs=[
            pl.BlockSpec((tm, d), lambda i: (i, 0)),
            pl.BlockSpec((None, None, N_ADA, d), lambda i: (layer, row(i), 0, 0)),
            pl.BlockSpec((None, 3, d), lambda i: (layer, 0, 0)),
            pl.BlockSpec((None, LANES, d), lambda i: (layer, W_GLOW // LANES, 0)),
        ],
        out_specs=[pl.BlockSpec((tm, d), lambda i: (i, 0)), pl.BlockSpec((tm, LANES), lambda i: (i, 0))],
        out_shape=[jax.ShapeDtypeStruct((t, d), BF16), jax.ShapeDtypeStruct((t, LANES), BF16)],
        compiler_params=_params(("parallel",)),
        name="mix_norm",
    )(xs, mod, norm_w, w_t)


P_TILES = P_TOTAL // P_TILE
P_SKIP = 2 * GLA_RANK
P_GATE_TILE = P_GATE // P_TILE


def _proj_row0(n):
    sublanes = 8
    first = W_GLOW // P_TILE
    after = (n + (P_TILES - first)) // P_TILES
    return (n * (P_TILE // sublanes) + after * (P_SKIP // sublanes)) * sublanes


def _proj_kernel(xn_ref, w_ref, o_ref, wb_ref):
    @pl.when(pl.program_id(1) == 0)
    def _():
        wb_ref[...] = w_ref[...].astype(BF16)

    @pl.when(pl.program_id(0) < P_GATE_TILE)
    def _():
        o_ref[...] = _mm_nt(xn_ref[...], wb_ref[...]).astype(BF16)

    @pl.when(pl.program_id(0) >= P_GATE_TILE)
    def _():
        o_ref[...] = jax.nn.sigmoid(_mm_nt(xn_ref[...], wb_ref[...])).astype(BF16)


def _proj_call(xn, w_t, *, layer):
    t, d = xn.shape
    tm = min(1024, t)
    return pl.pallas_call(
        _proj_kernel,
        grid=(P_TILES, t // tm),
        in_specs=[
            pl.BlockSpec((tm, d), lambda n, i: (i, 0)),
            pl.BlockSpec((None, pl.Element(P_TILE), pl.Element(d)), lambda n, i: (layer, _proj_row0(n), 0)),
        ],
        out_specs=pl.BlockSpec((tm, P_TILE), lambda n, i: (i, n)),
        out_shape=jax.ShapeDtypeStruct((t, P_TOTAL), BF16),
        scratch_shapes=[pltpu.VMEM((P_TILE, d), BF16)],
        compiler_params=_params(("arbitrary", "arbitrary")),
        name="mix_proj",
    )(xn, w_t)


def _log_sigmoid(x):
    return jnp.minimum(x, 0.0) - jnp.log1p(jnp.exp(-jnp.abs(x)))


def _chunk_scan(x, reverse):
    n = x.shape[0]
    pos = lax.broadcasted_iota(jnp.int32, x.shape, 0) % GLA_CHUNK
    s = 1
    while s < GLA_CHUNK:
        if reverse:
            x = x + jnp.where(pos < GLA_CHUNK - s, pltpu.roll(x, n - s, axis=0), 0.0)
        else:
            x = x + jnp.where(pos >= s, pltpu.roll(x, s, axis=0), 0.0)
        s *= 2
    return x


def _gla_kernel(q_ref, k_ref, v_ref, r_ref, gl_ref, w2f_ref, w2b_ref, gbf_ref, gbb_ref, nw_ref,
                s0f_ref, s0b_ref, y_ref, sf_ref, sb_ref, qd_ref, kd_ref, ke_ref, dec_ref, of_ref, ob_ref, *, seq):
    c = GLA_CHUNK
    n_chunks = seq // c
    glow = gl_ref[...]
    q = q_ref[...].astype(F32) * GLA_DK ** -0.5
    k = k_ref[...].astype(F32)
    for d, (w2_ref, gb_ref) in enumerate(((w2f_ref, gbf_ref), (w2b_ref, gbb_ref))):
        lg = _log_sigmoid(_mm(glow, w2_ref[...]) + gb_ref[...]) * (1.0 / GLA_TAU)
        gc = _chunk_scan(lg, reverse=d == 1)
        gc3 = gc.reshape(n_chunks, c, GLA_DK)
        gt = gc3[:, 0:1, :] if d == 1 else gc3[:, c - 1:c, :]
        qd_ref[d] = (q * jnp.exp(gc)).astype(BF16)
        kd_ref[d] = (k * jnp.exp(-gc)).astype(BF16)
        ke_ref[d] = (k.reshape(n_chunks, c, GLA_DK) * jnp.exp(gt - gc3)).reshape(seq, GLA_DK).astype(BF16)
        dec_ref[d] = jnp.exp(gt)
    sf_ref[...] = s0f_ref[...]
    sb_ref[...] = s0b_ref[...]

    ri = lax.broadcasted_iota(jnp.int32, (c, c), 0)
    ci = lax.broadcasted_iota(jnp.int32, (c, c), 1)

    def one_chunk(d, n, mask, st_ref, o_ref):
        r0 = pl.multiple_of(n * c, c)
        qd = qd_ref[d, pl.ds(r0, c), :]
        v = v_ref[pl.ds(r0, c), :]
        att = jnp.where(mask, _mm_nt(qd, kd_ref[d, pl.ds(r0, c), :]), 0.0).astype(BF16)
        st = st_ref[...]
        o_ref[pl.ds(r0, c), :] = _mm(att, v) + _mm_nt(qd, st.astype(BF16))
        st_ref[...] = st * dec_ref[d, n] + _mm_tn(v, ke_ref[d, pl.ds(r0, c), :])

    def body(n, carry):
        one_chunk(0, n, ri >= ci, sf_ref, of_ref)
        one_chunk(1, n_chunks - 1 - n, ri <= ci, sb_ref, ob_ref)
        return carry

    lax.fori_loop(0, n_chunks, body, 0, unroll=4)

    o = of_ref[...] + ob_ref[...]
    ms = jnp.mean(o * o, axis=-1, keepdims=True)
    y = o * lax.rsqrt(ms + EPS) * nw_ref[...] * _silu(r_ref[...].astype(F32))
    y_ref[...] = y.astype(BF16)


def _gla_call(proj, glow, w2f, w2b, gbf, gbb, gla_nw, s0f, s0b, *, batch, seq):
    t = proj.shape[0]
    h = GLA_HEADS
    qb, kb = P_Q // GLA_DK, P_K // GLA_DK
    vb, rb = P_V // GLA_DV, P_R // GLA_DV
    st_spec = pl.BlockSpec((None, None, GLA_DV, GLA_DK), lambda b, hh: (b, hh, 0, 0))
    st_shape = jax.ShapeDtypeStruct((batch, h, GLA_DV, GLA_DK), F32)
    return pl.pallas_call(
        functools.partial(_gla_kernel, seq=seq),
        grid=(batch, h),
        in_specs=[
            pl.BlockSpec((seq, GLA_DK), lambda b, hh: (b, qb + hh)),
            pl.BlockSpec((seq, GLA_DK), lambda b, hh: (b, kb + hh)),
            pl.BlockSpec((seq, GLA_DV), lambda b, hh: (b, vb + hh)),
            pl.BlockSpec((seq, GLA_DV), lambda b, hh: (b, rb + hh)),
            pl.BlockSpec((seq, LANES), lambda b, hh: (b, 0)),
            pl.BlockSpec((None, LANES, GLA_DK), lambda b, hh: (hh, 0, 0)),
            pl.BlockSpec((None, LANES, GLA_DK), lambda b, hh: (hh, 0, 0)),
            pl.BlockSpec((None, 1, GLA_DK), lambda b, hh: (hh, 0, 0)),
            pl.BlockSpec((None, 1, GLA_DK), lambda b, hh: (hh, 0, 0)),
            pl.BlockSpec((None, 1, GLA_DV), lambda b, hh: (hh, 0, 0)),
            st_spec,
            st_spec,
        ],
        out_specs=[pl.BlockSpec((seq, GLA_DV), lambda b, hh: (b, hh)), st_spec, st_spec],
        out_shape=[jax.ShapeDtypeStruct((t, GLA_VT), BF16), st_shape, st_shape],
        scratch_shapes=[
            pltpu.VMEM((2, seq, GLA_DK), BF16),
            pltpu.VMEM((2, seq, GLA_DK), BF16),
            pltpu.VMEM((2, seq, GLA_DK), BF16),
            pltpu.VMEM((2, seq // GLA_CHUNK, 1, GLA_DK), F32),
            pltpu.VMEM((seq, GLA_DV), F32),
            pltpu.VMEM((seq, GLA_DV), F32),
        ],
        compiler_params=_params(("parallel", "parallel")),
        name="gla",
    )(proj, proj, proj, proj, glow, w2f, w2b, gbf, gbb, gla_nw, s0f, s0b)


def _fnet_kernel(u_ref, cl_ref, sl_ref, cg_ref, sg_ref, o_ref, *, scale):
    u = u_ref[...]
    p = _mm(cl_ref[...], u).astype(BF16)
    q = _mm(sl_ref[...], u).astype(BF16)
    y = _mm(p, cg_ref[...]) - _mm(q, sg_ref[...])
    o_ref[...] = (y * scale).astype(BF16)


def _fnet_call(proj, cl, sl, cg, sg, *, batch, seq):
    t = proj.shape[0]
    tr = min(512, seq)
    nr = seq // tr
    ub = P_FN // FNET_W
    return pl.pallas_call(
        functools.partial(_fnet_kernel, scale=1.0 / math.sqrt(seq * FNET_GW)),
        grid=(nr, batch),
        in_specs=[
            pl.BlockSpec((seq, FNET_W), lambda r, b: (b, ub)),
            pl.BlockSpec((tr, seq), lambda r, b: (r, 0)),
            pl.BlockSpec((tr, seq), lambda r, b: (r, 0)),
            pl.BlockSpec((FNET_W, FNET_W), lambda r, b: (0, 0)),
            pl.BlockSpec((FNET_W, FNET_W), lambda r, b: (0, 0)),
        ],
        out_specs=pl.BlockSpec((tr, FNET_W), lambda r, b: (b * nr + r, 0)),
        out_shape=jax.ShapeDtypeStruct((t, FNET_W), BF16),
        compiler_params=_params(("arbitrary", "arbitrary")),
        name="fnet",
    )(proj, cl, sl, cg, sg)


def _hy_filter_kernel(w1t_ref, w1c_ref, w1s_ref, b1_ref, w2_ref, b2_ref, w3_ref, fr_ref, o_ref, *, seq):
    ti = lax.broadcasted_iota(jnp.int32, (seq, LANES), 0).astype(F32)
    lane = lax.broadcasted_iota(jnp.int32, (seq, LANES), 1)
    band_step = (HY_BANDS - 1 - 1e-4) / (HY_BANDS - 1)
    bands = jnp.where(lane < HY_BANDS, 1e-4 + lane.astype(F32) * band_step, 0.0)
    ang = 2.0 * math.pi * ti * bands / seq
    t_col = ti[:, 0:1] / (seq - 1.0)
    fr = fr_ref[...]
    pre1 = t_col * w1t_ref[...] + _mm3(jnp.cos(ang), w1c_ref[...]) + _mm3(-jnp.sin(ang), w1s_ref[...]) + b1_ref[...]
    h1 = jnp.sin(fr * pre1)
    h2 = jnp.sin(fr * (_mm3(h1, w2_ref[...]) + b2_ref[...]))
    ch = lax.broadcasted_iota(jnp.int32, (1, HY_W), 1).astype(F32)
    d0 = math.log(HY_TARGET) / HY_FAST
    d1 = math.log(HY_TARGET) / HY_SLOW
    deltas = jnp.abs(d0 + ch * ((d1 - d0) / (HY_W - 1)))
    win = jnp.exp(-t_col * deltas)
    ss = jnp.zeros((1, HY_W), F32)
    for lo in (0, HY_W):
        hd = _mm3(h2, w3_ref[:, lo:lo + HY_W]) * win
        o_ref[:, lo:lo + HY_W] = hd
        ss = ss + jnp.sum(hd * hd, axis=0, keepdims=True)
    inv = lax.rsqrt(ss + EPS)
    for lo in (0, HY_W):
        o_ref[:, lo:lo + HY_W] = o_ref[:, lo:lo + HY_W] * inv


def _hy_filter_call(w1t, w1c, w1s, b1, w2, b2, w3, fr, *, seq):
    hid = HY_HID
    full = lambda shape: pl.BlockSpec(shape, lambda o: (0,) * len(shape))
    return pl.pallas_call(
        functools.partial(_hy_filter_kernel, seq=seq),
        grid=(2,),
        in_specs=[
            full((1, hid)), full((LANES, hid)), full((LANES, hid)), full((1, hid)),
            full((hid, hid)), full((1, hid)),
            pl.BlockSpec((hid, 2 * HY_W), lambda o: (0, o)),
            full((1, hid)),
        ],
        out_specs=pl.BlockSpec((seq, 2 * HY_W), lambda o: (0, o)),
        out_shape=jax.ShapeDtypeStruct((seq, 4 * HY_W), F32),
        compiler_params=_params(("arbitrary",)),
        name="hy_filter",
    )(w1t, w1c, w1s, b1, w2, b2, w3, fr)


def _hy_spec_kernel(h_ref, c_ref, s_ref, rc_ref, rs_ref, o_ref, hb_ref, *, seq, tf):
    f = pl.program_id(1)

    @pl.when(f == 0)
    def _():
        hb_ref[...] = h_ref[...].astype(BF16)

    cm = c_ref[...]
    sm = s_ref[...]
    hfw = hb_ref[:, :HY_W]
    hbw = hb_ref[:, HY_W:]
    pf = _mm(cm, hfw)
    pb = _mm(cm, hbw)
    qf = _mm(sm, hfw)
    qb = _mm(sm, hbw)
    reps = HY_W // LANES
    cf = jnp.concatenate([rc_ref[...]] * reps, axis=1)
    sf = jnp.concatenate([rs_ref[...]] * reps, axis=1)
    kre = pf + cf * pb - sf * qb
    kim = cf * qb + sf * pb - qf
    row0 = (lax.broadcasted_iota(jnp.int32, (tf, HY_W), 0) + f * tf) == 0
    wgt = jnp.where(row0, 0.5 / seq, 1.0 / seq)
    k1 = kre * wgt
    o_ref[0] = k1
    o_ref[1] = jnp.where(row0, 0.0, kim * wgt)
    o_ref[2] = jnp.where(row0, (qf - qb) * (0.5 / seq), k1)


def _hy_spec_call(h, cmat, smat, rot_c, rot_s, *, seq):
    tf = min(512, seq)
    nf = seq // tf
    return pl.pallas_call(
        functools.partial(_hy_spec_kernel, seq=seq, tf=tf),
        grid=(2, nf),
        in_specs=[
            pl.BlockSpec((seq, 2 * HY_W), lambda o, f: (0, o)),
            pl.BlockSpec((tf, seq), lambda o, f: (f, 0)),
            pl.BlockSpec((tf, seq), lambda o, f: (f, 0)),
            pl.BlockSpec((tf, LANES), lambda o, f: (f, 0)),
            pl.BlockSpec((tf, LANES), lambda o, f: (f, 0)),
        ],
        out_specs=pl.BlockSpec((None, 3, tf, HY_W), lambda o, f: (o, 0, f, 0)),
        out_shape=jax.ShapeDtypeStruct((2, 3, seq, HY_W), F32),
        scratch_shapes=[pltpu.VMEM((seq, 2 * HY_W), BF16)],
        compiler_params=_params(("arbitrary", "arbitrary")),
        name="hy_spectrum",
    )(h, cmat, smat, rot_c, rot_s)


def _short_conv(u, w, b, seg):
    n = u.shape[0]
    pos = lax.broadcasted_iota(jnp.int32, u.shape, 0) % seg
    prev = jnp.where(pos == 0, 0.0, pltpu.roll(u, 1, axis=0))
    nxt = jnp.where(pos == seg - 1, 0.0, pltpu.roll(u, n - 1, axis=0))
    return prev * w[0:1, :] + u * w[1:2, :] + nxt * w[2:3, :] + b


def _hy_conv_kernel(u_ref, m_ref, wu_ref, bu_ref, wm_ref, bm_ref, bias_ref, k_ref, cr_ref, sr_ref, cc_ref, sc_ref,
                    o_ref, ub_ref, acc_ref, *, conv_u, seg, nf):
    f = pl.program_id(1)
    b = pl.program_id(2)

    def load_u():
        u = u_ref[...].astype(F32)
        if conv_u:
            u = _short_conv(u, wu_ref[...], bu_ref[...], seg)
        return u

    @pl.when(f == 0)
    def _():
        ub_ref[b] = load_u().astype(BF16)
        acc_ref[b] = jnp.zeros(acc_ref.shape[1:], F32)

    ub = ub_ref[b]
    p = _mm(cr_ref[...], ub)
    q = _mm(sr_ref[...], ub)
    k1 = k_ref[0]
    k2 = k_ref[1]
    k3 = k_ref[2]
    av = (p * k1 + q * k2).astype(BF16)
    bv = (q * k3 - p * k2).astype(BF16)
    acc_ref[b] += _mm(cc_ref[...], av) + _mm(sc_ref[...], bv)

    @pl.when(f == nf - 1)
    def _():
        u = load_u()
        m = _short_conv(m_ref[...].astype(F32), wm_ref[...], bm_ref[...], seg)
        o_ref[...] = (m * (acc_ref[b] + bias_ref[...] * u)).astype(o_ref.dtype)


def _hy_conv_call(u_arr, u_col, m_arr, m_col, conv_w, conv_b, u_sect, m_sect, bias, kspec, cmat, smat, smat_inv,
                  *, batch, seq, seg, conv_u):
    t = u_arr.shape[0]
    tc = 256
    nch = HY_W // tc
    tf = min(512, seq)
    nf = seq // tf
    ucb, mcb = u_col // tc, m_col // tc
    usb, msb = u_sect * nch, m_sect * nch
    return pl.pallas_call(
        functools.partial(_hy_conv_kernel, conv_u=conv_u, seg=seg, nf=nf),
        grid=(nch, nf, batch),
        in_specs=[
            pl.BlockSpec((seq, tc), lambda ch, f, b: (b, ucb + ch)),
            pl.BlockSpec((seq, tc), lambda ch, f, b: (b, mcb + ch)),
            pl.BlockSpec((3, tc), lambda ch, f, b: (0, usb + ch)),
            pl.BlockSpec((1, tc), lambda ch, f, b: (0, usb + ch)),
            pl.BlockSpec((3, tc), lambda ch, f, b: (0, msb + ch)),
            pl.BlockSpec((1, tc), lambda ch, f, b: (0, msb + ch)),
            pl.BlockSpec((1, tc), lambda ch, f, b: (0, ch)),
            pl.BlockSpec((3, tf, tc), lambda ch, f, b: (0, f, ch)),
            pl.BlockSpec((tf, seq), lambda ch, f, b: (f, 0)),
            pl.BlockSpec((tf, seq), lambda ch, f, b: (f, 0)),
            pl.BlockSpec((seq, tf), lambda ch, f, b: (0, f)),
            pl.BlockSpec((seq, tf), lambda ch, f, b: (0, f)),
        ],
        out_specs=pl.BlockSpec((seq, tc), lambda ch, f, b: (jnp.where(f == nf - 1, b, 0), ch)),
        out_shape=jax.ShapeDtypeStruct((t, HY_W), BF16),
        scratch_shapes=[pltpu.VMEM((batch, seq, tc), BF16), pltpu.VMEM((batch, seq, tc), F32)],
        compiler_params=_params(("arbitrary", "arbitrary", "arbitrary")),
        name="hy_conv",
    )(u_arr, m_arr, conv_w, conv_b, conv_w, conv_b, bias, kspec, cmat, smat, cmat, smat_inv)


def _merge_kernel(x_ref, mod_ref, yg_ref, yf_ref, yh_ref, g0_ref, g1_ref, g2_ref, wg_ref, wf_ref, wh_ref, wo_ref,
                  o_ref, mg_ref):
    @pl.when(pl.program_id(1) == 0)
    def _():
        m = g0_ref[...].astype(F32) * _mm(yg_ref[...], wg_ref[...])
        m = m + g1_ref[...].astype(F32) * _mm(yf_ref[...], wf_ref[...])
        m = m + g2_ref[...].astype(F32) * _mm(yh_ref[...], wh_ref[...])
        mg_ref[...] = m.astype(BF16)

    o_ref[...] = x_ref[...] + mod_ref[5:6, :] * _mm(mg_ref[...], wo_ref[...])


def _merge_call(xs, mod, proj, y_gla, y_fnet, y_hy, wbg, wbf, wbh, wo, *, layer, rows_per_mod, fixed_row):
    t, d = xs.shape
    tm = 512
    tn = 512
    row = _mod_row_map(rows_per_mod, tm, fixed_row)
    gate = lambda j: pl.BlockSpec((pl.Element(tm), pl.Element(d)), lambda i, n: (i * tm, P_GATE + j * d))
    return pl.pallas_call(
        _merge_kernel,
        grid=(t // tm, d // tn),
        in_specs=[
            pl.BlockSpec((tm, tn), lambda i, n: (i, n)),
            pl.BlockSpec((None, None, N_ADA, tn), lambda i, n: (layer, row(i), 0, n)),
            pl.BlockSpec((tm, GLA_VT), lambda i, n: (i, 0)),
            pl.BlockSpec((tm, FNET_W), lambda i, n: (i, 0)),
            pl.BlockSpec((tm, HY_W), lambda i, n: (i, 0)),
            gate(0),
            gate(1),
            gate(2),
            pl.BlockSpec((None, GLA_VT, d), lambda i, n: (layer, 0, 0)),
            pl.BlockSpec((None, FNET_W, d), lambda i, n: (layer, 0, 0)),
            pl.BlockSpec((None, HY_W, d), lambda i, n: (layer, 0, 0)),
            pl.BlockSpec((None, d, tn), lambda i, n: (layer, 0, n)),
        ],
        out_specs=pl.BlockSpec((tm, tn), lambda i, n: (i, n)),
        out_shape=jax.ShapeDtypeStruct((t, d), F32),
        scratch_shapes=[pltpu.VMEM((tm, d), BF16)],
        compiler_params=_params(("parallel", "arbitrary")),
        name="merge_out",
    )(xs, mod, y_gla, y_fnet, y_hy, proj, proj, proj, wbg, wbf, wbh, wo)


def _trig_tables(n, period):
    split = 32
    c = lax.broadcasted_iota(jnp.int32, (1, n), 1)

    def rows(r):
        ang = ((r * c) % period).astype(F32) * (2.0 * math.pi / period)
        return jnp.cos(ang), jnp.sin(ang)

    c_lo, s_lo = rows(lax.broadcasted_iota(jnp.int32, (split, 1), 0))
    c_hi, s_hi = rows(lax.broadcasted_iota(jnp.int32, (n // split, 1), 0) * split)
    cos = c_hi[:, None, :] * c_lo[None] - s_hi[:, None, :] * s_lo[None]
    sin = s_hi[:, None, :] * c_lo[None] + c_hi[:, None, :] * s_lo[None]
    return cos.reshape(n, n), sin.reshape(n, n)


def _fnet_tables(seq):
    cl, sl = _trig_tables(seq, seq)
    cg, sg = _trig_tables(FNET_GW, FNET_GW)
    eye = jnp.eye(FNET_GROUPS, dtype=F32)
    return cl.astype(BF16), sl.astype(BF16), jnp.kron(eye, cg).astype(BF16), jnp.kron(eye, sg).astype(BF16)


def _hyena_tables(seq):
    cm, sm = _trig_tables(seq, 2 * seq)
    r = lax.broadcasted_iota(jnp.int32, (seq, seq), 0)
    c = lax.broadcasted_iota(jnp.int32, (seq, seq), 1)
    sm_fwd = jnp.where(r == 0, (1 - 2 * (c % 2)).astype(F32), sm)
    sm_inv = jnp.where(c == 0, (1 - 2 * (r % 2)).astype(F32), sm)
    fr = lax.broadcasted_iota(jnp.int32, (seq, LANES), 0).astype(F32) * (math.pi / seq)
    return cm.astype(BF16), sm_fwd.astype(BF16), sm_inv.astype(BF16), jnp.cos(fr), jnp.sin(fr)


def _gla_gate_weights(gla_w2, gla_gb):
    depth = gla_w2.shape[0]
    w = gla_w2.reshape(depth, 2, GLA_RANK, GLA_HEADS, GLA_DK).transpose(0, 1, 3, 2, 4)
    zf = jnp.zeros((depth, GLA_HEADS, LANES, GLA_DK), F32)
    w2f = zf.at[:, :, :GLA_RANK].set(w[:, 0]).astype(BF16)
    w2b = zf.at[:, :, GLA_RANK:2 * GLA_RANK].set(w[:, 1]).astype(BF16)
    gb = gla_gb.reshape(depth, 2, GLA_HEADS, 1, GLA_DK)
    return w2f, w2b, gb[:, 0], gb[:, 1]


def kernel(x, c, ctx, c_ctx, ada_w, ada_b, norm_w, ffn1_wi, ffn1_wo, ffn2_wi, ffn2_wo, w_in, gla_w2, gla_gb,
           gla_norm_w, hy_conv_w, hy_conv_b, hy_f1_w, hy_f1_b, hy_f2_w, hy_f2_b, hy_f3_w, hy_freq, hy_bias,
           w_br_gla, w_br_fnet, w_br_hy, w_o, final_norm_w):
    batch, seq, d = x.shape
    ctx_len = ctx.shape[1]
    xs = x.reshape(batch * seq, d)
    cs = ctx.reshape(batch * ctx_len, d)

    cond = jnp.concatenate([c, c_ctx[None, :], jnp.zeros((8 - batch - 1, d), F32)], axis=0)
    mod = _ada_call(cond, ada_w, ada_b)
    ctx_row = batch

    w_t = jnp.swapaxes(w_in, 1, 2)
    wi1, wo1, wi2, wo2 = (w.astype(BF16) for w in (ffn1_wi, ffn1_wo, ffn2_wi, ffn2_wo))
    wbg, wbf, wbh, wob = (w.astype(BF16) for w in (w_br_gla, w_br_fnet, w_br_hy, w_o))
    w2f, w2b, gbf, gbb = _gla_gate_weights(gla_w2, gla_gb)
    gla_nw = gla_norm_w.reshape(DEPTH, GLA_HEADS, 1, GLA_DV)
    final_w = final_norm_w.reshape(1, d)

    fnet_tab = {n: _fnet_tables(n) for n in (seq, ctx_len)}
    hy_tab = {n: _hyena_tables(n) for n in (seq, ctx_len)}

    def mixers(proj, glow, layer, n, seg, s0f, s0b, need_y):
        y_gla, sf, sb = _gla_call(proj, glow, w2f[layer], w2b[layer], gbf[layer], gbb[layer], gla_nw[layer],
                                  s0f, s0b, batch=batch, seq=n)
        if not need_y:
            return None, sf, sb
        y_fnet = _fnet_call(proj, *fnet_tab[n], batch=batch, seq=n)
        cm, sm, sm_inv, rot_c, rot_s = hy_tab[n]
        w1 = hy_f1_w[layer]
        zpad = jnp.zeros((LANES - HY_BANDS, HY_HID), F32)
        h = _hy_filter_call(w1[0:1], jnp.concatenate([w1[1:1 + HY_BANDS], zpad], 0),
                            jnp.concatenate([w1[1 + HY_BANDS:], zpad], 0), hy_f1_b[layer][None, :],
                            hy_f2_w[layer], hy_f2_b[layer][None, :], hy_f3_w[layer], hy_freq[layer][None, :], seq=n)
        kspec = _hy_spec_call(h, cm, sm, rot_c, rot_s, seq=n)
        cw, cb = hy_conv_w[layer], hy_conv_b[layer][None, :]
        z = _hy_conv_call(proj, P_HY, proj, P_HY + HY_W, cw, cb, 0, 1, hy_bias[layer, 0][None, :], kspec[0], cm, sm,
                          sm_inv, batch=batch, seq=n, seg=seg, conv_u=True)
        y_hy = _hy_conv_call(z, 0, proj, P_HY + 2 * HY_W, cw, cb, 0, 2, hy_bias[layer, 1][None, :], kspec[1], cm, sm,
                             sm_inv, batch=batch, seq=n, seg=seg, conv_u=False)
        return (y_gla, y_fnet, y_hy), sf, sb

    s_zero = jnp.zeros((batch, GLA_HEADS, GLA_DV, GLA_DK), F32)
    for layer in range(DEPTH):
        last = layer == DEPTH - 1
        x_kw = dict(layer=layer, rows_per_mod=seq, fixed_row=None)
        c_kw = dict(layer=layer, rows_per_mod=ctx_len, fixed_row=ctx_row)
        xs = _ffn_call(xs, mod, norm_w, wi1, wo1, final_w, sub=0, final=False, **x_kw)
        cs = _ffn_call(cs, mod, norm_w, wi1, wo1, final_w, sub=0, final=False, **c_kw)
        cn, glow_c = _mixnorm_call(cs, mod, norm_w, w_t, **c_kw)
        proj_c = _proj_call(cn, w_t, layer=layer)
        ys_c, sf, sb = mixers(proj_c, glow_c, layer, ctx_len, ctx_len, s_zero, s_zero, not last)
        if not last:
            cs = _merge_call(cs, mod, proj_c, *ys_c, wbg, wbf, wbh, wob, **c_kw)
            cs = _ffn_call(cs, mod, norm_w, wi2, wo2, final_w, sub=2, final=False, **c_kw)
        xn, glow_x = _mixnorm_call(xs, mod, norm_w, w_t, **x_kw)
        proj_x = _proj_call(xn, w_t, layer=layer)
        ys_x, _, _ = mixers(proj_x, glow_x, layer, seq, GRID_W, sf, sb, True)
        xs = _merge_call(xs, mod, proj_x, *ys_x, wbg, wbf, wbh, wob, **x_kw)
        xs = _ffn_call(xs, mod, norm_w, wi2, wo2, final_w, sub=2, final=last, **x_kw)
    return xs.reshape(batch, seq, d)
```

```python
import functools
import math

import jax
import jax.numpy as jnp
from jax import lax
from jax.experimental import pallas as pl
from jax.experimental.pallas import tpu as pltpu

F32 = jnp.float32
BF16 = jnp.bfloat16

D_MODEL = 2048
DEPTH = 4
GRID_W = 64
N_ADA = 9
D_FF = 5504
GLA_HEADS = 4
GLA_DK = 128
GLA_DV = 256
GLA_KT = GLA_HEADS * GLA_DK
GLA_VT = GLA_HEADS * GLA_DV
GLA_RANK = 16
GLA_TAU = 16.0
GLA_CHUNK = 64
FNET_GROUPS = 4
FNET_GW = 128
FNET_W = FNET_GROUPS * FNET_GW
HY_W = 512
HY_BANDS = 16
HY_HID = 64
HY_FAST = 0.3
HY_SLOW = 1.5
HY_TARGET = 1e-2
EPS = 1e-6

LANES = 128
VMEM_LIMIT_BYTES = 56 * 1024 * 1024

FF_TILE = 512
FF_STEPS = -(-D_FF // FF_TILE)

W_GLOW = 2 * GLA_KT + GLA_VT
W_IN_COLS = W_GLOW + 2 * GLA_RANK + GLA_VT + FNET_W + 3 * HY_W + 3 * D_MODEL
P_TILE = 1024
P_Q = 0
P_K = GLA_KT
P_V = 2 * GLA_KT
P_R = W_GLOW
P_FN = P_R + GLA_VT
P_HY = P_FN + FNET_W
P_GATE = P_HY + 3 * HY_W
P_TOTAL = P_GATE + 3 * D_MODEL


def _mm(a, b):
    return jnp.dot(a, b, preferred_element_type=F32)


def _mm_nt(a, b):
    return lax.dot_general(a, b, (((1,), (1,)), ((), ())), preferred_element_type=F32)


def _mm_tn(a, b):
    return lax.dot_general(a, b, (((0,), (0,)), ((), ())), preferred_element_type=F32)


def _split2(a):
    hi = a.astype(BF16)
    lo = (a - hi.astype(F32)).astype(BF16)
    return hi, lo


def _mm3(a, b):
    ah, al = _split2(a)
    bh, bl = _split2(b)
    return _mm(ah, bh) + (_mm(ah, bl) + _mm(al, bh))


def _silu(x):
    return x * jax.nn.sigmoid(x)


def _params(sem):
    return pltpu.CompilerParams(dimension_semantics=sem, vmem_limit_bytes=VMEM_LIMIT_BYTES)


def _modnorm(x, nw, shift, scale):
    ms = jnp.mean(x * x, axis=-1, keepdims=True)
    y = x * lax.rsqrt(ms + EPS) * nw
    return y * (1.0 + scale) + shift


def _ada_kernel(c_ref, w_ref, b_ref, o_ref):
    a = _silu(c_ref[...]).astype(BF16)
    o_ref[...] = _mm(a, w_ref[...].astype(BF16)) + b_ref[...]


def _ada_call(cond, ada_w, ada_b):
    depth, d, n = ada_w.shape
    rows = cond.shape[0]
    tn = 1024
    out = pl.pallas_call(
        _ada_kernel,
        grid=(depth, n // tn),
        in_specs=[
            pl.BlockSpec((rows, d), lambda l, j: (0, 0)),
            pl.BlockSpec((None, d, tn), lambda l, j: (l, 0, j)),
            pl.BlockSpec((None, 1, tn), lambda l, j: (l, 0, j)),
        ],
        out_specs=pl.BlockSpec((None, rows, tn), lambda l, j: (l, 0, j)),
        out_shape=jax.ShapeDtypeStruct((depth, rows, n), F32),
        compiler_params=_params(("arbitrary", "arbitrary")),
        name="ada_mod",
    )(cond, ada_w, ada_b.reshape(depth, 1, n))
    return out.reshape(depth, rows, N_ADA, d)


def _mod_row_map(rows_per_mod, tm, fixed_row):
    if fixed_row is not None:
        return lambda i: fixed_row
    per = rows_per_mod // tm
    return lambda i: i // per


def _ffn_kernel(x_ref, mod_ref, nw_ref, wa_ref, wg_ref, wo_ref, fw_ref, o_ref, xn_ref, *, sub, final):
    f = pl.program_id(1)
    tm, d = o_ref.shape

    @pl.when(f == 0)
    def _():
        xn = _modnorm(x_ref[...], nw_ref[sub:sub + 1, :], mod_ref[3 * sub:3 * sub + 1, :],
                      mod_ref[3 * sub + 1:3 * sub + 2, :])
        xn_ref[...] = xn.astype(BF16)
        o_ref[...] = jnp.zeros((tm, d), F32)

    xn = xn_ref[...]
    a = _mm(xn, wa_ref[...])
    g = _mm(xn, wg_ref[...])
    col = lax.broadcasted_iota(jnp.int32, (tm, FF_TILE), 1)
    shared = jnp.where(f == FF_STEPS - 1, FF_STEPS * FF_TILE - D_FF, 0)
    h = jnp.where(col >= shared, _silu(g) * a, 0.0).astype(BF16)
    o_ref[...] += _mm(h, wo_ref[...])

    @pl.when(f == FF_STEPS - 1)
    def _():
        y = x_ref[...] + 0.5 * mod_ref[3 * sub + 2:3 * sub + 3, :] * o_ref[...]
        if final:
            ms = jnp.mean(y * y, axis=-1, keepdims=True)
            y = y * lax.rsqrt(ms + EPS) * fw_ref[...]
        o_ref[...] = y


def _ffn_call(xs, mod, norm_w, wi, wo, final_w, *, layer, sub, rows_per_mod, fixed_row, final):
    t, d = xs.shape
    tm = 512
    row = _mod_row_map(rows_per_mod, tm, fixed_row)
    back = (FF_STEPS * FF_TILE - D_FF) // LANES
    hid_blk = lambda f: f * (FF_TILE // LANES) - (f // (FF_STEPS - 1)) * back
    hid0 = lambda f: hid_blk(f) * LANES
    gate0 = lambda f: (D_FF // LANES + hid_blk(f)) * LANES
    return pl.pallas_call(
        functools.partial(_ffn_kernel, sub=sub, final=final),
        grid=(t // tm, FF_STEPS),
        in_specs=[
            pl.BlockSpec((tm, d), lambda i, f: (i, 0)),
            pl.BlockSpec((None, None, N_ADA, d), lambda i, f: (layer, row(i), 0, 0)),
            pl.BlockSpec((None, 3, d), lambda i, f: (layer, 0, 0)),
            pl.BlockSpec((None, pl.Element(d), pl.Element(FF_TILE)), lambda i, f: (layer, 0, hid0(f))),
            pl.BlockSpec((None, pl.Element(d), pl.Element(FF_TILE)), lambda i, f: (layer, 0, gate0(f))),
            pl.BlockSpec((None, pl.Element(FF_TILE), pl.Element(d)), lambda i, f: (layer, hid0(f), 0)),
            pl.BlockSpec((1, d), lambda i, f: (0, 0)),
        ],
        out_specs=pl.BlockSpec((tm, d), lambda i, f: (i, 0)),
        out_shape=jax.ShapeDtypeStruct((t, d), F32),
        scratch_shapes=[pltpu.VMEM((tm, d), BF16)],
        compiler_params=_params(("parallel", "arbitrary")),
        name="ffn",
    )(xs, mod, norm_w, wi, wi, wo, final_w)


def _mixnorm_kernel(x_ref, mod_ref, nw_ref, wl_ref, xn_ref, gl_ref):
    xn = _modnorm(x_ref[...], nw_ref[1:2, :], mod_ref[3:4, :], mod_ref[4:5, :]).astype(BF16)
    xn_ref[...] = xn
    gl_ref[...] = _mm_nt(xn, wl_ref[...].astype(BF16)).astype(BF16)


def _mixnorm_call(xs, mod, norm_w, w_t, *, layer, rows_per_mod, fixed_row):
    t, d = xs.shape
    tm = 512
    row = _mod_row_map(rows_per_mod, tm, fixed_row)
    return pl.pallas_call(
        _mixnorm_kernel,
        grid=(t // tm,),
        in_specs=[
            pl.BlockSpec((tm, d), lambda i: (i, 0)),
            pl.BlockSpec((None, None, N_ADA, d), lambda i: (layer, row(i), 0, 0)),
            pl.BlockSpec((None, 3, d), lambda i: (layer, 0, 0)),
            pl.BlockSpec((None, LANES, d), lambda i: (layer, W_GLOW // LANES, 0)),
        ],
        out_specs=[pl.BlockSpec((tm, d), lambda i: (i, 0)), pl.BlockSpec((tm, LANES), lambda i: (i, 0))],
        out_shape=[jax.ShapeDtypeStruct((t, d), BF16), jax.ShapeDtypeStruct((t, LANES), BF16)],
        compiler_params=_params(("parallel",)),
        name="mix_norm",
    )(xs, mod, norm_w, w_t)


P_TILES = P_TOTAL // P_TILE
P_SKIP = 2 * GLA_RANK
P_GATE_TILE = P_GATE // P_TILE


def _proj_row0(n):
    sublanes = 8
    first = W_GLOW // P_TILE
    after = (n + (P_TILES - first)) // P_TILES
    return (n * (P_TILE // sublanes) + after * (P_SKIP // sublanes)) * sublanes


def _proj_kernel(xn_ref, w_ref, o_ref, wb_ref):
    @pl.when(pl.program_id(1) == 0)
    def _():
        wb_ref[...] = w_ref[...].astype(BF16)

    @pl.when(pl.program_id(0) < P_GATE_TILE)
    def _():
        o_ref[...] = _mm_nt(xn_ref[...], wb_ref[...]).astype(BF16)

    @pl.when(pl.program_id(0) >= P_GATE_TILE)
    def _():
        o_ref[...] = jax.nn.sigmoid(_mm_nt(xn_ref[...], wb_ref[...])).astype(BF16)


def _proj_call(xn, w_t, *, layer):
    t, d = xn.shape
    tm = min(1024, t)
    return pl.pallas_call(
        _proj_kernel,
        grid=(P_TILES, t // tm),
        in_specs=[
            pl.BlockSpec((tm, d), lambda n, i: (i, 0)),
            pl.BlockSpec((None, pl.Element(P_TILE), pl.Element(d)), lambda n, i: (layer, _proj_row0(n), 0)),
        ],
        out_specs=pl.BlockSpec((tm, P_TILE), lambda n, i: (i, n)),
        out_shape=jax.ShapeDtypeStruct((t, P_TOTAL), BF16),
        scratch_shapes=[pltpu.VMEM((P_TILE, d), BF16)],
        compiler_params=_params(("arbitrary", "arbitrary")),
        name="mix_proj",
    )(xn, w_t)


def _log_sigmoid(x):
    return jnp.minimum(x, 0.0) - jnp.log1p(jnp.exp(-jnp.abs(x)))


def _chunk_scan(x, reverse):
    n = x.shape[0]
    pos = lax.broadcasted_iota(jnp.int32, x.shape, 0) % GLA_CHUNK
    s = 1
    while s < GLA_CHUNK:
        if reverse:
            x = x + jnp.where(pos < GLA_CHUNK - s, pltpu.roll(x, n - s, axis=0), 0.0)
        else:
            x = x + jnp.where(pos >= s, pltpu.roll(x, s, axis=0), 0.0)
        s *= 2
    return x


def _gla_kernel(q_ref, k_ref, v_ref, r_ref, gl_ref, w2f_ref, w2b_ref, gbf_ref, gbb_ref, nw_ref,
                s0f_ref, s0b_ref, y_ref, sf_ref, sb_ref, qd_ref, kd_ref, ke_ref, dec_ref, of_ref, ob_ref, *, seq):
    c = GLA_CHUNK
    n_chunks = seq // c
    glow = gl_ref[...]
    q = q_ref[...].astype(F32) * GLA_DK ** -0.5
    k = k_ref[...].astype(F32)
    for d, (w2_ref, gb_ref) in enumerate(((w2f_ref, gbf_ref), (w2b_ref, gbb_ref))):
        lg = _log_sigmoid(_mm(glow, w2_ref[...]) + gb_ref[...]) * (1.0 / GLA_TAU)
        gc = _chunk_scan(lg, reverse=d == 1)
        gc3 = gc.reshape(n_chunks, c, GLA_DK)
        gt = gc3[:, 0:1, :] if d == 1 else gc3[:, c - 1:c, :]
        qd_ref[d] = (q * jnp.exp(gc)).astype(BF16)
        kd_ref[d] = (k * jnp.exp(-gc)).astype(BF16)
        ke_ref[d] = (k.reshape(n_chunks, c, GLA_DK) * jnp.exp(gt - gc3)).reshape(seq, GLA_DK).astype(BF16)
        dec_ref[d] = jnp.exp(gt)
    sf_ref[...] = s0f_ref[...]
    sb_ref[...] = s0b_ref[...]

    ri = lax.broadcasted_iota(jnp.int32, (c, c), 0)
    ci = lax.broadcasted_iota(jnp.int32, (c, c), 1)

    def one_chunk(d, n, mask, st_ref, o_ref):
        r0 = pl.multiple_of(n * c, c)
        qd = qd_ref[d, pl.ds(r0, c), :]
        v = v_ref[pl.ds(r0, c), :]
        att = jnp.where(mask, _mm_nt(qd, kd_ref[d, pl.ds(r0, c), :]), 0.0).astype(BF16)
        st = st_ref[...]
        o_ref[pl.ds(r0, c), :] = _mm(att, v) + _mm_nt(qd, st.astype(BF16))
        st_ref[...] = st * dec_ref[d, n] + _mm_tn(v, ke_ref[d, pl.ds(r0, c), :])

    def body(n, carry):
        one_chunk(0, n, ri >= ci, sf_ref, of_ref)
        one_chunk(1, n_chunks - 1 - n, ri <= ci, sb_ref, ob_ref)
        return carry

    lax.fori_loop(0, n_chunks, body, 0, unroll=min(8, n_chunks))

    o = of_ref[...] + ob_ref[...]
    ms = jnp.mean(o * o, axis=-1, keepdims=True)
    y = o * lax.rsqrt(ms + EPS) * nw_ref[...] * _silu(r_ref[...].astype(F32))
    y_ref[...] = y.astype(BF16)


def _gla_call(proj, glow, w2f, w2b, gbf, gbb, gla_nw, s0f, s0b, *, batch, seq):
    t = proj.shape[0]
    h = GLA_HEADS
    qb, kb = P_Q // GLA_DK, P_K // GLA_DK
    vb, rb = P_V // GLA_DV, P_R // GLA_DV
    st_spec = pl.BlockSpec((None, None, GLA_DV, GLA_DK), lambda b, hh: (b, hh, 0, 0))
    st_shape = jax.ShapeDtypeStruct((batch, h, GLA_DV, GLA_DK), F32)
    return pl.pallas_call(
        functools.partial(_gla_kernel, seq=seq),
        grid=(batch, h),
        in_specs=[
            pl.BlockSpec((seq, GLA_DK), lambda b, hh: (b, qb + hh)),
            pl.BlockSpec((seq, GLA_DK), lambda b, hh: (b, kb + hh)),
            pl.BlockSpec((seq, GLA_DV), lambda b, hh: (b, vb + hh)),
            pl.BlockSpec((seq, GLA_DV), lambda b, hh: (b, rb + hh)),
            pl.BlockSpec((seq, LANES), lambda b, hh: (b, 0)),
            pl.BlockSpec((None, LANES, GLA_DK), lambda b, hh: (hh, 0, 0)),
            pl.BlockSpec((None, LANES, GLA_DK), lambda b, hh: (hh, 0, 0)),
            pl.BlockSpec((None, 1, GLA_DK), lambda b, hh: (hh, 0, 0)),
            pl.BlockSpec((None, 1, GLA_DK), lambda b, hh: (hh, 0, 0)),
            pl.BlockSpec((None, 1, GLA_DV), lambda b, hh: (hh, 0, 0)),
            st_spec,
            st_spec,
        ],
        out_specs=[pl.BlockSpec((seq, GLA_DV), lambda b, hh: (b, hh)), st_spec, st_spec],
        out_shape=[jax.ShapeDtypeStruct((t, GLA_VT), BF16), st_shape, st_shape],
        scratch_shapes=[
            pltpu.VMEM((2, seq, GLA_DK), BF16),
            pltpu.VMEM((2, seq, GLA_DK), BF16),
            pltpu.VMEM((2, seq, GLA_DK), BF16),
            pltpu.VMEM((2, seq // GLA_CHUNK, 1, GLA_DK), F32),
            pltpu.VMEM((seq, GLA_DV), F32),
            pltpu.VMEM((seq, GLA_DV), F32),
        ],
        compiler_params=_params(("parallel", "parallel")),
        name="gla",
    )(proj, proj, proj, proj, glow, w2f, w2b, gbf, gbb, gla_nw, s0f, s0b)


def _fnet_kernel(u_ref, cl_ref, sl_ref, cg_ref, sg_ref, o_ref, *, scale):
    u = u_ref[...]
    p = _mm(cl_ref[...], u).astype(BF16)
    q = _mm(sl_ref[...], u).astype(BF16)
    y = _mm(p, cg_ref[...]) - _mm(q, sg_ref[...])
    o_ref[...] = (y * scale).astype(BF16)


def _fnet_call(proj, cl, sl, cg, sg, *, batch, seq):
    t = proj.shape[0]
    tr = min(512, seq)
    nr = seq // tr
    ub = P_FN // FNET_W
    return pl.pallas_call(
        functools.partial(_fnet_kernel, scale=1.0 / math.sqrt(seq * FNET_GW)),
        grid=(nr, batch),
        in_specs=[
            pl.BlockSpec((seq, FNET_W), lambda r, b: (b, ub)),
            pl.BlockSpec((tr, seq), lambda r, b: (r, 0)),
            pl.BlockSpec((tr, seq), lambda r, b: (r, 0)),
            pl.BlockSpec((FNET_W, FNET_W), lambda r, b: (0, 0)),
            pl.BlockSpec((FNET_W, FNET_W), lambda r, b: (0, 0)),
        ],
        out_specs=pl.BlockSpec((tr, FNET_W), lambda r, b: (b * nr + r, 0)),
        out_shape=jax.ShapeDtypeStruct((t, FNET_W), BF16),
        compiler_params=_params(("arbitrary", "arbitrary")),
        name="fnet",
    )(proj, cl, sl, cg, sg)


def _hy_filter_kernel(w1t_ref, w1c_ref, w1s_ref, b1_ref, w2_ref, b2_ref, w3_ref, fr_ref, o_ref, *, seq):
    ti = lax.broadcasted_iota(jnp.int32, (seq, LANES), 0).astype(F32)
    lane = lax.broadcasted_iota(jnp.int32, (seq, LANES), 1)
    band_step = (HY_BANDS - 1 - 1e-4) / (HY_BANDS - 1)
    bands = jnp.where(lane < HY_BANDS, 1e-4 + lane.astype(F32) * band_step, 0.0)
    ang = 2.0 * math.pi * ti * bands / seq
    t_col = ti[:, 0:1] / (seq - 1.0)
    fr = fr_ref[...]
    pre1 = t_col * w1t_ref[...] + _mm3(jnp.cos(ang), w1c_ref[...]) + _mm3(-jnp.sin(ang), w1s_ref[...]) + b1_ref[...]
    h1 = jnp.sin(fr * pre1)
    h2 = jnp.sin(fr * (_mm3(h1, w2_ref[...]) + b2_ref[...]))
    ch = lax.broadcasted_iota(jnp.int32, (1, HY_W), 1).astype(F32)
    d0 = math.log(HY_TARGET) / HY_FAST
    d1 = math.log(HY_TARGET) / HY_SLOW
    deltas = jnp.abs(d0 + ch * ((d1 - d0) / (HY_W - 1)))
    win = jnp.exp(-t_col * deltas)
    ss = jnp.zeros((1, HY_W), F32)
    for lo in (0, HY_W):
        hd = _mm3(h2, w3_ref[:, lo:lo + HY_W]) * win
        o_ref[:, lo:lo + HY_W] = hd
        ss = ss + jnp.sum(hd * hd, axis=0, keepdims=True)
    inv = lax.rsqrt(ss + EPS)
    for lo in (0, HY_W):
        o_ref[:, lo:lo + HY_W] = o_ref[:, lo:lo + HY_W] * inv


def _hy_filter_call(w1t, w1c, w1s, b1, w2, b2, w3, fr, *, seq):
    hid = HY_HID
    full = lambda shape: pl.BlockSpec(shape, lambda o: (0,) * len(shape))
    return pl.pallas_call(
        functools.partial(_hy_filter_kernel, seq=seq),
        grid=(2,),
        in_specs=[
            full((1, hid)), full((LANES, hid)), full((LANES, hid)), full((1, hid)),
            full((hid, hid)), full((1, hid)),
            pl.BlockSpec((hid, 2 * HY_W), lambda o: (0, o)),
            full((1, hid)),
        ],
        out_specs=pl.BlockSpec((seq, 2 * HY_W), lambda o: (0, o)),
        out_shape=jax.ShapeDtypeStruct((seq, 4 * HY_W), F32),
        compiler_params=_params(("arbitrary",)),
        name="hy_filter",
    )(w1t, w1c, w1s, b1, w2, b2, w3, fr)


def _hy_spec_kernel(h_ref, c_ref, s_ref, rc_ref, rs_ref, o_ref, hb_ref, *, seq, tf):
    f = pl.program_id(1)

    @pl.when(f == 0)
    def _():
        hb_ref[...] = h_ref[...].astype(BF16)

    cm = c_ref[...]
    sm = s_ref[...]
    hfw = hb_ref[:, :HY_W]
    hbw = hb_ref[:, HY_W:]
    pf = _mm(cm, hfw)
    pb = _mm(cm, hbw)
    qf = _mm(sm, hfw)
    qb = _mm(sm, hbw)
    reps = HY_W // LANES
    cf = jnp.concatenate([rc_ref[...]] * reps, axis=1)
    sf = jnp.concatenate([rs_ref[...]] * reps, axis=1)
    kre = pf + cf * pb - sf * qb
    kim = cf * qb + sf * pb - qf
    row0 = (lax.broadcasted_iota(jnp.int32, (tf, HY_W), 0) + f * tf) == 0
    wgt = jnp.where(row0, 0.5 / seq, 1.0 / seq)
    k1 = kre * wgt
    o_ref[0] = k1
    o_ref[1] = jnp.where(row0, 0.0, kim * wgt)
    o_ref[2] = jnp.where(row0, (qf - qb) * (0.5 / seq), k1)


def _hy_spec_call(h, cmat, smat, rot_c, rot_s, *, seq):
    tf = min(512, seq)
    nf = seq // tf
    return pl.pallas_call(
        functools.partial(_hy_spec_kernel, seq=seq, tf=tf),
        grid=(2, nf),
        in_specs=[
            pl.BlockSpec((seq, 2 * HY_W), lambda o, f: (0, o)),
            pl.BlockSpec((tf, seq), lambda o, f: (f, 0)),
            pl.BlockSpec((tf, seq), lambda o, f: (f, 0)),
            pl.BlockSpec((tf, LANES), lambda o, f: (f, 0)),
            pl.BlockSpec((tf, LANES), lambda o, f: (f, 0)),
        ],
        out_specs=pl.BlockSpec((None, 3, tf, HY_W), lambda o, f: (o, 0, f, 0)),
        out_shape=jax.ShapeDtypeStruct((2, 3, seq, HY_W), F32),
        scratch_shapes=[pltpu.VMEM((seq, 2 * HY_W), BF16)],
        compiler_params=_params(("arbitrary", "arbitrary")),
        name="hy_spectrum",
    )(h, cmat, smat, rot_c, rot_s)


def _short_conv(u, w, b, seg):
    n = u.shape[0]
    pos = lax.broadcasted_iota(jnp.int32, u.shape, 0) % seg
    prev = jnp.where(pos == 0, 0.0, pltpu.roll(u, 1, axis=0))
    nxt = jnp.where(pos == seg - 1, 0.0, pltpu.roll(u, n - 1, axis=0))
    return prev * w[0:1, :] + u * w[1:2, :] + nxt * w[2:3, :] + b


def _hy_conv_kernel(u_ref, m_ref, wu_ref, bu_ref, wm_ref, bm_ref, bias_ref, k_ref, cr_ref, sr_ref, cc_ref, sc_ref,
                    o_ref, ub_ref, acc_ref, *, conv_u, seg, nf, batch, seq):
    f = pl.program_id(1)
    tc = o_ref.shape[1]

    def load_u(b):
        u = u_ref[b * seq:(b + 1) * seq, :].astype(F32)
        if conv_u:
            u = _short_conv(u, wu_ref[...], bu_ref[...], seg)
        return u

    @pl.when(f == 0)
    def _():
        for b in range(batch):
            ub_ref[:, b * tc:(b + 1) * tc] = load_u(b).astype(BF16)
        acc_ref[...] = jnp.zeros(acc_ref.shape, F32)

    ub = ub_ref[...]
    p = _mm(cr_ref[...], ub)
    q = _mm(sr_ref[...], ub)
    k1, k2, k3 = (jnp.concatenate([k_ref[j]] * batch, axis=1) for j in range(3))
    av = (p * k1 + q * k2).astype(BF16)
    bv = (q * k3 - p * k2).astype(BF16)
    acc_ref[...] += _mm(cc_ref[...], av) + _mm(sc_ref[...], bv)

    @pl.when(f == nf - 1)
    def _():
        for b in range(batch):
            m = _short_conv(m_ref[b * seq:(b + 1) * seq, :].astype(F32), wm_ref[...], bm_ref[...], seg)
            y = acc_ref[:, b * tc:(b + 1) * tc] + bias_ref[...] * ub_ref[:, b * tc:(b + 1) * tc].astype(F32)
            o_ref[b * seq:(b + 1) * seq, :] = (m * y).astype(o_ref.dtype)


def _hy_conv_call(u_arr, u_col, m_arr, m_col, conv_w, conv_b, u_sect, m_sect, bias, kspec, cmat, smat, smat_inv,
                  *, batch, seq, seg, conv_u):
    t = u_arr.shape[0]
    tc = 256
    nch = HY_W // tc
    tf = min(256, seq)
    nf = seq // tf
    ucb, mcb = u_col // tc, m_col // tc
    usb, msb = u_sect * nch, m_sect * nch
    once = pl.Buffered(1)
    return pl.pallas_call(
        functools.partial(_hy_conv_kernel, conv_u=conv_u, seg=seg, nf=nf, batch=batch, seq=seq),
        grid=(nch, nf),
        in_specs=[
            pl.BlockSpec((t, tc), lambda ch, f: (0, ucb + ch), pipeline_mode=once),
            pl.BlockSpec((t, tc), lambda ch, f: (0, mcb + ch), pipeline_mode=once),
            pl.BlockSpec((3, tc), lambda ch, f: (0, usb + ch)),
            pl.BlockSpec((1, tc), lambda ch, f: (0, usb + ch)),
            pl.BlockSpec((3, tc), lambda ch, f: (0, msb + ch)),
            pl.BlockSpec((1, tc), lambda ch, f: (0, msb + ch)),
            pl.BlockSpec((1, tc), lambda ch, f: (0, ch)),
            pl.BlockSpec((3, tf, tc), lambda ch, f: (0, f, ch)),
            pl.BlockSpec((tf, seq), lambda ch, f: (f, 0)),
            pl.BlockSpec((tf, seq), lambda ch, f: (f, 0)),
            pl.BlockSpec((seq, tf), lambda ch, f: (0, f)),
            pl.BlockSpec((seq, tf), lambda ch, f: (0, f)),
        ],
        out_specs=pl.BlockSpec((t, tc), lambda ch, f: (0, ch)),
        out_shape=jax.ShapeDtypeStruct((t, HY_W), BF16),
        scratch_shapes=[pltpu.VMEM((seq, batch * tc), BF16), pltpu.VMEM((seq, batch * tc), F32)],
        compiler_params=_params(("arbitrary", "arbitrary")),
        name="hy_conv",
    )(u_arr, m_arr, conv_w, conv_b, conv_w, conv_b, bias, kspec, cmat, smat, cmat, smat_inv)


MERGE_CHUNK = 512


def _merge_kernel(x_ref, mod_ref, yg_ref, yf_ref, yh_ref, g0_ref, g1_ref, g2_ref, wg_ref, wf_ref, wh_ref, wo_ref,
                  o_ref, mg_ref):
    d = o_ref.shape[1]
    yg = yg_ref[...]
    yf = yf_ref[...]
    yh = yh_ref[...]
    for c0 in range(0, d, MERGE_CHUNK):
        cols = slice(c0, c0 + MERGE_CHUNK)
        m = g0_ref[:, cols].astype(F32) * _mm(yg, wg_ref[:, cols])
        m = m + g1_ref[:, cols].astype(F32) * _mm(yf, wf_ref[:, cols])
        m = m + g2_ref[:, cols].astype(F32) * _mm(yh, wh_ref[:, cols])
        mg_ref[:, cols] = m.astype(BF16)
    mg = mg_ref[...]
    for c0 in range(0, d, MERGE_CHUNK):
        cols = slice(c0, c0 + MERGE_CHUNK)
        o_ref[:, cols] = x_ref[:, cols] + mod_ref[5:6, cols] * _mm(mg, wo_ref[:, cols])


def _merge_call(xs, mod, proj, y_gla, y_fnet, y_hy, wbg, wbf, wbh, wo, *, layer, rows_per_mod, fixed_row):
    t, d = xs.shape
    tm = 512
    row = _mod_row_map(rows_per_mod, tm, fixed_row)
    gate = lambda j: pl.BlockSpec((pl.Element(tm), pl.Element(d)), lambda i: (i * tm, P_GATE + j * d))
    weight = lambda k: pl.BlockSpec((None, k, d), lambda i: (layer, 0, 0), pipeline_mode=pl.Buffered(1))
    return pl.pallas_call(
        _merge_kernel,
        grid=(t // tm,),
        in_specs=[
            pl.BlockSpec((tm, d), lambda i: (i, 0)),
            pl.BlockSpec((None, None, N_ADA, d), lambda i: (layer, row(i), 0, 0)),
            pl.BlockSpec((tm, GLA_VT), lambda i: (i, 0)),
            pl.BlockSpec((tm, FNET_W), lambda i: (i, 0)),
            pl.BlockSpec((tm, HY_W), lambda i: (i, 0)),
            gate(0),
            gate(1),
            gate(2),
            weight(GLA_VT),
            weight(FNET_W),
            weight(HY_W),
            weight(d),
        ],
        out_specs=pl.BlockSpec((tm, d), lambda i: (i, 0)),
        out_shape=jax.ShapeDtypeStruct((t, d), F32),
        scratch_shapes=[pltpu.VMEM((tm, d), BF16)],
        compiler_params=_params(("parallel",)),
        name="merge_out",
    )(xs, mod, y_gla, y_fnet, y_hy, proj, proj, proj, wbg, wbf, wbh, wo)


def _trig_tables(n, period):
    split = 32
    c = lax.broadcasted_iota(jnp.int32, (1, n), 1)

    def rows(r):
        ang = ((r * c) % period).astype(F32) * (2.0 * math.pi / period)
        return jnp.cos(ang), jnp.sin(ang)

    c_lo, s_lo = rows(lax.broadcasted_iota(jnp.int32, (split, 1), 0))
    c_hi, s_hi = rows(lax.broadcasted_iota(jnp.int32, (n // split, 1), 0) * split)
    cos = c_hi[:, None, :] * c_lo[None] - s_hi[:, None, :] * s_lo[None]
    sin = s_hi[:, None, :] * c_lo[None] + c_hi[:, None, :] * s_lo[None]
    return cos.reshape(n, n), sin.reshape(n, n)


def _fnet_tables(seq):
    cl, sl = _trig_tables(seq, seq)
    cg, sg = _trig_tables(FNET_GW, FNET_GW)
    eye = jnp.eye(FNET_GROUPS, dtype=F32)
    return cl.astype(BF16), sl.astype(BF16), jnp.kron(eye, cg).astype(BF16), jnp.kron(eye, sg).astype(BF16)


def _hyena_tables(seq):
    cm, sm = _trig_tables(seq, 2 * seq)
    r = lax.broadcasted_iota(jnp.int32, (seq, seq), 0)
    c = lax.broadcasted_iota(jnp.int32, (seq, seq), 1)
    sm_fwd = jnp.where(r == 0, (1 - 2 * (c % 2)).astype(F32), sm)
    sm_inv = jnp.where(c == 0, (1 - 2 * (r % 2)).astype(F32), sm)
    fr = lax.broadcasted_iota(jnp.int32, (seq, LANES), 0).astype(F32) * (math.pi / seq)
    return cm.astype(BF16), sm_fwd.astype(BF16), sm_inv.astype(BF16), jnp.cos(fr), jnp.sin(fr)


def _gla_gate_weights(gla_w2, gla_gb):
    depth = gla_w2.shape[0]
    w = gla_w2.reshape(depth, 2, GLA_RANK, GLA_HEADS, GLA_DK).transpose(0, 1, 3, 2, 4)
    zf = jnp.zeros((depth, GLA_HEADS, LANES, GLA_DK), F32)
    w2f = zf.at[:, :, :GLA_RANK].set(w[:, 0]).astype(BF16)
    w2b = zf.at[:, :, GLA_RANK:2 * GLA_RANK].set(w[:, 1]).astype(BF16)
    gb = gla_gb.reshape(depth, 2, GLA_HEADS, 1, GLA_DK)
    return w2f, w2b, gb[:, 0], gb[:, 1]


def kernel(x, c, ctx, c_ctx, ada_w, ada_b, norm_w, ffn1_wi, ffn1_wo, ffn2_wi, ffn2_wo, w_in, gla_w2, gla_gb,
           gla_norm_w, hy_conv_w, hy_conv_b, hy_f1_w, hy_f1_b, hy_f2_w, hy_f2_b, hy_f3_w, hy_freq, hy_bias,
           w_br_gla, w_br_fnet, w_br_hy, w_o, final_norm_w):
    batch, seq, d = x.shape
    ctx_len = ctx.shape[1]
    xs = x.reshape(batch * seq, d)
    cs = ctx.reshape(batch * ctx_len, d)

    cond = jnp.concatenate([c, c_ctx[None, :], jnp.zeros((8 - batch - 1, d), F32)], axis=0)
    mod = _ada_call(cond, ada_w, ada_b)
    ctx_row = batch

    w_t = jnp.swapaxes(w_in, 1, 2)
    wi1, wo1, wi2, wo2 = (w.astype(BF16) for w in (ffn1_wi, ffn1_wo, ffn2_wi, ffn2_wo))
    wbg, wbf, wbh, wob = (w.astype(BF16) for w in (w_br_gla, w_br_fnet, w_br_hy, w_o))
    w2f, w2b, gbf, gbb = _gla_gate_weights(gla_w2, gla_gb)
    gla_nw = gla_norm_w.reshape(DEPTH, GLA_HEADS, 1, GLA_DV)
    final_w = final_norm_w.reshape(1, d)

    fnet_tab = {n: _fnet_tables(n) for n in (seq, ctx_len)}
    hy_tab = {n: _hyena_tables(n) for n in (seq, ctx_len)}

    def mixers(proj, glow, layer, n, seg, s0f, s0b, need_y):
        y_gla, sf, sb = _gla_call(proj, glow, w2f[layer], w2b[layer], gbf[layer], gbb[layer], gla_nw[layer],
                                  s0f, s0b, batch=batch, seq=n)
        if not need_y:
            return None, sf, sb
        y_fnet = _fnet_call(proj, *fnet_tab[n], batch=batch, seq=n)
        cm, sm, sm_inv, rot_c, rot_s = hy_tab[n]
        w1 = hy_f1_w[layer]
        zpad = jnp.zeros((LANES - HY_BANDS, HY_HID), F32)
        h = _hy_filter_call(w1[0:1], jnp.concatenate([w1[1:1 + HY_BANDS], zpad], 0),
                            jnp.concatenate([w1[1 + HY_BANDS:], zpad], 0), hy_f1_b[layer][None, :],
                            hy_f2_w[layer], hy_f2_b[layer][None, :], hy_f3_w[layer], hy_freq[layer][None, :], seq=n)
        kspec = _hy_spec_call(h, cm, sm, rot_c, rot_s, seq=n)
        cw, cb = hy_conv_w[layer], hy_conv_b[layer][None, :]
        z = _hy_conv_call(proj, P_HY, proj, P_HY + HY_W, cw, cb, 0, 1, hy_bias[layer, 0][None, :], kspec[0], cm, sm,
                          sm_inv, batch=batch, seq=n, seg=seg, conv_u=True)
        y_hy = _hy_conv_call(z, 0, proj, P_HY + 2 * HY_W, cw, cb, 0, 2, hy_bias[layer, 1][None, :], kspec[1], cm, sm,
                             sm_inv, batch=batch, seq=n, seg=seg, conv_u=False)
        return (y_gla, y_fnet, y_hy), sf, sb

    s_zero = jnp.zeros((batch, GLA_HEADS, GLA_DV, GLA_DK), F32)
    for layer in range(DEPTH):
        last = layer == DEPTH - 1
        x_kw = dict(layer=layer, rows_per_mod=seq, fixed_row=None)
        c_kw = dict(layer=layer, rows_per_mod=ctx_len, fixed_row=ctx_row)
        xs = _ffn_call(xs, mod, norm_w, wi1, wo1, final_w, sub=0, final=False, **x_kw)
        cs = _ffn_call(cs, mod, norm_w, wi1, wo1, final_w, sub=0, final=False, **c_kw)
        cn, glow_c = _mixnorm_call(cs, mod, norm_w, w_t, **c_kw)
        proj_c = _proj_call(cn, w_t, layer=layer)
        ys_c, sf, sb = mixers(proj_c, glow_c, layer, ctx_len, ctx_len, s_zero, s_zero, not last)
        if not last:
            cs = _merge_call(cs, mod, proj_c, *ys_c, wbg, wbf, wbh, wob, **c_kw)
            cs = _ffn_call(cs, mod, norm_w, wi2, wo2, final_w, sub=2, final=False, **c_kw)
        xn, glow_x = _mixnorm_call(xs, mod, norm_w, w_t, **x_kw)
        proj_x = _proj_call(xn, w_t, layer=layer)
        ys_x, _, _ = mixers(proj_x, glow_x, layer, seq, GRID_W, sf, sb, True)
        xs = _merge_call(xs, mod, proj_x, *ys_x, wbg, wbf, wbh, wob, **x_kw)
        xs = _ffn_call(xs, mod, norm_w, wi2, wo2, final_w, sub=2, final=last, **x_kw)
    return xs.reshape(batch, seq, d)
```

```python
import functools
import math

import jax
import jax.numpy as jnp
from jax import lax
from jax.experimental import pallas as pl
from jax.experimental.pallas import tpu as pltpu

F32 = jnp.float32
BF16 = jnp.bfloat16

D_MODEL = 2048
DEPTH = 4
GRID_W = 64
N_ADA = 9
D_FF = 5504
GLA_HEADS = 4
GLA_DK = 128
GLA_DV = 256
GLA_KT = GLA_HEADS * GLA_DK
GLA_VT = GLA_HEADS * GLA_DV
GLA_RANK = 16
GLA_TAU = 16.0
GLA_CHUNK = 64
FNET_GROUPS = 4
FNET_GW = 128
FNET_W = FNET_GROUPS * FNET_GW
HY_W = 512
HY_BANDS = 16
HY_HID = 64
HY_FAST = 0.3
HY_SLOW = 1.5
HY_TARGET = 1e-2
EPS = 1e-6

LANES = 128
VMEM_LIMIT_BYTES = 56 * 1024 * 1024

FF_TILE = 512
FF_STEPS = -(-D_FF // FF_TILE)

W_GLOW = 2 * GLA_KT + GLA_VT
W_IN_COLS = W_GLOW + 2 * GLA_RANK + GLA_VT + FNET_W + 3 * HY_W + 3 * D_MODEL
P_TILE = 1024
P_Q = 0
P_K = GLA_KT
P_V = 2 * GLA_KT
P_R = W_GLOW
P_FN = P_R + GLA_VT
P_HY = P_FN + FNET_W
P_GATE = P_HY + 3 * HY_W
P_TOTAL = P_GATE + 3 * D_MODEL


def _mm(a, b):
    return jnp.dot(a, b, preferred_element_type=F32)


def _mm_nt(a, b):
    return lax.dot_general(a, b, (((1,), (1,)), ((), ())), preferred_element_type=F32)


def _mm_tn(a, b):
    return lax.dot_general(a, b, (((0,), (0,)), ((), ())), preferred_element_type=F32)


def _split2(a):
    hi = a.astype(BF16)
    lo = (a - hi.astype(F32)).astype(BF16)
    return hi, lo


def _mm3(a, b):
    ah, al = _split2(a)
    bh, bl = _split2(b)
    return _mm(ah, bh) + (_mm(ah, bl) + _mm(al, bh))


def _sigmoid(x):
    return 0.5 * jnp.tanh(0.5 * x) + 0.5


def _silu(x):
    return x * _sigmoid(x)


def _params(sem):
    return pltpu.CompilerParams(dimension_semantics=sem, vmem_limit_bytes=VMEM_LIMIT_BYTES)


def _modnorm(x, nw, shift, scale):
    ms = jnp.mean(x * x, axis=-1, keepdims=True)
    y = x * lax.rsqrt(ms + EPS) * nw
    return y * (1.0 + scale) + shift


def _ada_kernel(c_ref, w_ref, b_ref, o_ref):
    a = _silu(c_ref[...]).astype(BF16)
    o_ref[...] = _mm(a, w_ref[...].astype(BF16)) + b_ref[...]


def _ada_call(cond, ada_w, ada_b):
    depth, d, n = ada_w.shape
    rows = cond.shape[0]
    tn = 1024
    out = pl.pallas_call(
        _ada_kernel,
        grid=(depth, n // tn),
        in_specs=[
            pl.BlockSpec((rows, d), lambda l, j: (0, 0)),
            pl.BlockSpec((None, d, tn), lambda l, j: (l, 0, j)),
            pl.BlockSpec((None, 1, tn), lambda l, j: (l, 0, j)),
        ],
        out_specs=pl.BlockSpec((None, rows, tn), lambda l, j: (l, 0, j)),
        out_shape=jax.ShapeDtypeStruct((depth, rows, n), F32),
        compiler_params=_params(("arbitrary", "arbitrary")),
        name="ada_mod",
    )(cond, ada_w, ada_b.reshape(depth, 1, n))
    return out.reshape(depth, rows, N_ADA, d)


def _mod_row_map(rows_per_mod, tm, fixed_row):
    if fixed_row is not None:
        return lambda i: fixed_row
    per = rows_per_mod // tm
    return lambda i: i // per


def _norm_kernel(x_ref, mod_ref, nw_ref, o_ref, *, sub):
    xn = _modnorm(x_ref[...], nw_ref[sub:sub + 1, :], mod_ref[3 * sub:3 * sub + 1, :],
                  mod_ref[3 * sub + 1:3 * sub + 2, :])
    o_ref[...] = xn.astype(BF16)


def _norm_call(xs, mod, norm_w, *, layer, sub, rows_per_mod, fixed_row):
    t, d = xs.shape
    tm = 512
    row = _mod_row_map(rows_per_mod, tm, fixed_row)
    return pl.pallas_call(
        functools.partial(_norm_kernel, sub=sub),
        grid=(t // tm,),
        in_specs=[
            pl.BlockSpec((tm, d), lambda i: (i, 0)),
            pl.BlockSpec((None, None, N_ADA, d), lambda i: (layer, row(i), 0, 0)),
            pl.BlockSpec((None, 3, d), lambda i: (layer, 0, 0)),
        ],
        out_specs=pl.BlockSpec((tm, d), lambda i: (i, 0)),
        out_shape=jax.ShapeDtypeStruct((t, d), BF16),
        compiler_params=_params(("parallel",)),
        name="ffn_norm",
    )(xs, mod, norm_w)


def _ffn_up_kernel(xn_ref, wa_ref, wg_ref, h_ref, wab_ref, wgb_ref):
    @pl.when(pl.program_id(1) == 0)
    def _():
        wab_ref[...] = wa_ref[...].astype(BF16)
        wgb_ref[...] = wg_ref[...].astype(BF16)

    xn = xn_ref[...]
    a = _mm(xn, wab_ref[...])
    g = _mm(xn, wgb_ref[...])
    h_ref[...] = (_silu(g) * a).astype(BF16)


def _ffn_up_call(xn, wi, *, layer):
    t, d = xn.shape
    tm = min(1024, t)
    back = (FF_STEPS * FF_TILE - D_FF) // LANES
    hid_blk = lambda f: f * (FF_TILE // LANES) - (f // (FF_STEPS - 1)) * back
    hid0 = lambda f: hid_blk(f) * LANES
    gate0 = lambda f: (D_FF // LANES + hid_blk(f)) * LANES
    return pl.pallas_call(
        _ffn_up_kernel,
        grid=(FF_STEPS, t // tm),
        in_specs=[
            pl.BlockSpec((tm, d), lambda f, i: (i, 0)),
            pl.BlockSpec((None, pl.Element(d), pl.Element(FF_TILE)), lambda f, i: (layer, 0, hid0(f))),
            pl.BlockSpec((None, pl.Element(d), pl.Element(FF_TILE)), lambda f, i: (layer, 0, gate0(f))),
        ],
        out_specs=pl.BlockSpec((pl.Element(tm), pl.Element(FF_TILE)), lambda f, i: (i * tm, hid0(f))),
        out_shape=jax.ShapeDtypeStruct((t, D_FF), BF16),
        scratch_shapes=[pltpu.VMEM((d, FF_TILE), BF16), pltpu.VMEM((d, FF_TILE), BF16)],
        compiler_params=_params(("arbitrary", "arbitrary")),
        name="ffn_up",
    )(xn, wi, wi)


def _ffn_down_kernel(h_ref, w_ref, x_ref, mod_ref, o_ref, wb_ref, *, sub):
    @pl.when(pl.program_id(1) == 0)
    def _():
        wb_ref[...] = w_ref[...].astype(BF16)

    y = _mm(h_ref[...], wb_ref[...])
    o_ref[...] = x_ref[...] + 0.5 * mod_ref[3 * sub + 2:3 * sub + 3, :] * y


def _ffn_down_call(h, wo, xs, mod, *, layer, sub, rows_per_mod, fixed_row):
    t, d = xs.shape
    tm = 512
    tn = 512
    row = _mod_row_map(rows_per_mod, tm, fixed_row)
    return pl.pallas_call(
        functools.partial(_ffn_down_kernel, sub=sub),
        grid=(d // tn, t // tm),
        in_specs=[
            pl.BlockSpec((tm, D_FF), lambda n, i: (i, 0)),
            pl.BlockSpec((None, D_FF, tn), lambda n, i: (layer, 0, n)),
            pl.BlockSpec((tm, tn), lambda n, i: (i, n)),
            pl.BlockSpec((None, None, N_ADA, tn), lambda n, i: (layer, row(i), 0, n)),
        ],
        out_specs=pl.BlockSpec((tm, tn), lambda n, i: (i, n)),
        out_shape=jax.ShapeDtypeStruct((t, d), F32),
        scratch_shapes=[pltpu.VMEM((D_FF, tn), BF16)],
        compiler_params=_params(("arbitrary", "arbitrary")),
        name="ffn_down",
    )(h, wo, xs, mod)


def _ffn(xs, xn, mod, wi, wo, *, layer, sub, rows_per_mod, fixed_row):
    h = _ffn_up_call(xn, wi, layer=layer)
    return _ffn_down_call(h, wo, xs, mod, layer=layer, sub=sub, rows_per_mod=rows_per_mod, fixed_row=fixed_row)


def _final_norm_kernel(x_ref, w_ref, o_ref):
    x = x_ref[...]
    ms = jnp.mean(x * x, axis=-1, keepdims=True)
    o_ref[...] = x * lax.rsqrt(ms + EPS) * w_ref[...]


def _final_norm_call(xs, w):
    t, d = xs.shape
    tm = 512
    return pl.pallas_call(
        _final_norm_kernel,
        grid=(t // tm,),
        in_specs=[pl.BlockSpec((tm, d), lambda i: (i, 0)), pl.BlockSpec((1, d), lambda i: (0, 0))],
        out_specs=pl.BlockSpec((tm, d), lambda i: (i, 0)),
        out_shape=jax.ShapeDtypeStruct((t, d), F32),
        compiler_params=_params(("parallel",)),
        name="final_norm",
    )(xs, w)


def _mixnorm_kernel(x_ref, mod_ref, nw_ref, wl_ref, xn_ref, gl_ref):
    xn = _modnorm(x_ref[...], nw_ref[1:2, :], mod_ref[3:4, :], mod_ref[4:5, :]).astype(BF16)
    xn_ref[...] = xn
    gl_ref[...] = _mm_nt(xn, wl_ref[...].astype(BF16)).astype(BF16)


def _mixnorm_call(xs, mod, norm_w, w_t, *, layer, rows_per_mod, fixed_row):
    t, d = xs.shape
    tm = 512
    row = _mod_row_map(rows_per_mod, tm, fixed_row)
    return pl.pallas_call(
        _mixnorm_kernel,
        grid=(t // tm,),
        in_specs=[
            pl.BlockSpec((tm, d), lambda i: (i, 0)),
            pl.BlockSpec((None, None, N_ADA, d), lambda i: (layer, row(i), 0, 0)),
            pl.BlockSpec((None, 3, d), lambda i: (layer, 0, 0)),
            pl.BlockSpec((None, LANES, d), lambda i: (layer, W_GLOW // LANES, 0)),
        ],
        out_specs=[pl.BlockSpec((tm, d), lambda i: (i, 0)), pl.BlockSpec((tm, LANES), lambda i: (i, 0))],
        out_shape=[jax.ShapeDtypeStruct((t, d), BF16), jax.ShapeDtypeStruct((t, LANES), BF16)],
        compiler_params=_params(("parallel",)),
        name="mix_norm",
    )(xs, mod, norm_w, w_t)


P_TILES = P_TOTAL // P_TILE
P_SKIP = 2 * GLA_RANK
P_GATE_TILE = P_GATE // P_TILE


def _proj_row0(n):
    sublanes = 8
    first = W_GLOW // P_TILE
    after = (n + (P_TILES - first)) // P_TILES
    return (n * (P_TILE // sublanes) + after * (P_SKIP // sublanes)) * sublanes


def _proj_kernel(xn_ref, w_ref, o_ref, wb_ref):
    @pl.when(pl.program_id(1) == 0)
    def _():
        wb_ref[...] = w_ref[...].astype(BF16)

    @pl.when(pl.program_id(0) < P_GATE_TILE)
    def _():
        o_ref[...] = _mm_nt(xn_ref[...], wb_ref[...]).astype(BF16)

    @pl.when(pl.program_id(0) >= P_GATE_TILE)
    def _():
        o_ref[...] = _sigmoid(_mm_nt(xn_ref[...], wb_ref[...])).astype(BF16)


def _proj_call(xn, w_t, *, layer):
    t, d = xn.shape
    tm = min(1024, t)
    return pl.pallas_call(
        _proj_kernel,
        grid=(P_TILES, t // tm),
        in_specs=[
            pl.BlockSpec((tm, d), lambda n, i: (i, 0)),
            pl.BlockSpec((None, pl.Element(P_TILE), pl.Element(d)), lambda n, i: (layer, _proj_row0(n), 0)),
        ],
        out_specs=pl.BlockSpec((tm, P_TILE), lambda n, i: (i, n)),
        out_shape=jax.ShapeDtypeStruct((t, P_TOTAL), BF16),
        scratch_shapes=[pltpu.VMEM((P_TILE, d), BF16)],
        compiler_params=_params(("arbitrary", "arbitrary")),
        name="mix_proj",
    )(xn, w_t)


def _log_sigmoid(x):
    return jnp.minimum(x, 0.0) - jnp.log1p(jnp.exp(-jnp.abs(x)))


def _chunk_scan(x, reverse):
    n = x.shape[0]
    pos = lax.broadcasted_iota(jnp.int32, x.shape, 0) % GLA_CHUNK
    s = 1
    while s < GLA_CHUNK:
        if reverse:
            x = x + jnp.where(pos < GLA_CHUNK - s, pltpu.roll(x, n - s, axis=0), 0.0)
        else:
            x = x + jnp.where(pos >= s, pltpu.roll(x, s, axis=0), 0.0)
        s *= 2
    return x


def _gla_kernel(q_ref, k_ref, v_ref, r_ref, gl_ref, w2f_ref, w2b_ref, gbf_ref, gbb_ref, nw_ref,
                s0f_ref, s0b_ref, y_ref, sf_ref, sb_ref, qd_ref, kd_ref, ke_ref, dec_ref, of_ref, ob_ref, *, seq):
    c = GLA_CHUNK
    n_chunks = seq // c
    glow = gl_ref[...]
    q = q_ref[...].astype(F32) * GLA_DK ** -0.5
    k = k_ref[...].astype(F32)
    for d, (w2_ref, gb_ref) in enumerate(((w2f_ref, gbf_ref), (w2b_ref, gbb_ref))):
        lg = _log_sigmoid(_mm(glow, w2_ref[...]) + gb_ref[...]) * (1.0 / GLA_TAU)
        gc = _chunk_scan(lg, reverse=d == 1)
        gc3 = gc.reshape(n_chunks, c, GLA_DK)
        gt = gc3[:, 0:1, :] if d == 1 else gc3[:, c - 1:c, :]
        qd_ref[d] = (q * jnp.exp(gc)).astype(BF16)
        kd_ref[d] = (k * jnp.exp(-gc)).astype(BF16)
        ke_ref[d] = (k.reshape(n_chunks, c, GLA_DK) * jnp.exp(gt - gc3)).reshape(seq, GLA_DK).astype(BF16)
        dec_ref[d] = jnp.exp(gt)
    sf_ref[...] = s0f_ref[...]
    sb_ref[...] = s0b_ref[...]

    ri = lax.broadcasted_iota(jnp.int32, (c, c), 0)
    ci = lax.broadcasted_iota(jnp.int32, (c, c), 1)

    def one_chunk(d, n, mask, st_ref, o_ref):
        r0 = pl.multiple_of(n * c, c)
        qd = qd_ref[d, pl.ds(r0, c), :]
        v = v_ref[pl.ds(r0, c), :]
        att = jnp.where(mask, _mm_nt(qd, kd_ref[d, pl.ds(r0, c), :]), 0.0).astype(BF16)
        st = st_ref[...]
        o_ref[pl.ds(r0, c), :] = _mm(att, v) + _mm_nt(qd, st.astype(BF16))
        st_ref[...] = st * dec_ref[d, n] + _mm_tn(v, ke_ref[d, pl.ds(r0, c), :])

    def body(n, carry):
        one_chunk(0, n, ri >= ci, sf_ref, of_ref)
        one_chunk(1, n_chunks - 1 - n, ri <= ci, sb_ref, ob_ref)
        return carry

    lax.fori_loop(0, n_chunks, body, 0, unroll=min(8, n_chunks))

    o = of_ref[...] + ob_ref[...]
    ms = jnp.mean(o * o, axis=-1, keepdims=True)
    y = o * lax.rsqrt(ms + EPS) * nw_ref[...] * _silu(r_ref[...].astype(F32))
    y_ref[...] = y.astype(BF16)


def _gla_call(proj, glow, w2f, w2b, gbf, gbb, gla_nw, s0f, s0b, *, batch, seq):
    t = proj.shape[0]
    h = GLA_HEADS
    qb, kb = P_Q // GLA_DK, P_K // GLA_DK
    vb, rb = P_V // GLA_DV, P_R // GLA_DV
    st_spec = pl.BlockSpec((None, None, GLA_DV, GLA_DK), lambda b, hh: (b, hh, 0, 0))
    st_shape = jax.ShapeDtypeStruct((batch, h, GLA_DV, GLA_DK), F32)
    return pl.pallas_call(
        functools.partial(_gla_kernel, seq=seq),
        grid=(batch, h),
        in_specs=[
            pl.BlockSpec((seq, GLA_DK), lambda b, hh: (b, qb + hh)),
            pl.BlockSpec((seq, GLA_DK), lambda b, hh: (b, kb + hh)),
            pl.BlockSpec((seq, GLA_DV), lambda b, hh: (b, vb + hh)),
            pl.BlockSpec((seq, GLA_DV), lambda b, hh: (b, rb + hh)),
            pl.BlockSpec((seq, LANES), lambda b, hh: (b, 0)),
            pl.BlockSpec((None, LANES, GLA_DK), lambda b, hh: (hh, 0, 0)),
            pl.BlockSpec((None, LANES, GLA_DK), lambda b, hh: (hh, 0, 0)),
            pl.BlockSpec((None, 1, GLA_DK), lambda b, hh: (hh, 0, 0)),
            pl.BlockSpec((None, 1, GLA_DK), lambda b, hh: (hh, 0, 0)),
            pl.BlockSpec((None, 1, GLA_DV), lambda b, hh: (hh, 0, 0)),
            st_spec,
            st_spec,
        ],
        out_specs=[pl.BlockSpec((seq, GLA_DV), lambda b, hh: (b, hh)), st_spec, st_spec],
        out_shape=[jax.ShapeDtypeStruct((t, GLA_VT), BF16), st_shape, st_shape],
        scratch_shapes=[
            pltpu.VMEM((2, seq, GLA_DK), BF16),
            pltpu.VMEM((2, seq, GLA_DK), BF16),
            pltpu.VMEM((2, seq, GLA_DK), BF16),
            pltpu.VMEM((2, seq // GLA_CHUNK, 1, GLA_DK), F32),
            pltpu.VMEM((seq, GLA_DV), F32),
            pltpu.VMEM((seq, GLA_DV), F32),
        ],
        compiler_params=_params(("parallel", "parallel")),
        name="gla",
    )(proj, proj, proj, proj, glow, w2f, w2b, gbf, gbb, gla_nw, s0f, s0b)


def _fnet_kernel(u_ref, cl_ref, sl_ref, cg_ref, sg_ref, o_ref, *, scale):
    u = u_ref[...]
    p = _mm(cl_ref[...], u).astype(BF16)
    q = _mm(sl_ref[...], u).astype(BF16)
    y = _mm(p, cg_ref[...]) - _mm(q, sg_ref[...])
    o_ref[...] = (y * scale).astype(BF16)


def _fnet_call(proj, cl, sl, cg, sg, *, batch, seq):
    t = proj.shape[0]
    tr = min(512, seq)
    nr = seq // tr
    ub = P_FN // FNET_W
    return pl.pallas_call(
        functools.partial(_fnet_kernel, scale=1.0 / math.sqrt(seq * FNET_GW)),
        grid=(nr, batch),
        in_specs=[
            pl.BlockSpec((seq, FNET_W), lambda r, b: (b, ub)),
            pl.BlockSpec((tr, seq), lambda r, b: (r, 0)),
            pl.BlockSpec((tr, seq), lambda r, b: (r, 0)),
            pl.BlockSpec((FNET_W, FNET_W), lambda r, b: (0, 0)),
            pl.BlockSpec((FNET_W, FNET_W), lambda r, b: (0, 0)),
        ],
        out_specs=pl.BlockSpec((tr, FNET_W), lambda r, b: (b * nr + r, 0)),
        out_shape=jax.ShapeDtypeStruct((t, FNET_W), BF16),
        compiler_params=_params(("arbitrary", "arbitrary")),
        name="fnet",
    )(proj, cl, sl, cg, sg)


def _hy_filter_kernel(w1t_ref, w1c_ref, w1s_ref, b1_ref, w2_ref, b2_ref, w3_ref, fr_ref, o_ref, *, seq):
    ti = lax.broadcasted_iota(jnp.int32, (seq, LANES), 0).astype(F32)
    lane = lax.broadcasted_iota(jnp.int32, (seq, LANES), 1)
    band_step = (HY_BANDS - 1 - 1e-4) / (HY_BANDS - 1)
    bands = jnp.where(lane < HY_BANDS, 1e-4 + lane.astype(F32) * band_step, 0.0)
    ang = 2.0 * math.pi * ti * bands / seq
    t_col = ti[:, 0:1] / (seq - 1.0)
    fr = fr_ref[...]
    pre1 = t_col * w1t_ref[...] + _mm3(jnp.cos(ang), w1c_ref[...]) + _mm3(-jnp.sin(ang), w1s_ref[...]) + b1_ref[...]
    h1 = jnp.sin(fr * pre1)
    h2 = jnp.sin(fr * (_mm3(h1, w2_ref[...]) + b2_ref[...]))
    ch = lax.broadcasted_iota(jnp.int32, (1, HY_W), 1).astype(F32)
    d0 = math.log(HY_TARGET) / HY_FAST
    d1 = math.log(HY_TARGET) / HY_SLOW
    deltas = jnp.abs(d0 + ch * ((d1 - d0) / (HY_W - 1)))
    win = jnp.exp(-t_col * deltas)
    ss = jnp.zeros((1, HY_W), F32)
    for lo in (0, HY_W):
        hd = _mm3(h2, w3_ref[:, lo:lo + HY_W]) * win
        o_ref[:, lo:lo + HY_W] = hd
        ss = ss + jnp.sum(hd * hd, axis=0, keepdims=True)
    inv = lax.rsqrt(ss + EPS)
    for lo in (0, HY_W):
        o_ref[:, lo:lo + HY_W] = o_ref[:, lo:lo + HY_W] * inv


def _hy_filter_call(w1t, w1c, w1s, b1, w2, b2, w3, fr, *, seq):
    hid = HY_HID
    full = lambda shape: pl.BlockSpec(shape, lambda o: (0,) * len(shape))
    return pl.pallas_call(
        functools.partial(_hy_filter_kernel, seq=seq),
        grid=(2,),
        in_specs=[
            full((1, hid)), full((LANES, hid)), full((LANES, hid)), full((1, hid)),
            full((hid, hid)), full((1, hid)),
            pl.BlockSpec((hid, 2 * HY_W), lambda o: (0, o)),
            full((1, hid)),
        ],
        out_specs=pl.BlockSpec((seq, 2 * HY_W), lambda o: (0, o)),
        out_shape=jax.ShapeDtypeStruct((seq, 4 * HY_W), F32),
        compiler_params=_params(("arbitrary",)),
        name="hy_filter",
    )(w1t, w1c, w1s, b1, w2, b2, w3, fr)


def _hy_spec_kernel(h_ref, c_ref, s_ref, rc_ref, rs_ref, o_ref, hb_ref, *, seq, tf):
    f = pl.program_id(1)

    @pl.when(f == 0)
    def _():
        hb_ref[...] = h_ref[...].astype(BF16)

    cm = c_ref[...]
    sm = s_ref[...]
    hfw = hb_ref[:, :HY_W]
    hbw = hb_ref[:, HY_W:]
    pf = _mm(cm, hfw)
    pb = _mm(cm, hbw)
    qf = _mm(sm, hfw)
    qb = _mm(sm, hbw)
    reps = HY_W // LANES
    cf = jnp.concatenate([rc_ref[...]] * reps, axis=1)
    sf = jnp.concatenate([rs_ref[...]] * reps, axis=1)
    kre = pf + cf * pb - sf * qb
    kim = cf * qb + sf * pb - qf
    row0 = (lax.broadcasted_iota(jnp.int32, (tf, HY_W), 0) + f * tf) == 0
    wgt = jnp.where(row0, 0.5 / seq, 1.0 / seq)
    k1 = kre * wgt
    o_ref[0] = k1
    o_ref[1] = jnp.where(row0, 0.0, kim * wgt)
    o_ref[2] = jnp.where(row0, (qf - qb) * (0.5 / seq), k1)


def _hy_spec_call(h, cmat, smat, rot_c, rot_s, *, seq):
    tf = min(512, seq)
    nf = seq // tf
    return pl.pallas_call(
        functools.partial(_hy_spec_kernel, seq=seq, tf=tf),
        grid=(2, nf),
        in_specs=[
            pl.BlockSpec((seq, 2 * HY_W), lambda o, f: (0, o)),
            pl.BlockSpec((tf, seq), lambda o, f: (f, 0)),
            pl.BlockSpec((tf, seq), lambda o, f: (f, 0)),
            pl.BlockSpec((tf, LANES), lambda o, f: (f, 0)),
            pl.BlockSpec((tf, LANES), lambda o, f: (f, 0)),
        ],
        out_specs=pl.BlockSpec((None, 3, tf, HY_W), lambda o, f: (o, 0, f, 0)),
        out_shape=jax.ShapeDtypeStruct((2, 3, seq, HY_W), F32),
        scratch_shapes=[pltpu.VMEM((seq, 2 * HY_W), BF16)],
        compiler_params=_params(("arbitrary", "arbitrary")),
        name="hy_spectrum",
    )(h, cmat, smat, rot_c, rot_s)


def _short_conv(u, w, b, seg):
    n = u.shape[0]
    pos = lax.broadcasted_iota(jnp.int32, u.shape, 0) % seg
    prev = jnp.where(pos == 0, 0.0, pltpu.roll(u, 1, axis=0))
    nxt = jnp.where(pos == seg - 1, 0.0, pltpu.roll(u, n - 1, axis=0))
    return prev * w[0:1, :] + u * w[1:2, :] + nxt * w[2:3, :] + b


def _hy_conv_kernel(u_ref, m_ref, wu_ref, bu_ref, wm_ref, bm_ref, bias_ref, k_ref, cr_ref, sr_ref, cc_ref, sc_ref,
                    o_ref, ub_ref, acc_ref, *, conv_u, seg, nf, batch, seq):
    f = pl.program_id(1)
    tc = o_ref.shape[1]

    def load_u(b):
        u = u_ref[b * seq:(b + 1) * seq, :].astype(F32)
        if conv_u:
            u = _short_conv(u, wu_ref[...], bu_ref[...], seg)
        return u

    @pl.when(f == 0)
    def _():
        for b in range(batch):
            ub_ref[:, b * tc:(b + 1) * tc] = load_u(b).astype(BF16)
        acc_ref[...] = jnp.zeros(acc_ref.shape, F32)

    ub = ub_ref[...]
    p = _mm(cr_ref[...], ub)
    q = _mm(sr_ref[...], ub)
    k1, k2, k3 = (jnp.concatenate([k_ref[j]] * batch, axis=1) for j in range(3))
    av = (p * k1 + q * k2).astype(BF16)
    bv = (q * k3 - p * k2).astype(BF16)
    acc_ref[...] += _mm(cc_ref[...], av) + _mm(sc_ref[...], bv)

    @pl.when(f == nf - 1)
    def _():
        for b in range(batch):
            m = _short_conv(m_ref[b * seq:(b + 1) * seq, :].astype(F32), wm_ref[...], bm_ref[...], seg)
            y = acc_ref[:, b * tc:(b + 1) * tc] + bias_ref[...] * ub_ref[:, b * tc:(b + 1) * tc].astype(F32)
            o_ref[b * seq:(b + 1) * seq, :] = (m * y).astype(o_ref.dtype)


def _hy_conv_call(u_arr, u_col, m_arr, m_col, conv_w, conv_b, u_sect, m_sect, bias, kspec, cmat, smat, smat_inv,
                  *, batch, seq, seg, conv_u):
    t = u_arr.shape[0]
    tc = 256
    nch = HY_W // tc
    tf = min(256, seq)
    nf = seq // tf
    ucb, mcb = u_col // tc, m_col // tc
    usb, msb = u_sect * nch, m_sect * nch
    once = pl.Buffered(1)
    return pl.pallas_call(
        functools.partial(_hy_conv_kernel, conv_u=conv_u, seg=seg, nf=nf, batch=batch, seq=seq),
        grid=(nch, nf),
        in_specs=[
            pl.BlockSpec((t, tc), lambda ch, f: (0, ucb + ch), pipeline_mode=once),
            pl.BlockSpec((t, tc), lambda ch, f: (0, mcb + ch), pipeline_mode=once),
            pl.BlockSpec((3, tc), lambda ch, f: (0, usb + ch)),
            pl.BlockSpec((1, tc), lambda ch, f: (0, usb + ch)),
            pl.BlockSpec((3, tc), lambda ch, f: (0, msb + ch)),
            pl.BlockSpec((1, tc), lambda ch, f: (0, msb + ch)),
            pl.BlockSpec((1, tc), lambda ch, f: (0, ch)),
            pl.BlockSpec((3, tf, tc), lambda ch, f: (0, f, ch)),
            pl.BlockSpec((tf, seq), lambda ch, f: (f, 0)),
            pl.BlockSpec((tf, seq), lambda ch, f: (f, 0)),
            pl.BlockSpec((seq, tf), lambda ch, f: (0, f)),
            pl.BlockSpec((seq, tf), lambda ch, f: (0, f)),
        ],
        out_specs=pl.BlockSpec((t, tc), lambda ch, f: (0, ch)),
        out_shape=jax.ShapeDtypeStruct((t, HY_W), BF16),
        scratch_shapes=[pltpu.VMEM((seq, batch * tc), BF16), pltpu.VMEM((seq, batch * tc), F32)],
        compiler_params=_params(("arbitrary", "arbitrary")),
        name="hy_conv",
    )(u_arr, m_arr, conv_w, conv_b, conv_w, conv_b, bias, kspec, cmat, smat, cmat, smat_inv)


MERGE_CHUNK = 512


def _merge_kernel(x_ref, mod_ref, yg_ref, yf_ref, yh_ref, g0_ref, g1_ref, g2_ref, wg_ref, wf_ref, wh_ref, wo_ref,
                  o_ref, mg_ref):
    d = o_ref.shape[1]
    yg = yg_ref[...]
    yf = yf_ref[...]
    yh = yh_ref[...]
    for c0 in range(0, d, MERGE_CHUNK):
        cols = slice(c0, c0 + MERGE_CHUNK)
        m = g0_ref[:, cols].astype(F32) * _mm(yg, wg_ref[:, cols])
        m = m + g1_ref[:, cols].astype(F32) * _mm(yf, wf_ref[:, cols])
        m = m + g2_ref[:, cols].astype(F32) * _mm(yh, wh_ref[:, cols])
        mg_ref[:, cols] = m.astype(BF16)
    mg = mg_ref[...]
    for c0 in range(0, d, MERGE_CHUNK):
        cols = slice(c0, c0 + MERGE_CHUNK)
        o_ref[:, cols] = x_ref[:, cols] + mod_ref[5:6, cols] * _mm(mg, wo_ref[:, cols])


def _merge_call(xs, mod, proj, y_gla, y_fnet, y_hy, wbg, wbf, wbh, wo, *, layer, rows_per_mod, fixed_row):
    t, d = xs.shape
    tm = 512
    row = _mod_row_map(rows_per_mod, tm, fixed_row)
    gate = lambda j: pl.BlockSpec((pl.Element(tm), pl.Element(d)), lambda i: (i * tm, P_GATE + j * d))
    weight = lambda k: pl.BlockSpec((None, k, d), lambda i: (layer, 0, 0), pipeline_mode=pl.Buffered(1))
    return pl.pallas_call(
        _merge_kernel,
        grid=(t // tm,),
        in_specs=[
            pl.BlockSpec((tm, d), lambda i: (i, 0)),
            pl.BlockSpec((None, None, N_ADA, d), lambda i: (layer, row(i), 0, 0)),
            pl.BlockSpec((tm, GLA_VT), lambda i: (i, 0)),
            pl.BlockSpec((tm, FNET_W), lambda i: (i, 0)),
            pl.BlockSpec((tm, HY_W), lambda i: (i, 0)),
            gate(0),
            gate(1),
            gate(2),
            weight(GLA_VT),
            weight(FNET_W),
            weight(HY_W),
            weight(d),
        ],
        out_specs=pl.BlockSpec((tm, d), lambda i: (i, 0)),
        out_shape=jax.ShapeDtypeStruct((t, d), F32),
        scratch_shapes=[pltpu.VMEM((tm, d), BF16)],
        compiler_params=_params(("parallel",)),
        name="merge_out",
    )(xs, mod, y_gla, y_fnet, y_hy, proj, proj, proj, wbg, wbf, wbh, wo)


def _trig_tables(n, period):
    split = 32
    c = lax.broadcasted_iota(jnp.int32, (1, n), 1)

    def rows(r):
        ang = ((r * c) % period).astype(F32) * (2.0 * math.pi / period)
        return jnp.cos(ang), jnp.sin(ang)

    c_lo, s_lo = rows(lax.broadcasted_iota(jnp.int32, (split, 1), 0))
    c_hi, s_hi = rows(lax.broadcasted_iota(jnp.int32, (n // split, 1), 0) * split)
    cos = c_hi[:, None, :] * c_lo[None] - s_hi[:, None, :] * s_lo[None]
    sin = s_hi[:, None, :] * c_lo[None] + c_hi[:, None, :] * s_lo[None]
    return cos.reshape(n, n), sin.reshape(n, n)


def _fnet_tables(seq):
    cl, sl = _trig_tables(seq, seq)
    cg, sg = _trig_tables(FNET_GW, FNET_GW)
    eye = jnp.eye(FNET_GROUPS, dtype=F32)
    return cl.astype(BF16), sl.astype(BF16), jnp.kron(eye, cg).astype(BF16), jnp.kron(eye, sg).astype(BF16)


def _hyena_tables(seq):
    cm, sm = _trig_tables(seq, 2 * seq)
    r = lax.broadcasted_iota(jnp.int32, (seq, seq), 0)
    c = lax.broadcasted_iota(jnp.int32, (seq, seq), 1)
    sm_fwd = jnp.where(r == 0, (1 - 2 * (c % 2)).astype(F32), sm)
    sm_inv = jnp.where(c == 0, (1 - 2 * (r % 2)).astype(F32), sm)
    fr = lax.broadcasted_iota(jnp.int32, (seq, LANES), 0).astype(F32) * (math.pi / seq)
    return cm.astype(BF16), sm_fwd.astype(BF16), sm_inv.astype(BF16), jnp.cos(fr), jnp.sin(fr)


def _gla_gate_weights(gla_w2, gla_gb):
    depth = gla_w2.shape[0]
    w = gla_w2.reshape(depth, 2, GLA_RANK, GLA_HEADS, GLA_DK).transpose(0, 1, 3, 2, 4)
    zf = jnp.zeros((depth, GLA_HEADS, LANES, GLA_DK), F32)
    w2f = zf.at[:, :, :GLA_RANK].set(w[:, 0]).astype(BF16)
    w2b = zf.at[:, :, GLA_RANK:2 * GLA_RANK].set(w[:, 1]).astype(BF16)
    gb = gla_gb.reshape(depth, 2, GLA_HEADS, 1, GLA_DK)
    return w2f, w2b, gb[:, 0], gb[:, 1]


def kernel(x, c, ctx, c_ctx, ada_w, ada_b, norm_w, ffn1_wi, ffn1_wo, ffn2_wi, ffn2_wo, w_in, gla_w2, gla_gb,
           gla_norm_w, hy_conv_w, hy_conv_b, hy_f1_w, hy_f1_b, hy_f2_w, hy_f2_b, hy_f3_w, hy_freq, hy_bias,
           w_br_gla, w_br_fnet, w_br_hy, w_o, final_norm_w):
    batch, seq, d = x.shape
    ctx_len = ctx.shape[1]
    xs = x.reshape(batch * seq, d)
    cs = ctx.reshape(batch * ctx_len, d)

    cond = jnp.concatenate([c, c_ctx[None, :], jnp.zeros((8 - batch - 1, d), F32)], axis=0)
    mod = _ada_call(cond, ada_w, ada_b)
    ctx_row = batch

    w_t = jnp.swapaxes(w_in, 1, 2)
    wbg, wbf, wbh, wob = (w.astype(BF16) for w in (w_br_gla, w_br_fnet, w_br_hy, w_o))
    w2f, w2b, gbf, gbb = _gla_gate_weights(gla_w2, gla_gb)
    gla_nw = gla_norm_w.reshape(DEPTH, GLA_HEADS, 1, GLA_DV)

    fnet_tab = {n: _fnet_tables(n) for n in (seq, ctx_len)}
    hy_tab = {n: _hyena_tables(n) for n in (seq, ctx_len)}

    def mixers(proj, glow, layer, n, seg, s0f, s0b, need_y):
        y_gla, sf, sb = _gla_call(proj, glow, w2f[layer], w2b[layer], gbf[layer], gbb[layer], gla_nw[layer],
                                  s0f, s0b, batch=batch, seq=n)
        if not need_y:
            return None, sf, sb
        y_fnet = _fnet_call(proj, *fnet_tab[n], batch=batch, seq=n)
        cm, sm, sm_inv, rot_c, rot_s = hy_tab[n]
        w1 = hy_f1_w[layer]
        zpad = jnp.zeros((LANES - HY_BANDS, HY_HID), F32)
        h = _hy_filter_call(w1[0:1], jnp.concatenate([w1[1:1 + HY_BANDS], zpad], 0),
                            jnp.concatenate([w1[1 + HY_BANDS:], zpad], 0), hy_f1_b[layer][None, :],
                            hy_f2_w[layer], hy_f2_b[layer][None, :], hy_f3_w[layer], hy_freq[layer][None, :], seq=n)
        kspec = _hy_spec_call(h, cm, sm, rot_c, rot_s, seq=n)
        cw, cb = hy_conv_w[layer], hy_conv_b[layer][None, :]
        z = _hy_conv_call(proj, P_HY, proj, P_HY + HY_W, cw, cb, 0, 1, hy_bias[layer, 0][None, :], kspec[0], cm, sm,
                          sm_inv, batch=batch, seq=n, seg=seg, conv_u=True)
        y_hy = _hy_conv_call(z, 0, proj, P_HY + 2 * HY_W, cw, cb, 0, 2, hy_bias[layer, 1][None, :], kspec[1], cm, sm,
                             sm_inv, batch=batch, seq=n, seg=seg, conv_u=False)
        return (y_gla, y_fnet, y_hy), sf, sb

    s_zero = jnp.zeros((batch, GLA_HEADS, GLA_DV, GLA_DK), F32)
    for layer in range(DEPTH):
        last = layer == DEPTH - 1
        x_kw = dict(layer=layer, rows_per_mod=seq, fixed_row=None)
        c_kw = dict(layer=layer, rows_per_mod=ctx_len, fixed_row=ctx_row)
        xs = _ffn(xs, _norm_call(xs, mod, norm_w, sub=0, **x_kw), mod, ffn1_wi, ffn1_wo, sub=0, **x_kw)
        cs = _ffn(cs, _norm_call(cs, mod, norm_w, sub=0, **c_kw), mod, ffn1_wi, ffn1_wo, sub=0, **c_kw)
        cn, glow_c = _mixnorm_call(cs, mod, norm_w, w_t, **c_kw)
        proj_c = _proj_call(cn, w_t, layer=layer)
        ys_c, sf, sb = mixers(proj_c, glow_c, layer, ctx_len, ctx_len, s_zero, s_zero, not last)
        if not last:
            cs = _merge_call(cs, mod, proj_c, *ys_c, wbg, wbf, wbh, wob, **c_kw)
            cs = _ffn(cs, _norm_call(cs, mod, norm_w, sub=2, **c_kw), mod, ffn2_wi, ffn2_wo, sub=2, **c_kw)
        xn, glow_x = _mixnorm_call(xs, mod, norm_w, w_t, **x_kw)
        proj_x = _proj_call(xn, w_t, layer=layer)
        ys_x, _, _ = mixers(proj_x, glow_x, layer, seq, GRID_W, sf, sb, True)
        xs = _merge_call(xs, mod, proj_x, *ys_x, wbg, wbf, wbh, wob, **x_kw)
        xs = _ffn(xs, _norm_call(xs, mod, norm_w, sub=2, **x_kw), mod, ffn2_wi, ffn2_wo, sub=2, **x_kw)
    return _final_norm_call(xs, final_norm_w.reshape(1, d)).reshape(batch, seq, d)
```

```python
import functools
import math

import jax
import jax.numpy as jnp
from jax import lax
from jax.experimental import pallas as pl
from jax.experimental.pallas import tpu as pltpu

F32 = jnp.float32
BF16 = jnp.bfloat16

D_MODEL = 2048
DEPTH = 4
GRID_W = 64
N_ADA = 9
D_FF = 5504
GLA_HEADS = 4
GLA_DK = 128
GLA_DV = 256
GLA_KT = GLA_HEADS * GLA_DK
GLA_VT = GLA_HEADS * GLA_DV
GLA_RANK = 16
GLA_TAU = 16.0
GLA_CHUNK = 64
FNET_GROUPS = 4
FNET_GW = 128
FNET_W = FNET_GROUPS * FNET_GW
HY_W = 512
HY_BANDS = 16
HY_HID = 64
HY_FAST = 0.3
HY_SLOW = 1.5
HY_TARGET = 1e-2
EPS = 1e-6

LANES = 128
VMEM_LIMIT_BYTES = 56 * 1024 * 1024

FF_TILE = 512
FF_STEPS = -(-D_FF // FF_TILE)

W_GLOW = 2 * GLA_KT + GLA_VT
W_IN_COLS = W_GLOW + 2 * GLA_RANK + GLA_VT + FNET_W + 3 * HY_W + 3 * D_MODEL
P_TILE = 1024
P_Q = 0
P_K = GLA_KT
P_V = 2 * GLA_KT
P_R = W_GLOW
P_FN = P_R + GLA_VT
P_HY = P_FN + FNET_W
P_GATE = P_HY + 3 * HY_W
P_TOTAL = P_GATE + 3 * D_MODEL


def _mm(a, b):
    return jnp.dot(a, b, preferred_element_type=F32)


def _mm_nt(a, b):
    return lax.dot_general(a, b, (((1,), (1,)), ((), ())), preferred_element_type=F32)


def _mm_tn(a, b):
    return lax.dot_general(a, b, (((0,), (0,)), ((), ())), preferred_element_type=F32)


def _split2(a):
    hi = a.astype(BF16)
    lo = (a - hi.astype(F32)).astype(BF16)
    return hi, lo


def _mm3(a, b):
    ah, al = _split2(a)
    bh, bl = _split2(b)
    return _mm(ah, bh) + (_mm(ah, bl) + _mm(al, bh))


def _sigmoid(x):
    return 0.5 * jnp.tanh(0.5 * x) + 0.5


def _silu(x):
    return x * _sigmoid(x)


def _params(sem):
    return pltpu.CompilerParams(dimension_semantics=sem, vmem_limit_bytes=VMEM_LIMIT_BYTES)


def _modnorm(x, nw, shift, scale):
    ms = jnp.mean(x * x, axis=-1, keepdims=True)
    y = x * lax.rsqrt(ms + EPS) * nw
    return y * (1.0 + scale) + shift


def _ada_kernel(c_ref, w_ref, b_ref, o_ref):
    a = _silu(c_ref[...]).astype(BF16)
    o_ref[...] = _mm(a, w_ref[...].astype(BF16)) + b_ref[...]


def _ada_call(cond, ada_w, ada_b):
    depth, d, n = ada_w.shape
    rows = cond.shape[0]
    tn = 1024
    out = pl.pallas_call(
        _ada_kernel,
        grid=(depth, n // tn),
        in_specs=[
            pl.BlockSpec((rows, d), lambda l, j: (0, 0)),
            pl.BlockSpec((None, d, tn), lambda l, j: (l, 0, j)),
            pl.BlockSpec((None, 1, tn), lambda l, j: (l, 0, j)),
        ],
        out_specs=pl.BlockSpec((None, rows, tn), lambda l, j: (l, 0, j)),
        out_shape=jax.ShapeDtypeStruct((depth, rows, n), F32),
        compiler_params=_params(("arbitrary", "arbitrary")),
        name="ada_mod",
    )(cond, ada_w, ada_b.reshape(depth, 1, n))
    return out.reshape(depth, rows, N_ADA, d)


def _mod_row_map(rows_per_mod, tm, fixed_row):
    if fixed_row is not None:
        return lambda i: fixed_row
    per = rows_per_mod // tm
    return lambda i: i // per


def _norm_kernel(x_ref, mod_ref, nw_ref, o_ref, *, sub):
    xn = _modnorm(x_ref[...], nw_ref[sub:sub + 1, :], mod_ref[3 * sub:3 * sub + 1, :],
                  mod_ref[3 * sub + 1:3 * sub + 2, :])
    o_ref[...] = xn.astype(BF16)


def _norm_call(xs, mod, norm_w, *, layer, sub, rows_per_mod, fixed_row):
    t, d = xs.shape
    tm = 512
    row = _mod_row_map(rows_per_mod, tm, fixed_row)
    return pl.pallas_call(
        functools.partial(_norm_kernel, sub=sub),
        grid=(t // tm,),
        in_specs=[
            pl.BlockSpec((tm, d), lambda i: (i, 0)),
            pl.BlockSpec((None, None, N_ADA, d), lambda i: (layer, row(i), 0, 0)),
            pl.BlockSpec((None, 3, d), lambda i: (layer, 0, 0)),
        ],
        out_specs=pl.BlockSpec((tm, d), lambda i: (i, 0)),
        out_shape=jax.ShapeDtypeStruct((t, d), BF16),
        compiler_params=_params(("parallel",)),
        name="ffn_norm",
    )(xs, mod, norm_w)


def _ffn_up_kernel(xn_ref, wa_ref, wg_ref, h_ref, wab_ref, wgb_ref):
    @pl.when(pl.program_id(1) == 0)
    def _():
        wab_ref[...] = wa_ref[...].astype(BF16)
        wgb_ref[...] = wg_ref[...].astype(BF16)

    xn = xn_ref[...]
    a = _mm(xn, wab_ref[...])
    g = _mm(xn, wgb_ref[...])
    h_ref[...] = (_silu(g) * a).astype(BF16)


def _ffn_up_call(xn, wi, *, layer):
    t, d = xn.shape
    tm = min(2048, t)
    back = (FF_STEPS * FF_TILE - D_FF) // LANES
    hid_blk = lambda f: f * (FF_TILE // LANES) - (f // (FF_STEPS - 1)) * back
    hid0 = lambda f: hid_blk(f) * LANES
    gate0 = lambda f: (D_FF // LANES + hid_blk(f)) * LANES
    return pl.pallas_call(
        _ffn_up_kernel,
        grid=(FF_STEPS, t // tm),
        in_specs=[
            pl.BlockSpec((tm, d), lambda f, i: (i, 0)),
            pl.BlockSpec((None, pl.Element(d), pl.Element(FF_TILE)), lambda f, i: (layer, 0, hid0(f))),
            pl.BlockSpec((None, pl.Element(d), pl.Element(FF_TILE)), lambda f, i: (layer, 0, gate0(f))),
        ],
        out_specs=pl.BlockSpec((pl.Element(tm), pl.Element(FF_TILE)), lambda f, i: (i * tm, hid0(f))),
        out_shape=jax.ShapeDtypeStruct((t, D_FF), BF16),
        scratch_shapes=[pltpu.VMEM((d, FF_TILE), BF16), pltpu.VMEM((d, FF_TILE), BF16)],
        compiler_params=_params(("arbitrary", "arbitrary")),
        name="ffn_up",
    )(xn, wi, wi)


def _ffn_down_kernel(h_ref, w_ref, x_ref, mod_ref, o_ref, wb_ref, *, sub):
    @pl.when(pl.program_id(1) == 0)
    def _():
        wb_ref[...] = w_ref[...].astype(BF16)

    y = _mm(h_ref[...], wb_ref[...])
    o_ref[...] = x_ref[...] + 0.5 * mod_ref[3 * sub + 2:3 * sub + 3, :] * y


def _ffn_down_call(h, wo, xs, mod, *, layer, sub, rows_per_mod, fixed_row):
    t, d = xs.shape
    tm = 512
    tn = 512
    row = _mod_row_map(rows_per_mod, tm, fixed_row)
    return pl.pallas_call(
        functools.partial(_ffn_down_kernel, sub=sub),
        grid=(d // tn, t // tm),
        in_specs=[
            pl.BlockSpec((tm, D_FF), lambda n, i: (i, 0)),
            pl.BlockSpec((None, D_FF, tn), lambda n, i: (layer, 0, n)),
            pl.BlockSpec((tm, tn), lambda n, i: (i, n)),
            pl.BlockSpec((None, None, N_ADA, tn), lambda n, i: (layer, row(i), 0, n)),
        ],
        out_specs=pl.BlockSpec((tm, tn), lambda n, i: (i, n)),
        out_shape=jax.ShapeDtypeStruct((t, d), F32),
        scratch_shapes=[pltpu.VMEM((D_FF, tn), BF16)],
        compiler_params=_params(("arbitrary", "arbitrary")),
        name="ffn_down",
    )(h, wo, xs, mod)


def _ffn(xs, xn, mod, wi, wo, *, layer, sub, rows_per_mod, fixed_row):
    h = _ffn_up_call(xn, wi, layer=layer)
    return _ffn_down_call(h, wo, xs, mod, layer=layer, sub=sub, rows_per_mod=rows_per_mod, fixed_row=fixed_row)


def _final_norm_kernel(x_ref, w_ref, o_ref):
    x = x_ref[...]
    ms = jnp.mean(x * x, axis=-1, keepdims=True)
    o_ref[...] = x * lax.rsqrt(ms + EPS) * w_ref[...]


def _final_norm_call(xs, w):
    t, d = xs.shape
    tm = 512
    return pl.pallas_call(
        _final_norm_kernel,
        grid=(t // tm,),
        in_specs=[pl.BlockSpec((tm, d), lambda i: (i, 0)), pl.BlockSpec((1, d), lambda i: (0, 0))],
        out_specs=pl.BlockSpec((tm, d), lambda i: (i, 0)),
        out_shape=jax.ShapeDtypeStruct((t, d), F32),
        compiler_params=_params(("parallel",)),
        name="final_norm",
    )(xs, w)


def _mixnorm_kernel(x_ref, mod_ref, nw_ref, wl_ref, xn_ref, gl_ref):
    xn = _modnorm(x_ref[...], nw_ref[1:2, :], mod_ref[3:4, :], mod_ref[4:5, :]).astype(BF16)
    xn_ref[...] = xn
    gl_ref[...] = _mm_nt(xn, wl_ref[...].astype(BF16)).astype(BF16)


def _mixnorm_call(xs, mod, norm_w, w_t, *, layer, rows_per_mod, fixed_row):
    t, d = xs.shape
    tm = 512
    row = _mod_row_map(rows_per_mod, tm, fixed_row)
    return pl.pallas_call(
        _mixnorm_kernel,
        grid=(t // tm,),
        in_specs=[
            pl.BlockSpec((tm, d), lambda i: (i, 0)),
            pl.BlockSpec((None, None, N_ADA, d), lambda i: (layer, row(i), 0, 0)),
            pl.BlockSpec((None, 3, d), lambda i: (layer, 0, 0)),
            pl.BlockSpec((None, LANES, d), lambda i: (layer, W_GLOW // LANES, 0)),
        ],
        out_specs=[pl.BlockSpec((tm, d), lambda i: (i, 0)), pl.BlockSpec((tm, LANES), lambda i: (i, 0))],
        out_shape=[jax.ShapeDtypeStruct((t, d), BF16), jax.ShapeDtypeStruct((t, LANES), BF16)],
        compiler_params=_params(("parallel",)),
        name="mix_norm",
    )(xs, mod, norm_w, w_t)


P_TILES = P_TOTAL // P_TILE
P_SKIP = 2 * GLA_RANK
P_GATE_TILE = P_GATE // P_TILE


def _proj_row0(n):
    sublanes = 8
    first = W_GLOW // P_TILE
    after = (n + (P_TILES - first)) // P_TILES
    return (n * (P_TILE // sublanes) + after * (P_SKIP // sublanes)) * sublanes


def _proj_kernel(xn_ref, w_ref, o_ref, wb_ref):
    @pl.when(pl.program_id(1) == 0)
    def _():
        wb_ref[...] = w_ref[...].astype(BF16)

    @pl.when(pl.program_id(0) < P_GATE_TILE)
    def _():
        o_ref[...] = _mm_nt(xn_ref[...], wb_ref[...]).astype(BF16)

    @pl.when(pl.program_id(0) >= P_GATE_TILE)
    def _():
        o_ref[...] = _sigmoid(_mm_nt(xn_ref[...], wb_ref[...])).astype(BF16)


def _proj_call(xn, w_t, *, layer):
    t, d = xn.shape
    tm = min(2048, t)
    return pl.pallas_call(
        _proj_kernel,
        grid=(P_TILES, t // tm),
        in_specs=[
            pl.BlockSpec((tm, d), lambda n, i: (i, 0)),
            pl.BlockSpec((None, pl.Element(P_TILE), pl.Element(d)), lambda n, i: (layer, _proj_row0(n), 0)),
        ],
        out_specs=pl.BlockSpec((tm, P_TILE), lambda n, i: (i, n)),
        out_shape=jax.ShapeDtypeStruct((t, P_TOTAL), BF16),
        scratch_shapes=[pltpu.VMEM((P_TILE, d), BF16)],
        compiler_params=_params(("arbitrary", "arbitrary")),
        name="mix_proj",
    )(xn, w_t)


def _log_sigmoid(x):
    return jnp.minimum(x, 0.0) - jnp.log1p(jnp.exp(-jnp.abs(x)))


def _chunk_scan(x, reverse):
    n = x.shape[0]
    pos = lax.broadcasted_iota(jnp.int32, x.shape, 0) % GLA_CHUNK
    s = 1
    while s < GLA_CHUNK:
        if reverse:
            x = x + jnp.where(pos < GLA_CHUNK - s, pltpu.roll(x, n - s, axis=0), 0.0)
        else:
            x = x + jnp.where(pos >= s, pltpu.roll(x, s, axis=0), 0.0)
        s *= 2
    return x


def _gla_kernel(q_ref, k_ref, v_ref, r_ref, gl_ref, w2f_ref, w2b_ref, gbf_ref, gbb_ref, nw_ref,
                s0f_ref, s0b_ref, y_ref, sf_ref, sb_ref, qd_ref, kd_ref, ke_ref, dec_ref, of_ref, ob_ref,
                att_ref, kv_ref, sp_ref, *, seq):
    c = GLA_CHUNK
    n_chunks = seq // c
    glow = gl_ref[...]
    q = q_ref[...].astype(F32) * GLA_DK ** -0.5
    k = k_ref[...].astype(F32)
    for d, (w2_ref, gb_ref) in enumerate(((w2f_ref, gbf_ref), (w2b_ref, gbb_ref))):
        lg = _log_sigmoid(_mm(glow, w2_ref[...]) + gb_ref[...]) * (1.0 / GLA_TAU)
        gc = _chunk_scan(lg, reverse=d == 1)
        gc3 = gc.reshape(n_chunks, c, GLA_DK)
        gt = gc3[:, 0:1, :] if d == 1 else gc3[:, c - 1:c, :]
        qd_ref[d] = (q * jnp.exp(gc)).astype(BF16)
        kd_ref[d] = (k * jnp.exp(-gc)).astype(BF16)
        ke_ref[d] = (k.reshape(n_chunks, c, GLA_DK) * jnp.exp(gt - gc3)).reshape(seq, GLA_DK).astype(BF16)
        dec_ref[d] = jnp.exp(gt)
    sf_ref[...] = s0f_ref[...]
    sb_ref[...] = s0b_ref[...]

    ri = lax.broadcasted_iota(jnp.int32, (c, c), 0)
    ci = lax.broadcasted_iota(jnp.int32, (c, c), 1)

    unroll = min(8, n_chunks)
    dirs = ((0, ri >= ci, sf_ref, of_ref), (1, ri <= ci, sb_ref, ob_ref))

    def local_body(n, carry):
        r0 = pl.multiple_of(n * c, c)
        v = v_ref[pl.ds(r0, c), :]
        for d, mask, _, _ in dirs:
            scores = _mm_nt(qd_ref[d, pl.ds(r0, c), :], kd_ref[d, pl.ds(r0, c), :])
            att_ref[d, n] = jnp.where(mask, scores, 0.0).astype(BF16)
            kv_ref[d, n] = _mm_tn(v, ke_ref[d, pl.ds(r0, c), :])
        return carry

    lax.fori_loop(0, n_chunks, local_body, 0, unroll=unroll)

    def scan_body(i, carry):
        for d, _, st_ref, _ in dirs:
            n = i if d == 0 else n_chunks - 1 - i
            st = st_ref[...]
            sp_ref[d, n] = st.astype(BF16)
            st_ref[...] = st * dec_ref[d, n] + kv_ref[d, n]
        return carry

    lax.fori_loop(0, n_chunks, scan_body, 0, unroll=min(4, n_chunks))

    def out_body(n, carry):
        r0 = pl.multiple_of(n * c, c)
        v = v_ref[pl.ds(r0, c), :]
        for d, _, _, o_ref in dirs:
            o_ref[pl.ds(r0, c), :] = _mm(att_ref[d, n], v) + _mm_nt(qd_ref[d, pl.ds(r0, c), :], sp_ref[d, n])
        return carry

    lax.fori_loop(0, n_chunks, out_body, 0, unroll=unroll)

    o = of_ref[...] + ob_ref[...]
    ms = jnp.mean(o * o, axis=-1, keepdims=True)
    y = o * lax.rsqrt(ms + EPS) * nw_ref[...] * _silu(r_ref[...].astype(F32))
    y_ref[...] = y.astype(BF16)


def _gla_call(proj, glow, w2f, w2b, gbf, gbb, gla_nw, s0f, s0b, *, batch, seq):
    t = proj.shape[0]
    h = GLA_HEADS
    qb, kb = P_Q // GLA_DK, P_K // GLA_DK
    vb, rb = P_V // GLA_DV, P_R // GLA_DV
    st_spec = pl.BlockSpec((None, None, GLA_DV, GLA_DK), lambda b, hh: (b, hh, 0, 0))
    st_shape = jax.ShapeDtypeStruct((batch, h, GLA_DV, GLA_DK), F32)
    return pl.pallas_call(
        functools.partial(_gla_kernel, seq=seq),
        grid=(batch, h),
        in_specs=[
            pl.BlockSpec((seq, GLA_DK), lambda b, hh: (b, qb + hh)),
            pl.BlockSpec((seq, GLA_DK), lambda b, hh: (b, kb + hh)),
            pl.BlockSpec((seq, GLA_DV), lambda b, hh: (b, vb + hh)),
            pl.BlockSpec((seq, GLA_DV), lambda b, hh: (b, rb + hh)),
            pl.BlockSpec((seq, LANES), lambda b, hh: (b, 0)),
            pl.BlockSpec((None, LANES, GLA_DK), lambda b, hh: (hh, 0, 0)),
            pl.BlockSpec((None, LANES, GLA_DK), lambda b, hh: (hh, 0, 0)),
            pl.BlockSpec((None, 1, GLA_DK), lambda b, hh: (hh, 0, 0)),
            pl.BlockSpec((None, 1, GLA_DK), lambda b, hh: (hh, 0, 0)),
            pl.BlockSpec((None, 1, GLA_DV), lambda b, hh: (hh, 0, 0)),
            st_spec,
            st_spec,
        ],
        out_specs=[pl.BlockSpec((seq, GLA_DV), lambda b, hh: (b, hh)), st_spec, st_spec],
        out_shape=[jax.ShapeDtypeStruct((t, GLA_VT), BF16), st_shape, st_shape],
        scratch_shapes=[
            pltpu.VMEM((2, seq, GLA_DK), BF16),
            pltpu.VMEM((2, seq, GLA_DK), BF16),
            pltpu.VMEM((2, seq, GLA_DK), BF16),
            pltpu.VMEM((2, seq // GLA_CHUNK, 1, GLA_DK), F32),
            pltpu.VMEM((seq, GLA_DV), F32),
            pltpu.VMEM((seq, GLA_DV), F32),
            pltpu.VMEM((2, seq // GLA_CHUNK, GLA_CHUNK, GLA_CHUNK), BF16),
            pltpu.VMEM((2, seq // GLA_CHUNK, GLA_DV, GLA_DK), F32),
            pltpu.VMEM((2, seq // GLA_CHUNK, GLA_DV, GLA_DK), BF16),
        ],
        compiler_params=_params(("parallel", "parallel")),
        name="gla",
    )(proj, proj, proj, proj, glow, w2f, w2b, gbf, gbb, gla_nw, s0f, s0b)


def _fnet_kernel(u_ref, cl_ref, sl_ref, cg_ref, sg_ref, o_ref, *, scale):
    u = u_ref[...]
    p = _mm(cl_ref[...], u).astype(BF16)
    q = _mm(sl_ref[...], u).astype(BF16)
    y = _mm(p, cg_ref[...]) - _mm(q, sg_ref[...])
    o_ref[...] = (y * scale).astype(BF16)


def _fnet_call(proj, cl, sl, cg, sg, *, batch, seq):
    t = proj.shape[0]
    tr = min(512, seq)
    nr = seq // tr
    ub = P_FN // FNET_W
    return pl.pallas_call(
        functools.partial(_fnet_kernel, scale=1.0 / math.sqrt(seq * FNET_GW)),
        grid=(nr, batch),
        in_specs=[
            pl.BlockSpec((seq, FNET_W), lambda r, b: (b, ub)),
            pl.BlockSpec((tr, seq), lambda r, b: (r, 0)),
            pl.BlockSpec((tr, seq), lambda r, b: (r, 0)),
            pl.BlockSpec((FNET_W, FNET_W), lambda r, b: (0, 0)),
            pl.BlockSpec((FNET_W, FNET_W), lambda r, b: (0, 0)),
        ],
        out_specs=pl.BlockSpec((tr, FNET_W), lambda r, b: (b * nr + r, 0)),
        out_shape=jax.ShapeDtypeStruct((t, FNET_W), BF16),
        compiler_params=_params(("arbitrary", "arbitrary")),
        name="fnet",
    )(proj, cl, sl, cg, sg)


def _hy_filter_kernel(w1t_ref, w1c_ref, w1s_ref, b1_ref, w2_ref, b2_ref, w3_ref, fr_ref, o_ref, *, seq):
    ti = lax.broadcasted_iota(jnp.int32, (seq, LANES), 0).astype(F32)
    lane = lax.broadcasted_iota(jnp.int32, (seq, LANES), 1)
    band_step = (HY_BANDS - 1 - 1e-4) / (HY_BANDS - 1)
    bands = jnp.where(lane < HY_BANDS, 1e-4 + lane.astype(F32) * band_step, 0.0)
    ang = 2.0 * math.pi * ti * bands / seq
    t_col = ti[:, 0:1] / (seq - 1.0)
    fr = fr_ref[...]
    pre1 = t_col * w1t_ref[...] + _mm3(jnp.cos(ang), w1c_ref[...]) + _mm3(-jnp.sin(ang), w1s_ref[...]) + b1_ref[...]
    h1 = jnp.sin(fr * pre1)
    h2 = jnp.sin(fr * (_mm3(h1, w2_ref[...]) + b2_ref[...]))
    ch = lax.broadcasted_iota(jnp.int32, (1, HY_W), 1).astype(F32)
    d0 = math.log(HY_TARGET) / HY_FAST
    d1 = math.log(HY_TARGET) / HY_SLOW
    deltas = jnp.abs(d0 + ch * ((d1 - d0) / (HY_W - 1)))
    win = jnp.exp(-t_col * deltas)
    ss = jnp.zeros((1, HY_W), F32)
    for lo in (0, HY_W):
        hd = _mm3(h2, w3_ref[:, lo:lo + HY_W]) * win
        o_ref[:, lo:lo + HY_W] = hd
        ss = ss + jnp.sum(hd * hd, axis=0, keepdims=True)
    inv = lax.rsqrt(ss + EPS)
    for lo in (0, HY_W):
        o_ref[:, lo:lo + HY_W] = o_ref[:, lo:lo + HY_W] * inv


def _hy_filter_call(w1t, w1c, w1s, b1, w2, b2, w3, fr, *, seq):
    hid = HY_HID
    full = lambda shape: pl.BlockSpec(shape, lambda o: (0,) * len(shape))
    return pl.pallas_call(
        functools.partial(_hy_filter_kernel, seq=seq),
        grid=(2,),
        in_specs=[
            full((1, hid)), full((LANES, hid)), full((LANES, hid)), full((1, hid)),
            full((hid, hid)), full((1, hid)),
            pl.BlockSpec((hid, 2 * HY_W), lambda o: (0, o)),
            full((1, hid)),
        ],
        out_specs=pl.BlockSpec((seq, 2 * HY_W), lambda o: (0, o)),
        out_shape=jax.ShapeDtypeStruct((seq, 4 * HY_W), F32),
        compiler_params=_params(("arbitrary",)),
        name="hy_filter",
    )(w1t, w1c, w1s, b1, w2, b2, w3, fr)


def _hy_spec_kernel(h_ref, c_ref, s_ref, rc_ref, rs_ref, o_ref, hb_ref, *, seq, tf):
    f = pl.program_id(1)

    @pl.when(f == 0)
    def _():
        hb_ref[...] = h_ref[...].astype(BF16)

    cm = c_ref[...]
    sm = s_ref[...]
    hfw = hb_ref[:, :HY_W]
    hbw = hb_ref[:, HY_W:]
    pf = _mm(cm, hfw)
    pb = _mm(cm, hbw)
    qf = _mm(sm, hfw)
    qb = _mm(sm, hbw)
    reps = HY_W // LANES
    cf = jnp.concatenate([rc_ref[...]] * reps, axis=1)
    sf = jnp.concatenate([rs_ref[...]] * reps, axis=1)
    kre = pf + cf * pb - sf * qb
    kim = cf * qb + sf * pb - qf
    row0 = (lax.broadcasted_iota(jnp.int32, (tf, HY_W), 0) + f * tf) == 0
    wgt = jnp.where(row0, 0.5 / seq, 1.0 / seq)
    k1 = kre * wgt
    o_ref[0] = k1
    o_ref[1] = jnp.where(row0, 0.0, kim * wgt)
    o_ref[2] = jnp.where(row0, (qf - qb) * (0.5 / seq), k1)


def _hy_spec_call(h, cmat, smat, rot_c, rot_s, *, seq):
    tf = min(512, seq)
    nf = seq // tf
    return pl.pallas_call(
        functools.partial(_hy_spec_kernel, seq=seq, tf=tf),
        grid=(2, nf),
        in_specs=[
            pl.BlockSpec((seq, 2 * HY_W), lambda o, f: (0, o)),
            pl.BlockSpec((tf, seq), lambda o, f: (f, 0)),
            pl.BlockSpec((tf, seq), lambda o, f: (f, 0)),
            pl.BlockSpec((tf, LANES), lambda o, f: (f, 0)),
            pl.BlockSpec((tf, LANES), lambda o, f: (f, 0)),
        ],
        out_specs=pl.BlockSpec((None, 3, tf, HY_W), lambda o, f: (o, 0, f, 0)),
        out_shape=jax.ShapeDtypeStruct((2, 3, seq, HY_W), F32),
        scratch_shapes=[pltpu.VMEM((seq, 2 * HY_W), BF16)],
        compiler_params=_params(("arbitrary", "arbitrary")),
        name="hy_spectrum",
    )(h, cmat, smat, rot_c, rot_s)


def _short_conv(u, w, b, seg):
    n = u.shape[0]
    pos = lax.broadcasted_iota(jnp.int32, u.shape, 0) % seg
    prev = jnp.where(pos == 0, 0.0, pltpu.roll(u, 1, axis=0))
    nxt = jnp.where(pos == seg - 1, 0.0, pltpu.roll(u, n - 1, axis=0))
    return prev * w[0:1, :] + u * w[1:2, :] + nxt * w[2:3, :] + b


def _hy_conv_kernel(u_ref, m_ref, wu_ref, bu_ref, wm_ref, bm_ref, bias_ref, k_ref, cr_ref, sr_ref, cc_ref, sc_ref,
                    o_ref, ub_ref, acc_ref, *, conv_u, seg, nf, batch, seq):
    f = pl.program_id(1)
    tc = o_ref.shape[1]

    def load_u(b):
        u = u_ref[b * seq:(b + 1) * seq, :].astype(F32)
        if conv_u:
            u = _short_conv(u, wu_ref[...], bu_ref[...], seg)
        return u

    @pl.when(f == 0)
    def _():
        for b in range(batch):
            ub_ref[:, b * tc:(b + 1) * tc] = load_u(b).astype(BF16)
        acc_ref[...] = jnp.zeros(acc_ref.shape, F32)

    ub = ub_ref[...]
    p = _mm(cr_ref[...], ub)
    q = _mm(sr_ref[...], ub)
    k1, k2, k3 = (jnp.concatenate([k_ref[j]] * batch, axis=1) for j in range(3))
    av = (p * k1 + q * k2).astype(BF16)
    bv = (q * k3 - p * k2).astype(BF16)
    acc_ref[...] += _mm(cc_ref[...], av) + _mm(sc_ref[...], bv)

    @pl.when(f == nf - 1)
    def _():
        for b in range(batch):
            m = _short_conv(m_ref[b * seq:(b + 1) * seq, :].astype(F32), wm_ref[...], bm_ref[...], seg)
            y = acc_ref[:, b * tc:(b + 1) * tc] + bias_ref[...] * ub_ref[:, b * tc:(b + 1) * tc].astype(F32)
            o_ref[b * seq:(b + 1) * seq, :] = (m * y).astype(o_ref.dtype)


def _hy_conv_call(u_arr, u_col, m_arr, m_col, conv_w, conv_b, u_sect, m_sect, bias, kspec, cmat, smat, smat_inv,
                  *, batch, seq, seg, conv_u):
    t = u_arr.shape[0]
    tc = 256
    nch = HY_W // tc
    tf = min(256, seq)
    nf = seq // tf
    ucb, mcb = u_col // tc, m_col // tc
    usb, msb = u_sect * nch, m_sect * nch
    once = pl.Buffered(1)
    return pl.pallas_call(
        functools.partial(_hy_conv_kernel, conv_u=conv_u, seg=seg, nf=nf, batch=batch, seq=seq),
        grid=(nch, nf),
        in_specs=[
            pl.BlockSpec((t, tc), lambda ch, f: (0, ucb + ch), pipeline_mode=once),
            pl.BlockSpec((t, tc), lambda ch, f: (0, mcb + ch), pipeline_mode=once),
            pl.BlockSpec((3, tc), lambda ch, f: (0, usb + ch)),
            pl.BlockSpec((1, tc), lambda ch, f: (0, usb + ch)),
            pl.BlockSpec((3, tc), lambda ch, f: (0, msb + ch)),
            pl.BlockSpec((1, tc), lambda ch, f: (0, msb + ch)),
            pl.BlockSpec((1, tc), lambda ch, f: (0, ch)),
            pl.BlockSpec((3, tf, tc), lambda ch, f: (0, f, ch)),
            pl.BlockSpec((tf, seq), lambda ch, f: (f, 0)),
            pl.BlockSpec((tf, seq), lambda ch, f: (f, 0)),
            pl.BlockSpec((seq, tf), lambda ch, f: (0, f)),
            pl.BlockSpec((seq, tf), lambda ch, f: (0, f)),
        ],
        out_specs=pl.BlockSpec((t, tc), lambda ch, f: (0, ch)),
        out_shape=jax.ShapeDtypeStruct((t, HY_W), BF16),
        scratch_shapes=[pltpu.VMEM((seq, batch * tc), BF16), pltpu.VMEM((seq, batch * tc), F32)],
        compiler_params=_params(("arbitrary", "arbitrary")),
        name="hy_conv",
    )(u_arr, m_arr, conv_w, conv_b, conv_w, conv_b, bias, kspec, cmat, smat, cmat, smat_inv)


MERGE_CHUNK = 512


def _merge_kernel(x_ref, mod_ref, nw_ref, yg_ref, yf_ref, yh_ref, g0_ref, g1_ref, g2_ref, wg_ref, wf_ref, wh_ref,
                  wo_ref, o_ref, xn_ref, mg_ref):
    d = o_ref.shape[1]
    yg = yg_ref[...]
    yf = yf_ref[...]
    yh = yh_ref[...]
    for c0 in range(0, d, MERGE_CHUNK):
        cols = slice(c0, c0 + MERGE_CHUNK)
        m = g0_ref[:, cols].astype(F32) * _mm(yg, wg_ref[:, cols])
        m = m + g1_ref[:, cols].astype(F32) * _mm(yf, wf_ref[:, cols])
        m = m + g2_ref[:, cols].astype(F32) * _mm(yh, wh_ref[:, cols])
        mg_ref[:, cols] = m.astype(BF16)
    mg = mg_ref[...]
    for c0 in range(0, d, MERGE_CHUNK):
        cols = slice(c0, c0 + MERGE_CHUNK)
        o_ref[:, cols] = x_ref[:, cols] + mod_ref[5:6, cols] * _mm(mg, wo_ref[:, cols])
    xn_ref[...] = _modnorm(o_ref[...], nw_ref[2:3, :], mod_ref[6:7, :], mod_ref[7:8, :]).astype(BF16)


def _merge_call(xs, mod, norm_w, proj, y_gla, y_fnet, y_hy, wbg, wbf, wbh, wo, *, layer, rows_per_mod, fixed_row):
    t, d = xs.shape
    tm = 512
    row = _mod_row_map(rows_per_mod, tm, fixed_row)
    gate = lambda j: pl.BlockSpec((pl.Element(tm), pl.Element(d)), lambda i: (i * tm, P_GATE + j * d))
    weight = lambda k: pl.BlockSpec((None, k, d), lambda i: (layer, 0, 0), pipeline_mode=pl.Buffered(1))
    return pl.pallas_call(
        _merge_kernel,
        grid=(t // tm,),
        in_specs=[
            pl.BlockSpec((tm, d), lambda i: (i, 0)),
            pl.BlockSpec((None, None, N_ADA, d), lambda i: (layer, row(i), 0, 0)),
            pl.BlockSpec((None, 3, d), lambda i: (layer, 0, 0)),
            pl.BlockSpec((tm, GLA_VT), lambda i: (i, 0)),
            pl.BlockSpec((tm, FNET_W), lambda i: (i, 0)),
            pl.BlockSpec((tm, HY_W), lambda i: (i, 0)),
            gate(0),
            gate(1),
            gate(2),
            weight(GLA_VT),
            weight(FNET_W),
            weight(HY_W),
            weight(d),
        ],
        out_specs=[pl.BlockSpec((tm, d), lambda i: (i, 0)), pl.BlockSpec((tm, d), lambda i: (i, 0))],
        out_shape=[jax.ShapeDtypeStruct((t, d), F32), jax.ShapeDtypeStruct((t, d), BF16)],
        scratch_shapes=[pltpu.VMEM((tm, d), BF16)],
        compiler_params=_params(("parallel",)),
        name="merge_out",
    )(xs, mod, norm_w, y_gla, y_fnet, y_hy, proj, proj, proj, wbg, wbf, wbh, wo)


def _trig_tables(n, period):
    split = 32
    c = lax.broadcasted_iota(jnp.int32, (1, n), 1)

    def rows(r):
        ang = ((r * c) % period).astype(F32) * (2.0 * math.pi / period)
        return jnp.cos(ang), jnp.sin(ang)

    c_lo, s_lo = rows(lax.broadcasted_iota(jnp.int32, (split, 1), 0))
    c_hi, s_hi = rows(lax.broadcasted_iota(jnp.int32, (n // split, 1), 0) * split)
    cos = c_hi[:, None, :] * c_lo[None] - s_hi[:, None, :] * s_lo[None]
    sin = s_hi[:, None, :] * c_lo[None] + c_hi[:, None, :] * s_lo[None]
    return cos.reshape(n, n), sin.reshape(n, n)


def _fnet_tables(seq):
    cl, sl = _trig_tables(seq, seq)
    cg, sg = _trig_tables(FNET_GW, FNET_GW)
    eye = jnp.eye(FNET_GROUPS, dtype=F32)
    return cl.astype(BF16), sl.astype(BF16), jnp.kron(eye, cg).astype(BF16), jnp.kron(eye, sg).astype(BF16)


def _hyena_tables(seq):
    cm, sm = _trig_tables(seq, 2 * seq)
    r = lax.broadcasted_iota(jnp.int32, (seq, seq), 0)
    c = lax.broadcasted_iota(jnp.int32, (seq, seq), 1)
    sm_fwd = jnp.where(r == 0, (1 - 2 * (c % 2)).astype(F32), sm)
    sm_inv = jnp.where(c == 0, (1 - 2 * (r % 2)).astype(F32), sm)
    fr = lax.broadcasted_iota(jnp.int32, (seq, LANES), 0).astype(F32) * (math.pi / seq)
    return cm.astype(BF16), sm_fwd.astype(BF16), sm_inv.astype(BF16), jnp.cos(fr), jnp.sin(fr)


def _gla_gate_weights(gla_w2, gla_gb):
    depth = gla_w2.shape[0]
    w = gla_w2.reshape(depth, 2, GLA_RANK, GLA_HEADS, GLA_DK).transpose(0, 1, 3, 2, 4)
    zf = jnp.zeros((depth, GLA_HEADS, LANES, GLA_DK), F32)
    w2f = zf.at[:, :, :GLA_RANK].set(w[:, 0]).astype(BF16)
    w2b = zf.at[:, :, GLA_RANK:2 * GLA_RANK].set(w[:, 1]).astype(BF16)
    gb = gla_gb.reshape(depth, 2, GLA_HEADS, 1, GLA_DK)
    return w2f, w2b, gb[:, 0], gb[:, 1]


def kernel(x, c, ctx, c_ctx, ada_w, ada_b, norm_w, ffn1_wi, ffn1_wo, ffn2_wi, ffn2_wo, w_in, gla_w2, gla_gb,
           gla_norm_w, hy_conv_w, hy_conv_b, hy_f1_w, hy_f1_b, hy_f2_w, hy_f2_b, hy_f3_w, hy_freq, hy_bias,
           w_br_gla, w_br_fnet, w_br_hy, w_o, final_norm_w):
    batch, seq, d = x.shape
    ctx_len = ctx.shape[1]
    xs = x.reshape(batch * seq, d)
    cs = ctx.reshape(batch * ctx_len, d)

    cond = jnp.concatenate([c, c_ctx[None, :], jnp.zeros((8 - batch - 1, d), F32)], axis=0)
    mod = _ada_call(cond, ada_w, ada_b)
    ctx_row = batch

    w_t = jnp.swapaxes(w_in, 1, 2)
    wbg, wbf, wbh, wob = (w.astype(BF16) for w in (w_br_gla, w_br_fnet, w_br_hy, w_o))
    w2f, w2b, gbf, gbb = _gla_gate_weights(gla_w2, gla_gb)
    gla_nw = gla_norm_w.reshape(DEPTH, GLA_HEADS, 1, GLA_DV)

    fnet_tab = {n: _fnet_tables(n) for n in (seq, ctx_len)}
    hy_tab = {n: _hyena_tables(n) for n in (seq, ctx_len)}

    def mixers(proj, glow, layer, n, seg, s0f, s0b, need_y):
        y_gla, sf, sb = _gla_call(proj, glow, w2f[layer], w2b[layer], gbf[layer], gbb[layer], gla_nw[layer],
                                  s0f, s0b, batch=batch, seq=n)
        if not need_y:
            return None, sf, sb
        y_fnet = _fnet_call(proj, *fnet_tab[n], batch=batch, seq=n)
        cm, sm, sm_inv, rot_c, rot_s = hy_tab[n]
        w1 = hy_f1_w[layer]
        zpad = jnp.zeros((LANES - HY_BANDS, HY_HID), F32)
        h = _hy_filter_call(w1[0:1], jnp.concatenate([w1[1:1 + HY_BANDS], zpad], 0),
                            jnp.concatenate([w1[1 + HY_BANDS:], zpad], 0), hy_f1_b[layer][None, :],
                            hy_f2_w[layer], hy_f2_b[layer][None, :], hy_f3_w[layer], hy_freq[layer][None, :], seq=n)
        kspec = _hy_spec_call(h, cm, sm, rot_c, rot_s, seq=n)
        cw, cb = hy_conv_w[layer], hy_conv_b[layer][None, :]
        z = _hy_conv_call(proj, P_HY, proj, P_HY + HY_W, cw, cb, 0, 1, hy_bias[layer, 0][None, :], kspec[0], cm, sm,
                          sm_inv, batch=batch, seq=n, seg=seg, conv_u=True)
        y_hy = _hy_conv_call(z, 0, proj, P_HY + 2 * HY_W, cw, cb, 0, 2, hy_bias[layer, 1][None, :], kspec[1], cm, sm,
                             sm_inv, batch=batch, seq=n, seg=seg, conv_u=False)
        return (y_gla, y_fnet, y_hy), sf, sb

    s_zero = jnp.zeros((batch, GLA_HEADS, GLA_DV, GLA_DK), F32)
    for layer in range(DEPTH):
        last = layer == DEPTH - 1
        x_kw = dict(layer=layer, rows_per_mod=seq, fixed_row=None)
        c_kw = dict(layer=layer, rows_per_mod=ctx_len, fixed_row=ctx_row)
        xs = _ffn(xs, _norm_call(xs, mod, norm_w, sub=0, **x_kw), mod, ffn1_wi, ffn1_wo, sub=0, **x_kw)
        cs = _ffn(cs, _norm_call(cs, mod, norm_w, sub=0, **c_kw), mod, ffn1_wi, ffn1_wo, sub=0, **c_kw)
        cn, glow_c = _mixnorm_call(cs, mod, norm_w, w_t, **c_kw)
        proj_c = _proj_call(cn, w_t, layer=layer)
        ys_c, sf, sb = mixers(proj_c, glow_c, layer, ctx_len, ctx_len, s_zero, s_zero, not last)
        if not last:
            cs, cn2 = _merge_call(cs, mod, norm_w, proj_c, *ys_c, wbg, wbf, wbh, wob, **c_kw)
            cs = _ffn(cs, cn2, mod, ffn2_wi, ffn2_wo, sub=2, **c_kw)
        xn, glow_x = _mixnorm_call(xs, mod, norm_w, w_t, **x_kw)
        proj_x = _proj_call(xn, w_t, layer=layer)
        ys_x, _, _ = mixers(proj_x, glow_x, layer, seq, GRID_W, sf, sb, True)
        xs, xn2 = _merge_call(xs, mod, norm_w, proj_x, *ys_x, wbg, wbf, wbh, wob, **x_kw)
        xs = _ffn(xs, xn2, mod, ffn2_wi, ffn2_wo, sub=2, **x_kw)
    return _final_norm_call(xs, final_norm_w.reshape(1, d)).reshape(batch, seq, d)
```

```python
import functools
import math

import jax
import jax.numpy as jnp
from jax import lax
from jax.experimental import pallas as pl
from jax.experimental.pallas import tpu as pltpu

F32 = jnp.float32
BF16 = jnp.bfloat16

D_MODEL = 2048
DEPTH = 4
GRID_W = 64
N_ADA = 9
D_FF = 5504
GLA_HEADS = 4
GLA_DK = 128
GLA_DV = 256
GLA_KT = GLA_HEADS * GLA_DK
GLA_VT = GLA_HEADS * GLA_DV
GLA_RANK = 16
GLA_TAU = 16.0
GLA_CHUNK = 64
FNET_GROUPS = 4
FNET_GW = 128
FNET_W = FNET_GROUPS * FNET_GW
HY_W = 512
HY_BANDS = 16
HY_HID = 64
HY_FAST = 0.3
HY_SLOW = 1.5
HY_TARGET = 1e-2
EPS = 1e-6

LANES = 128
VMEM_LIMIT_BYTES = 56 * 1024 * 1024

FF_TILE = 512
FF_STEPS = -(-D_FF // FF_TILE)

W_GLOW = 2 * GLA_KT + GLA_VT
W_IN_COLS = W_GLOW + 2 * GLA_RANK + GLA_VT + FNET_W + 3 * HY_W + 3 * D_MODEL
P_TILE = 1024
P_Q = 0
P_K = GLA_KT
P_V = 2 * GLA_KT
P_R = W_GLOW
P_FN = P_R + GLA_VT
P_HY = P_FN + FNET_W
P_GATE = P_HY + 3 * HY_W
P_TOTAL = P_GATE + 3 * D_MODEL


def _mm(a, b):
    return jnp.dot(a, b, preferred_element_type=F32)


def _mm_nt(a, b):
    return lax.dot_general(a, b, (((1,), (1,)), ((), ())), preferred_element_type=F32)


def _mm_tn(a, b):
    return lax.dot_general(a, b, (((0,), (0,)), ((), ())), preferred_element_type=F32)


def _split2(a):
    hi = a.astype(BF16)
    lo = (a - hi.astype(F32)).astype(BF16)
    return hi, lo


def _mm3(a, b):
    ah, al = _split2(a)
    bh, bl = _split2(b)
    return _mm(ah, bh) + (_mm(ah, bl) + _mm(al, bh))


def _sigmoid(x):
    return 0.5 * jnp.tanh(0.5 * x) + 0.5


def _silu(x):
    return x * _sigmoid(x)


def _params(sem):
    return pltpu.CompilerParams(dimension_semantics=sem, vmem_limit_bytes=VMEM_LIMIT_BYTES)


def _modnorm(x, nw, shift, scale):
    ms = jnp.mean(x * x, axis=-1, keepdims=True)
    y = x * lax.rsqrt(ms + EPS) * nw
    return y * (1.0 + scale) + shift


def _ada_kernel(c_ref, w_ref, b_ref, o_ref):
    a = _silu(c_ref[...]).astype(BF16)
    o_ref[...] = _mm(a, w_ref[...].astype(BF16)) + b_ref[...]


def _ada_call(cond, ada_w, ada_b):
    depth, d, n = ada_w.shape
    rows = cond.shape[0]
    tn = 1024
    out = pl.pallas_call(
        _ada_kernel,
        grid=(depth, n // tn),
        in_specs=[
            pl.BlockSpec((rows, d), lambda l, j: (0, 0)),
            pl.BlockSpec((None, d, tn), lambda l, j: (l, 0, j)),
            pl.BlockSpec((None, 1, tn), lambda l, j: (l, 0, j)),
        ],
        out_specs=pl.BlockSpec((None, rows, tn), lambda l, j: (l, 0, j)),
        out_shape=jax.ShapeDtypeStruct((depth, rows, n), F32),
        compiler_params=_params(("arbitrary", "arbitrary")),
        name="ada_mod",
    )(cond, ada_w, ada_b.reshape(depth, 1, n))
    return out.reshape(depth, rows, N_ADA, d)


def _mod_row_map(rows_per_mod, tm, fixed_row):
    if fixed_row is not None:
        return lambda i: fixed_row
    per = rows_per_mod // tm
    return lambda i: i // per


def _norm_kernel(x_ref, mod_ref, nw_ref, o_ref, *, sub):
    xn = _modnorm(x_ref[...], nw_ref[sub:sub + 1, :], mod_ref[3 * sub:3 * sub + 1, :],
                  mod_ref[3 * sub + 1:3 * sub + 2, :])
    o_ref[...] = xn.astype(BF16)


def _norm_call(xs, mod, norm_w, *, layer, sub, rows_per_mod, fixed_row):
    t, d = xs.shape
    tm = 1024
    row = _mod_row_map(rows_per_mod, tm, fixed_row)
    return pl.pallas_call(
        functools.partial(_norm_kernel, sub=sub),
        grid=(t // tm,),
        in_specs=[
            pl.BlockSpec((tm, d), lambda i: (i, 0)),
            pl.BlockSpec((None, None, N_ADA, d), lambda i: (layer, row(i), 0, 0)),
            pl.BlockSpec((None, 3, d), lambda i: (layer, 0, 0)),
        ],
        out_specs=pl.BlockSpec((tm, d), lambda i: (i, 0)),
        out_shape=jax.ShapeDtypeStruct((t, d), BF16),
        compiler_params=_params(("parallel",)),
        name="ffn_norm",
    )(xs, mod, norm_w)


def _ffn_up_kernel(xn_ref, wa_ref, wg_ref, h_ref, wab_ref, wgb_ref):
    @pl.when(pl.program_id(1) == 0)
    def _():
        wab_ref[...] = wa_ref[...].astype(BF16)
        wgb_ref[...] = wg_ref[...].astype(BF16)

    xn = xn_ref[...]
    a = _mm(xn, wab_ref[...])
    g = _mm(xn, wgb_ref[...])
    h_ref[...] = (_silu(g) * a).astype(BF16)


def _ffn_up_call(xn, wi, *, layer):
    t, d = xn.shape
    tm = min(2048, t)
    back = (FF_STEPS * FF_TILE - D_FF) // LANES
    hid_blk = lambda f: f * (FF_TILE // LANES) - (f // (FF_STEPS - 1)) * back
    hid0 = lambda f: hid_blk(f) * LANES
    gate0 = lambda f: (D_FF // LANES + hid_blk(f)) * LANES
    return pl.pallas_call(
        _ffn_up_kernel,
        grid=(FF_STEPS, t // tm),
        in_specs=[
            pl.BlockSpec((tm, d), lambda f, i: (i, 0)),
            pl.BlockSpec((None, pl.Element(d), pl.Element(FF_TILE)), lambda f, i: (layer, 0, hid0(f))),
            pl.BlockSpec((None, pl.Element(d), pl.Element(FF_TILE)), lambda f, i: (layer, 0, gate0(f))),
        ],
        out_specs=pl.BlockSpec((pl.Element(tm), pl.Element(FF_TILE)), lambda f, i: (i * tm, hid0(f))),
        out_shape=jax.ShapeDtypeStruct((t, D_FF), BF16),
        scratch_shapes=[pltpu.VMEM((d, FF_TILE), BF16), pltpu.VMEM((d, FF_TILE), BF16)],
        compiler_params=_params(("arbitrary", "arbitrary")),
        name="ffn_up",
    )(xn, wi, wi)


def _ffn_down_kernel(h_ref, w_ref, x_ref, mod_ref, o_ref, wb_ref, *, sub):
    n = pl.program_id(1)

    @pl.when(pl.program_id(0) == 0)
    def _():
        wb_ref[n] = w_ref[...].astype(BF16)

    y = _mm(h_ref[...], wb_ref[n])
    o_ref[...] = x_ref[...] + 0.5 * mod_ref[3 * sub + 2:3 * sub + 3, :] * y


def _ffn_down_call(h, wo, xs, mod, *, layer, sub, rows_per_mod, fixed_row):
    t, d = xs.shape
    tm = 512
    tn = 512
    nt = d // tn
    row = _mod_row_map(rows_per_mod, tm, fixed_row)
    w_col = lambda i, n: jnp.where(i == 0, n, nt - 1)
    return pl.pallas_call(
        functools.partial(_ffn_down_kernel, sub=sub),
        grid=(t // tm, nt),
        in_specs=[
            pl.BlockSpec((tm, D_FF), lambda i, n: (i, 0)),
            pl.BlockSpec((None, D_FF, tn), lambda i, n: (layer, 0, w_col(i, n)), pipeline_mode=pl.Buffered(1)),
            pl.BlockSpec((tm, tn), lambda i, n: (i, n)),
            pl.BlockSpec((None, None, N_ADA, tn), lambda i, n: (layer, row(i), 0, n)),
        ],
        out_specs=pl.BlockSpec((tm, tn), lambda i, n: (i, n)),
        out_shape=jax.ShapeDtypeStruct((t, d), F32),
        scratch_shapes=[pltpu.VMEM((nt, D_FF, tn), BF16)],
        compiler_params=_params(("arbitrary", "arbitrary")),
        name="ffn_down",
    )(h, wo, xs, mod)


def _ffn(xs, xn, mod, wi, wo, *, layer, sub, rows_per_mod, fixed_row):
    h = _ffn_up_call(xn, wi, layer=layer)
    return _ffn_down_call(h, wo, xs, mod, layer=layer, sub=sub, rows_per_mod=rows_per_mod, fixed_row=fixed_row)


def _final_norm_kernel(x_ref, w_ref, o_ref):
    x = x_ref[...]
    ms = jnp.mean(x * x, axis=-1, keepdims=True)
    o_ref[...] = x * lax.rsqrt(ms + EPS) * w_ref[...]


def _final_norm_call(xs, w):
    t, d = xs.shape
    tm = 1024
    return pl.pallas_call(
        _final_norm_kernel,
        grid=(t // tm,),
        in_specs=[pl.BlockSpec((tm, d), lambda i: (i, 0)), pl.BlockSpec((1, d), lambda i: (0, 0))],
        out_specs=pl.BlockSpec((tm, d), lambda i: (i, 0)),
        out_shape=jax.ShapeDtypeStruct((t, d), F32),
        compiler_params=_params(("parallel",)),
        name="final_norm",
    )(xs, w)


def _mixnorm_kernel(x_ref, mod_ref, nw_ref, wl_ref, xn_ref, gl_ref):
    xn = _modnorm(x_ref[...], nw_ref[1:2, :], mod_ref[3:4, :], mod_ref[4:5, :]).astype(BF16)
    xn_ref[...] = xn
    gl_ref[...] = _mm_nt(xn, wl_ref[...].astype(BF16)).astype(BF16)


def _mixnorm_call(xs, mod, norm_w, w_t, *, layer, rows_per_mod, fixed_row):
    t, d = xs.shape
    tm = 1024
    row = _mod_row_map(rows_per_mod, tm, fixed_row)
    return pl.pallas_call(
        _mixnorm_kernel,
        grid=(t // tm,),
        in_specs=[
            pl.BlockSpec((tm, d), lambda i: (i, 0)),
            pl.BlockSpec((None, None, N_ADA, d), lambda i: (layer, row(i), 0, 0)),
            pl.BlockSpec((None, 3, d), lambda i: (layer, 0, 0)),
            pl.BlockSpec((None, LANES, d), lambda i: (layer, W_GLOW // LANES, 0)),
        ],
        out_specs=[pl.BlockSpec((tm, d), lambda i: (i, 0)), pl.BlockSpec((tm, LANES), lambda i: (i, 0))],
        out_shape=[jax.ShapeDtypeStruct((t, d), BF16), jax.ShapeDtypeStruct((t, LANES), BF16)],
        compiler_params=_params(("parallel",)),
        name="mix_norm",
    )(xs, mod, norm_w, w_t)


P_TILES = P_TOTAL // P_TILE
P_SKIP = 2 * GLA_RANK
P_GATE_TILE = P_GATE // P_TILE


def _proj_row0(n):
    sublanes = 8
    first = W_GLOW // P_TILE
    after = (n + (P_TILES - first)) // P_TILES
    return (n * (P_TILE // sublanes) + after * (P_SKIP // sublanes)) * sublanes


def _proj_kernel(xn_ref, w_ref, o_ref, wb_ref):
    @pl.when(pl.program_id(1) == 0)
    def _():
        wb_ref[...] = w_ref[...].astype(BF16)

    @pl.when(pl.program_id(0) < P_GATE_TILE)
    def _():
        o_ref[...] = _mm_nt(xn_ref[...], wb_ref[...]).astype(BF16)

    @pl.when(pl.program_id(0) >= P_GATE_TILE)
    def _():
        o_ref[...] = _sigmoid(_mm_nt(xn_ref[...], wb_ref[...])).astype(BF16)


def _proj_call(xn, w_t, *, layer):
    t, d = xn.shape
    tm = min(2048, t)
    return pl.pallas_call(
        _proj_kernel,
        grid=(P_TILES, t // tm),
        in_specs=[
            pl.BlockSpec((tm, d), lambda n, i: (i, 0)),
            pl.BlockSpec((None, pl.Element(P_TILE), pl.Element(d)), lambda n, i: (layer, _proj_row0(n), 0)),
        ],
        out_specs=pl.BlockSpec((tm, P_TILE), lambda n, i: (i, n)),
        out_shape=jax.ShapeDtypeStruct((t, P_TOTAL), BF16),
        scratch_shapes=[pltpu.VMEM((P_TILE, d), BF16)],
        compiler_params=_params(("arbitrary", "arbitrary")),
        name="mix_proj",
    )(xn, w_t)


def _log_sigmoid(x):
    return jnp.minimum(x, 0.0) - jnp.log(1.0 + jnp.exp(-jnp.abs(x)))


def _chunk_scan(x, reverse):
    n, w = x.shape
    sub = 8
    tiles = GLA_CHUNK // sub
    x3 = x.reshape(n // sub, sub, w)
    pos = lax.broadcasted_iota(jnp.int32, x3.shape, 1)
    s = 1
    while s < sub:
        if reverse:
            x3 = x3 + jnp.where(pos < sub - s, pltpu.roll(x3, sub - s, axis=1), 0.0)
        else:
            x3 = x3 + jnp.where(pos >= s, pltpu.roll(x3, s, axis=1), 0.0)
        s *= 2
    tot = jnp.broadcast_to(x3[:, 0:1, :] if reverse else x3[:, sub - 1:sub, :], x3.shape)
    tpos = lax.broadcasted_iota(jnp.int32, x3.shape, 0) % tiles
    acc = tot
    s = 1
    while s < tiles:
        pad = jnp.zeros((s, sub, w), F32)
        if reverse:
            acc = acc + jnp.where(tpos < tiles - s, jnp.concatenate([acc[s:], pad], axis=0), 0.0)
        else:
            acc = acc + jnp.where(tpos >= s, jnp.concatenate([pad, acc[:-s]], axis=0), 0.0)
        s *= 2
    return (x3 + (acc - tot)).reshape(n, w)


def _gla_kernel(q_ref, k_ref, v_ref, r_ref, gl_ref, w2f_ref, w2b_ref, gbf_ref, gbb_ref, nw_ref,
                s0f_ref, s0b_ref, y_ref, sf_ref, sb_ref, qd_ref, kd_ref, ke_ref, dec_ref, of_ref, ob_ref,
                att_ref, kv_ref, sp_ref, *, seq):
    c = GLA_CHUNK
    n_chunks = seq // c
    glow = gl_ref[...]
    q = q_ref[...].astype(F32) * GLA_DK ** -0.5
    k = k_ref[...].astype(F32)
    for d, (w2_ref, gb_ref) in enumerate(((w2f_ref, gbf_ref), (w2b_ref, gbb_ref))):
        lg = _log_sigmoid(_mm(glow, w2_ref[...]) + gb_ref[...]) * (1.0 / GLA_TAU)
        gc = _chunk_scan(lg, reverse=d == 1)
        gc3 = gc.reshape(n_chunks, c, GLA_DK)
        gt = gc3[:, 0:1, :] if d == 1 else gc3[:, c - 1:c, :]
        qd_ref[d] = (q * jnp.exp(gc)).astype(BF16)
        kd_ref[d] = (k * jnp.exp(-gc)).astype(BF16)
        ke_ref[d] = (k.reshape(n_chunks, c, GLA_DK) * jnp.exp(gt - gc3)).reshape(seq, GLA_DK).astype(BF16)
        dec_ref[d] = jnp.exp(gt)
    sf_ref[...] = s0f_ref[...]
    sb_ref[...] = s0b_ref[...]

    ri = lax.broadcasted_iota(jnp.int32, (c, c), 0)
    ci = lax.broadcasted_iota(jnp.int32, (c, c), 1)

    unroll = min(8, n_chunks)
    dirs = ((0, ri >= ci, sf_ref, of_ref), (1, ri <= ci, sb_ref, ob_ref))

    def local_body(n, carry):
        r0 = pl.multiple_of(n * c, c)
        v = v_ref[pl.ds(r0, c), :]
        for d, mask, _, _ in dirs:
            scores = _mm_nt(qd_ref[d, pl.ds(r0, c), :], kd_ref[d, pl.ds(r0, c), :])
            att_ref[d, n] = jnp.where(mask, scores, 0.0).astype(BF16)
            kv_ref[d, n] = _mm_tn(v, ke_ref[d, pl.ds(r0, c), :])
        return carry

    lax.fori_loop(0, n_chunks, local_body, 0, unroll=unroll)

    def scan_body(i, carry):
        for d, _, st_ref, _ in dirs:
            n = i if d == 0 else n_chunks - 1 - i
            st = st_ref[...]
            sp_ref[d, n] = st.astype(BF16)
            st_ref[...] = st * dec_ref[d, n] + kv_ref[d, n]
        return carry

    lax.fori_loop(0, n_chunks, scan_body, 0, unroll=min(4, n_chunks))

    def out_body(n, carry):
        r0 = pl.multiple_of(n * c, c)
        v = v_ref[pl.ds(r0, c), :]
        for d, _, _, o_ref in dirs:
            o_ref[pl.ds(r0, c), :] = _mm(att_ref[d, n], v) + _mm_nt(qd_ref[d, pl.ds(r0, c), :], sp_ref[d, n])
        return carry

    lax.fori_loop(0, n_chunks, out_body, 0, unroll=unroll)

    o = of_ref[...] + ob_ref[...]
    ms = jnp.mean(o * o, axis=-1, keepdims=True)
    y = o * lax.rsqrt(ms + EPS) * nw_ref[...] * _silu(r_ref[...].astype(F32))
    y_ref[...] = y.astype(BF16)


def _gla_call(proj, glow, w2f, w2b, gbf, gbb, gla_nw, s0f, s0b, *, batch, seq):
    t = proj.shape[0]
    h = GLA_HEADS
    qb, kb = P_Q // GLA_DK, P_K // GLA_DK
    vb, rb = P_V // GLA_DV, P_R // GLA_DV
    st_spec = pl.BlockSpec((None, None, GLA_DV, GLA_DK), lambda b, hh: (b, hh, 0, 0))
    st_shape = jax.ShapeDtypeStruct((batch, h, GLA_DV, GLA_DK), F32)
    return pl.pallas_call(
        functools.partial(_gla_kernel, seq=seq),
        grid=(batch, h),
        in_specs=[
            pl.BlockSpec((seq, GLA_DK), lambda b, hh: (b, qb + hh)),
            pl.BlockSpec((seq, GLA_DK), lambda b, hh: (b, kb + hh)),
            pl.BlockSpec((seq, GLA_DV), lambda b, hh: (b, vb + hh)),
            pl.BlockSpec((seq, GLA_DV), lambda b, hh: (b, rb + hh)),
            pl.BlockSpec((seq, LANES), lambda b, hh: (b, 0)),
            pl.BlockSpec((None, LANES, GLA_DK), lambda b, hh: (hh, 0, 0)),
            pl.BlockSpec((None, LANES, GLA_DK), lambda b, hh: (hh, 0, 0)),
            pl.BlockSpec((None, 1, GLA_DK), lambda b, hh: (hh, 0, 0)),
            pl.BlockSpec((None, 1, GLA_DK), lambda b, hh: (hh, 0, 0)),
            pl.BlockSpec((None, 1, GLA_DV), lambda b, hh: (hh, 0, 0)),
            st_spec,
            st_spec,
        ],
        out_specs=[pl.BlockSpec((seq, GLA_DV), lambda b, hh: (b, hh)), st_spec, st_spec],
        out_shape=[jax.ShapeDtypeStruct((t, GLA_VT), BF16), st_shape, st_shape],
        scratch_shapes=[
            pltpu.VMEM((2, seq, GLA_DK), BF16),
            pltpu.VMEM((2, seq, GLA_DK), BF16),
            pltpu.VMEM((2, seq, GLA_DK), BF16),
            pltpu.VMEM((2, seq // GLA_CHUNK, 1, GLA_DK), F32),
            pltpu.VMEM((seq, GLA_DV), F32),
            pltpu.VMEM((seq, GLA_DV), F32),
            pltpu.VMEM((2, seq // GLA_CHUNK, GLA_CHUNK, GLA_CHUNK), BF16),
            pltpu.VMEM((2, seq // GLA_CHUNK, GLA_DV, GLA_DK), F32),
            pltpu.VMEM((2, seq // GLA_CHUNK, GLA_DV, GLA_DK), BF16),
        ],
        compiler_params=_params(("parallel", "parallel")),
        name="gla",
    )(proj, proj, proj, proj, glow, w2f, w2b, gbf, gbb, gla_nw, s0f, s0b)


def _fnet_kernel(u_ref, cl_ref, sl_ref, cg_ref, sg_ref, o_ref, *, scale):
    u = u_ref[...]
    p = _mm(cl_ref[...], u).astype(BF16)
    q = _mm(sl_ref[...], u).astype(BF16)
    y = _mm(p, cg_ref[...]) - _mm(q, sg_ref[...])
    o_ref[...] = (y * scale).astype(BF16)


def _fnet_call(proj, cl, sl, cg, sg, *, batch, seq):
    t = proj.shape[0]
    tr = min(512, seq)
    nr = seq // tr
    ub = P_FN // FNET_W
    return pl.pallas_call(
        functools.partial(_fnet_kernel, scale=1.0 / math.sqrt(seq * FNET_GW)),
        grid=(nr, batch),
        in_specs=[
            pl.BlockSpec((seq, FNET_W), lambda r, b: (b, ub)),
            pl.BlockSpec((tr, seq), lambda r, b: (r, 0)),
            pl.BlockSpec((tr, seq), lambda r, b: (r, 0)),
            pl.BlockSpec((FNET_W, FNET_W), lambda r, b: (0, 0)),
            pl.BlockSpec((FNET_W, FNET_W), lambda r, b: (0, 0)),
        ],
        out_specs=pl.BlockSpec((tr, FNET_W), lambda r, b: (b * nr + r, 0)),
        out_shape=jax.ShapeDtypeStruct((t, FNET_W), BF16),
        compiler_params=_params(("arbitrary", "arbitrary")),
        name="fnet",
    )(proj, cl, sl, cg, sg)


def _hy_filter_kernel(w1t_ref, w1c_ref, w1s_ref, b1_ref, w2_ref, b2_ref, w3_ref, fr_ref, o_ref, *, seq):
    ti = lax.broadcasted_iota(jnp.int32, (seq, LANES), 0).astype(F32)
    lane = lax.broadcasted_iota(jnp.int32, (seq, LANES), 1)
    band_step = (HY_BANDS - 1 - 1e-4) / (HY_BANDS - 1)
    bands = jnp.where(lane < HY_BANDS, 1e-4 + lane.astype(F32) * band_step, 0.0)
    ang = 2.0 * math.pi * ti * bands / seq
    t_col = ti[:, 0:1] / (seq - 1.0)
    fr = fr_ref[...]
    pre1 = t_col * w1t_ref[...] + _mm3(jnp.cos(ang), w1c_ref[...]) + _mm3(-jnp.sin(ang), w1s_ref[...]) + b1_ref[...]
    h1 = jnp.sin(fr * pre1)
    h2 = jnp.sin(fr * (_mm3(h1, w2_ref[...]) + b2_ref[...]))
    ch = lax.broadcasted_iota(jnp.int32, (1, HY_W), 1).astype(F32)
    d0 = math.log(HY_TARGET) / HY_FAST
    d1 = math.log(HY_TARGET) / HY_SLOW
    deltas = jnp.abs(d0 + ch * ((d1 - d0) / (HY_W - 1)))
    win = jnp.exp(-t_col * deltas)
    ss = jnp.zeros((1, HY_W), F32)
    for lo in (0, HY_W):
        hd = _mm3(h2, w3_ref[:, lo:lo + HY_W]) * win
        o_ref[:, lo:lo + HY_W] = hd
        ss = ss + jnp.sum(hd * hd, axis=0, keepdims=True)
    inv = lax.rsqrt(ss + EPS)
    for lo in (0, HY_W):
        o_ref[:, lo:lo + HY_W] = o_ref[:, lo:lo + HY_W] * inv


def _hy_filter_call(w1t, w1c, w1s, b1, w2, b2, w3, fr, *, seq):
    hid = HY_HID
    full = lambda shape: pl.BlockSpec(shape, lambda o: (0,) * len(shape))
    return pl.pallas_call(
        functools.partial(_hy_filter_kernel, seq=seq),
        grid=(2,),
        in_specs=[
            full((1, hid)), full((LANES, hid)), full((LANES, hid)), full((1, hid)),
            full((hid, hid)), full((1, hid)),
            pl.BlockSpec((hid, 2 * HY_W), lambda o: (0, o)),
            full((1, hid)),
        ],
        out_specs=pl.BlockSpec((seq, 2 * HY_W), lambda o: (0, o)),
        out_shape=jax.ShapeDtypeStruct((seq, 4 * HY_W), F32),
        compiler_params=_params(("arbitrary",)),
        name="hy_filter",
    )(w1t, w1c, w1s, b1, w2, b2, w3, fr)


def _hy_spec_kernel(h_ref, c_ref, s_ref, rc_ref, rs_ref, o_ref, hb_ref, *, seq, tf):
    f = pl.program_id(1)

    @pl.when(f == 0)
    def _():
        hb_ref[...] = h_ref[...].astype(BF16)

    cm = c_ref[...]
    sm = s_ref[...]
    hfw = hb_ref[:, :HY_W]
    hbw = hb_ref[:, HY_W:]
    pf = _mm(cm, hfw)
    pb = _mm(cm, hbw)
    qf = _mm(sm, hfw)
    qb = _mm(sm, hbw)
    reps = HY_W // LANES
    cf = jnp.concatenate([rc_ref[...]] * reps, axis=1)
    sf = jnp.concatenate([rs_ref[...]] * reps, axis=1)
    kre = pf + cf * pb - sf * qb
    kim = cf * qb + sf * pb - qf
    row0 = (lax.broadcasted_iota(jnp.int32, (tf, HY_W), 0) + f * tf) == 0
    wgt = jnp.where(row0, 0.5 / seq, 1.0 / seq)
    k1 = kre * wgt
    o_ref[0] = k1
    o_ref[1] = jnp.where(row0, 0.0, kim * wgt)
    o_ref[2] = jnp.where(row0, (qf - qb) * (0.5 / seq), k1)


def _hy_spec_call(h, cmat, smat, rot_c, rot_s, *, seq):
    tf = min(512, seq)
    nf = seq // tf
    return pl.pallas_call(
        functools.partial(_hy_spec_kernel, seq=seq, tf=tf),
        grid=(2, nf),
        in_specs=[
            pl.BlockSpec((seq, 2 * HY_W), lambda o, f: (0, o)),
            pl.BlockSpec((tf, seq), lambda o, f: (f, 0)),
            pl.BlockSpec((tf, seq), lambda o, f: (f, 0)),
            pl.BlockSpec((tf, LANES), lambda o, f: (f, 0)),
            pl.BlockSpec((tf, LANES), lambda o, f: (f, 0)),
        ],
        out_specs=pl.BlockSpec((None, 3, tf, HY_W), lambda o, f: (o, 0, f, 0)),
        out_shape=jax.ShapeDtypeStruct((2, 3, seq, HY_W), F32),
        scratch_shapes=[pltpu.VMEM((seq, 2 * HY_W), BF16)],
        compiler_params=_params(("arbitrary", "arbitrary")),
        name="hy_spectrum",
    )(h, cmat, smat, rot_c, rot_s)


def _short_conv(u, w, b, seg):
    n = u.shape[0]
    pos = lax.broadcasted_iota(jnp.int32, u.shape, 0) % seg
    prev = jnp.where(pos == 0, 0.0, pltpu.roll(u, 1, axis=0))
    nxt = jnp.where(pos == seg - 1, 0.0, pltpu.roll(u, n - 1, axis=0))
    return prev * w[0:1, :] + u * w[1:2, :] + nxt * w[2:3, :] + b


def _hy_conv_kernel(u_ref, m_ref, wu_ref, bu_ref, wm_ref, bm_ref, bias_ref, k_ref, cr_ref, sr_ref, cc_ref, sc_ref,
                    o_ref, ub_ref, acc_ref, *, conv_u, seg, nf, batch, seq):
    f = pl.program_id(1)
    tc = o_ref.shape[1]

    def load_u(b):
        u = u_ref[b * seq:(b + 1) * seq, :].astype(F32)
        if conv_u:
            u = _short_conv(u, wu_ref[...], bu_ref[...], seg)
        return u

    @pl.when(f == 0)
    def _():
        for b in range(batch):
            ub_ref[:, b * tc:(b + 1) * tc] = load_u(b).astype(BF16)
        acc_ref[...] = jnp.zeros(acc_ref.shape, F32)

    ub = ub_ref[...]
    p = _mm(cr_ref[...], ub)
    q = _mm(sr_ref[...], ub)
    k1, k2, k3 = (jnp.concatenate([k_ref[j]] * batch, axis=1) for j in range(3))
    av = (p * k1 + q * k2).astype(BF16)
    bv = (q * k3 - p * k2).astype(BF16)
    acc_ref[...] += _mm(cc_ref[...], av) + _mm(sc_ref[...], bv)

    @pl.when(f == nf - 1)
    def _():
        for b in range(batch):
            m = _short_conv(m_ref[b * seq:(b + 1) * seq, :].astype(F32), wm_ref[...], bm_ref[...], seg)
            y = acc_ref[:, b * tc:(b + 1) * tc] + bias_ref[...] * ub_ref[:, b * tc:(b + 1) * tc].astype(F32)
            o_ref[b * seq:(b + 1) * seq, :] = (m * y).astype(o_ref.dtype)


def _hy_conv_call(u_arr, u_col, m_arr, m_col, conv_w, conv_b, u_sect, m_sect, bias, kspec, cmat, smat, smat_inv,
                  *, batch, seq, seg, conv_u):
    t = u_arr.shape[0]
    tc = 256
    nch = HY_W // tc
    tf = min(256, seq)
    nf = seq // tf
    ucb, mcb = u_col // tc, m_col // tc
    usb, msb = u_sect * nch, m_sect * nch
    once = pl.Buffered(1)
    return pl.pallas_call(
        functools.partial(_hy_conv_kernel, conv_u=conv_u, seg=seg, nf=nf, batch=batch, seq=seq),
        grid=(nch, nf),
        in_specs=[
            pl.BlockSpec((t, tc), lambda ch, f: (0, ucb + ch), pipeline_mode=once),
            pl.BlockSpec((t, tc), lambda ch, f: (0, mcb + ch), pipeline_mode=once),
            pl.BlockSpec((3, tc), lambda ch, f: (0, usb + ch)),
            pl.BlockSpec((1, tc), lambda ch, f: (0, usb + ch)),
            pl.BlockSpec((3, tc), lambda ch, f: (0, msb + ch)),
            pl.BlockSpec((1, tc), lambda ch, f: (0, msb + ch)),
            pl.BlockSpec((1, tc), lambda ch, f: (0, ch)),
            pl.BlockSpec((3, tf, tc), lambda ch, f: (0, f, ch)),
            pl.BlockSpec((tf, seq), lambda ch, f: (f, 0)),
            pl.BlockSpec((tf, seq), lambda ch, f: (f, 0)),
            pl.BlockSpec((seq, tf), lambda ch, f: (0, f)),
            pl.BlockSpec((seq, tf), lambda ch, f: (0, f)),
        ],
        out_specs=pl.BlockSpec((t, tc), lambda ch, f: (0, ch)),
        out_shape=jax.ShapeDtypeStruct((t, HY_W), BF16),
        scratch_shapes=[pltpu.VMEM((seq, batch * tc), BF16), pltpu.VMEM((seq, batch * tc), F32)],
        compiler_params=_params(("arbitrary", "arbitrary")),
        name="hy_conv",
    )(u_arr, m_arr, conv_w, conv_b, conv_w, conv_b, bias, kspec, cmat, smat, cmat, smat_inv)


MERGE_CHUNK = 512


def _merge_kernel(x_ref, mod_ref, nw_ref, yg_ref, yf_ref, yh_ref, g0_ref, g1_ref, g2_ref, wg_ref, wf_ref, wh_ref,
                  wo_ref, o_ref, xn_ref, mg_ref):
    d = o_ref.shape[1]
    yg = yg_ref[...]
    yf = yf_ref[...]
    yh = yh_ref[...]
    for c0 in range(0, d, MERGE_CHUNK):
        cols = slice(c0, c0 + MERGE_CHUNK)
        m = g0_ref[:, cols].astype(F32) * _mm(yg, wg_ref[:, cols])
        m = m + g1_ref[:, cols].astype(F32) * _mm(yf, wf_ref[:, cols])
        m = m + g2_ref[:, cols].astype(F32) * _mm(yh, wh_ref[:, cols])
        mg_ref[:, cols] = m.astype(BF16)
    mg = mg_ref[...]
    for c0 in range(0, d, MERGE_CHUNK):
        cols = slice(c0, c0 + MERGE_CHUNK)
        o_ref[:, cols] = x_ref[:, cols] + mod_ref[5:6, cols] * _mm(mg, wo_ref[:, cols])
    xn_ref[...] = _modnorm(o_ref[...], nw_ref[2:3, :], mod_ref[6:7, :], mod_ref[7:8, :]).astype(BF16)


def _merge_call(xs, mod, norm_w, proj, y_gla, y_fnet, y_hy, wbg, wbf, wbh, wo, *, layer, rows_per_mod, fixed_row):
    t, d = xs.shape
    tm = 512
    row = _mod_row_map(rows_per_mod, tm, fixed_row)
    gate = lambda j: pl.BlockSpec((pl.Element(tm), pl.Element(d)), lambda i: (i * tm, P_GATE + j * d))
    weight = lambda k: pl.BlockSpec((None, k, d), lambda i: (layer, 0, 0), pipeline_mode=pl.Buffered(1))
    return pl.pallas_call(
        _merge_kernel,
        grid=(t // tm,),
        in_specs=[
            pl.BlockSpec((tm, d), lambda i: (i, 0)),
            pl.BlockSpec((None, None, N_ADA, d), lambda i: (layer, row(i), 0, 0)),
            pl.BlockSpec((None, 3, d), lambda i: (layer, 0, 0)),
            pl.BlockSpec((tm, GLA_VT), lambda i: (i, 0)),
            pl.BlockSpec((tm, FNET_W), lambda i: (i, 0)),
            pl.BlockSpec((tm, HY_W), lambda i: (i, 0)),
            gate(0),
            gate(1),
            gate(2),
            weight(GLA_VT),
            weight(FNET_W),
            weight(HY_W),
            weight(d),
        ],
        out_specs=[pl.BlockSpec((tm, d), lambda i: (i, 0)), pl.BlockSpec((tm, d), lambda i: (i, 0))],
        out_shape=[jax.ShapeDtypeStruct((t, d), F32), jax.ShapeDtypeStruct((t, d), BF16)],
        scratch_shapes=[pltpu.VMEM((tm, d), BF16)],
        compiler_params=_params(("parallel",)),
        name="merge_out",
    )(xs, mod, norm_w, y_gla, y_fnet, y_hy, proj, proj, proj, wbg, wbf, wbh, wo)


def _trig_tables(n, period):
    split = 32
    c = lax.broadcasted_iota(jnp.int32, (1, n), 1)

    def rows(r):
        ang = ((r * c) % period).astype(F32) * (2.0 * math.pi / period)
        return jnp.cos(ang), jnp.sin(ang)

    c_lo, s_lo = rows(lax.broadcasted_iota(jnp.int32, (split, 1), 0))
    c_hi, s_hi = rows(lax.broadcasted_iota(jnp.int32, (n // split, 1), 0) * split)
    cos = c_hi[:, None, :] * c_lo[None] - s_hi[:, None, :] * s_lo[None]
    sin = s_hi[:, None, :] * c_lo[None] + c_hi[:, None, :] * s_lo[None]
    return cos.reshape(n, n), sin.reshape(n, n)


def _fnet_tables(seq):
    cl, sl = _trig_tables(seq, seq)
    cg, sg = _trig_tables(FNET_GW, FNET_GW)
    eye = jnp.eye(FNET_GROUPS, dtype=F32)
    return cl.astype(BF16), sl.astype(BF16), jnp.kron(eye, cg).astype(BF16), jnp.kron(eye, sg).astype(BF16)


def _hyena_tables(seq):
    cm, sm = _trig_tables(seq, 2 * seq)
    r = lax.broadcasted_iota(jnp.int32, (seq, seq), 0)
    c = lax.broadcasted_iota(jnp.int32, (seq, seq), 1)
    sm_fwd = jnp.where(r == 0, (1 - 2 * (c % 2)).astype(F32), sm)
    sm_inv = jnp.where(c == 0, (1 - 2 * (r % 2)).astype(F32), sm)
    fr = lax.broadcasted_iota(jnp.int32, (seq, LANES), 0).astype(F32) * (math.pi / seq)
    return cm.astype(BF16), sm_fwd.astype(BF16), sm_inv.astype(BF16), jnp.cos(fr), jnp.sin(fr)


def _gla_gate_weights(gla_w2, gla_gb):
    depth = gla_w2.shape[0]
    w = gla_w2.reshape(depth, 2, GLA_RANK, GLA_HEADS, GLA_DK).transpose(0, 1, 3, 2, 4)
    zf = jnp.zeros((depth, GLA_HEADS, LANES, GLA_DK), F32)
    w2f = zf.at[:, :, :GLA_RANK].set(w[:, 0]).astype(BF16)
    w2b = zf.at[:, :, GLA_RANK:2 * GLA_RANK].set(w[:, 1]).astype(BF16)
    gb = gla_gb.reshape(depth, 2, GLA_HEADS, 1, GLA_DK)
    return w2f, w2b, gb[:, 0], gb[:, 1]


def kernel(x, c, ctx, c_ctx, ada_w, ada_b, norm_w, ffn1_wi, ffn1_wo, ffn2_wi, ffn2_wo, w_in, gla_w2, gla_gb,
           gla_norm_w, hy_conv_w, hy_conv_b, hy_f1_w, hy_f1_b, hy_f2_w, hy_f2_b, hy_f3_w, hy_freq, hy_bias,
           w_br_gla, w_br_fnet, w_br_hy, w_o, final_norm_w):
    batch, seq, d = x.shape
    ctx_len = ctx.shape[1]
    xs = x.reshape(batch * seq, d)
    cs = ctx.reshape(batch * ctx_len, d)

    cond = jnp.concatenate([c, c_ctx[None, :], jnp.zeros((8 - batch - 1, d), F32)], axis=0)
    mod = _ada_call(cond, ada_w, ada_b)
    ctx_row = batch

    w_t = jnp.swapaxes(w_in, 1, 2)
    wbg, wbf, wbh, wob = (w.astype(BF16) for w in (w_br_gla, w_br_fnet, w_br_hy, w_o))
    w2f, w2b, gbf, gbb = _gla_gate_weights(gla_w2, gla_gb)
    gla_nw = gla_norm_w.reshape(DEPTH, GLA_HEADS, 1, GLA_DV)

    fnet_tab = {n: _fnet_tables(n) for n in (seq, ctx_len)}
    hy_tab = {n: _hyena_tables(n) for n in (seq, ctx_len)}

    def mixers(proj, glow, layer, n, seg, s0f, s0b, need_y):
        y_gla, sf, sb = _gla_call(proj, glow, w2f[layer], w2b[layer], gbf[layer], gbb[layer], gla_nw[layer],
                                  s0f, s0b, batch=batch, seq=n)
        if not need_y:
            return None, sf, sb
        y_fnet = _fnet_call(proj, *fnet_tab[n], batch=batch, seq=n)
        cm, sm, sm_inv, rot_c, rot_s = hy_tab[n]
        w1 = hy_f1_w[layer]
        zpad = jnp.zeros((LANES - HY_BANDS, HY_HID), F32)
        h = _hy_filter_call(w1[0:1], jnp.concatenate([w1[1:1 + HY_BANDS], zpad], 0),
                            jnp.concatenate([w1[1 + HY_BANDS:], zpad], 0), hy_f1_b[layer][None, :],
                            hy_f2_w[layer], hy_f2_b[layer][None, :], hy_f3_w[layer], hy_freq[layer][None, :], seq=n)
        kspec = _hy_spec_call(h, cm, sm, rot_c, rot_s, seq=n)
        cw, cb = hy_conv_w[layer], hy_conv_b[layer][None, :]
        z = _hy_conv_call(proj, P_HY, proj, P_HY + HY_W, cw, cb, 0, 1, hy_bias[layer, 0][None, :], kspec[0], cm, sm,
                          sm_inv, batch=batch, seq=n, seg=seg, conv_u=True)
        y_hy = _hy_conv_call(z, 0, proj, P_HY + 2 * HY_W, cw, cb, 0, 2, hy_bias[layer, 1][None, :], kspec[1], cm, sm,
                             sm_inv, batch=batch, seq=n, seg=seg, conv_u=False)
        return (y_gla, y_fnet, y_hy), sf, sb

    s_zero = jnp.zeros((batch, GLA_HEADS, GLA_DV, GLA_DK), F32)
    for layer in range(DEPTH):
        last = layer == DEPTH - 1
        x_kw = dict(layer=layer, rows_per_mod=seq, fixed_row=None)
        c_kw = dict(layer=layer, rows_per_mod=ctx_len, fixed_row=ctx_row)
        xs = _ffn(xs, _norm_call(xs, mod, norm_w, sub=0, **x_kw), mod, ffn1_wi, ffn1_wo, sub=0, **x_kw)
        cs = _ffn(cs, _norm_call(cs, mod, norm_w, sub=0, **c_kw), mod, ffn1_wi, ffn1_wo, sub=0, **c_kw)
        cn, glow_c = _mixnorm_call(cs, mod, norm_w, w_t, **c_kw)
        proj_c = _proj_call(cn, w_t, layer=layer)
        ys_c, sf, sb = mixers(proj_c, glow_c, layer, ctx_len, ctx_len, s_zero, s_zero, not last)
        if not last:
            cs, cn2 = _merge_call(cs, mod, norm_w, proj_c, *ys_c, wbg, wbf, wbh, wob, **c_kw)
            cs = _ffn(cs, cn2, mod, ffn2_wi, ffn2_wo, sub=2, **c_kw)
        xn, glow_x = _mixnorm_call(xs, mod, norm_w, w_t, **x_kw)
        proj_x = _proj_call(xn, w_t, layer=layer)
        ys_x, _, _ = mixers(proj_x, glow_x, layer, seq, GRID_W, sf, sb, True)
        xs, xn2 = _merge_call(xs, mod, norm_w, proj_x, *ys_x, wbg, wbf, wbh, wob, **x_kw)
        xs = _ffn(xs, xn2, mod, ffn2_wi, ffn2_wo, sub=2, **x_kw)
    return _final_norm_call(xs, final_norm_w.reshape(1, d)).reshape(batch, seq, d)
```

```python
import functools
import math

import jax
import jax.numpy as jnp
from jax import lax
from jax.experimental import pallas as pl
from jax.experimental.pallas import tpu as pltpu

F32 = jnp.float32
BF16 = jnp.bfloat16

D_MODEL = 2048
DEPTH = 4
GRID_W = 64
N_ADA = 9
D_FF = 5504
GLA_HEADS = 4
GLA_DK = 128
GLA_DV = 256
GLA_KT = GLA_HEADS * GLA_DK
GLA_VT = GLA_HEADS * GLA_DV
GLA_RANK = 16
GLA_TAU = 16.0
GLA_CHUNK = 64
FNET_GROUPS = 4
FNET_GW = 128
FNET_W = FNET_GROUPS * FNET_GW
HY_W = 512
HY_BANDS = 16
HY_HID = 64
HY_FAST = 0.3
HY_SLOW = 1.5
HY_TARGET = 1e-2
EPS = 1e-6

LANES = 128
VMEM_LIMIT_BYTES = 56 * 1024 * 1024

FF_TILE = 512
FF_STEPS = -(-D_FF // FF_TILE)

W_GLOW = 2 * GLA_KT + GLA_VT
W_IN_COLS = W_GLOW + 2 * GLA_RANK + GLA_VT + FNET_W + 3 * HY_W + 3 * D_MODEL
P_TILE = 1024
P_Q = 0
P_K = GLA_KT
P_V = 2 * GLA_KT
P_R = W_GLOW
P_FN = P_R + GLA_VT
P_HY = P_FN + FNET_W
P_GATE = P_HY + 3 * HY_W
P_TOTAL = P_GATE + 3 * D_MODEL


def _mm(a, b):
    return jnp.dot(a, b, preferred_element_type=F32)


def _mm_nt(a, b):
    return lax.dot_general(a, b, (((1,), (1,)), ((), ())), preferred_element_type=F32)


def _mm_tn(a, b):
    return lax.dot_general(a, b, (((0,), (0,)), ((), ())), preferred_element_type=F32)


def _split2(a):
    hi = a.astype(BF16)
    lo = (a - hi.astype(F32)).astype(BF16)
    return hi, lo


def _mm3(a, b):
    ah, al = _split2(a)
    bh, bl = _split2(b)
    return _mm(ah, bh) + (_mm(ah, bl) + _mm(al, bh))


def _sigmoid(x):
    return 0.5 * jnp.tanh(0.5 * x) + 0.5


def _silu(x):
    return x * _sigmoid(x)


def _params(sem):
    return pltpu.CompilerParams(dimension_semantics=sem, vmem_limit_bytes=VMEM_LIMIT_BYTES)


def _modnorm(x, nw, shift, scale):
    ms = jnp.mean(x * x, axis=-1, keepdims=True)
    y = x * lax.rsqrt(ms + EPS) * nw
    return y * (1.0 + scale) + shift


def _ada_kernel(c_ref, w_ref, b_ref, o_ref):
    a = _silu(c_ref[...]).astype(BF16)
    o_ref[...] = _mm(a, w_ref[...].astype(BF16)) + b_ref[...]


def _ada_call(cond, ada_w, ada_b):
    depth, d, n = ada_w.shape
    rows = cond.shape[0]
    tn = 1024
    out = pl.pallas_call(
        _ada_kernel,
        grid=(depth, n // tn),
        in_specs=[
            pl.BlockSpec((rows, d), lambda l, j: (0, 0)),
            pl.BlockSpec((None, d, tn), lambda l, j: (l, 0, j)),
            pl.BlockSpec((None, 1, tn), lambda l, j: (l, 0, j)),
        ],
        out_specs=pl.BlockSpec((None, rows, tn), lambda l, j: (l, 0, j)),
        out_shape=jax.ShapeDtypeStruct((depth, rows, n), F32),
        compiler_params=_params(("arbitrary", "arbitrary")),
        name="ada_mod",
    )(cond, ada_w, ada_b.reshape(depth, 1, n))
    return out.reshape(depth, rows, N_ADA, d)


def _mod_row_map(rows_per_mod, tm, fixed_row):
    if fixed_row is not None:
        return lambda i: fixed_row
    per = rows_per_mod // tm
    return lambda i: i // per


def _norm_kernel(x_ref, mod_ref, nw_ref, o_ref, *, sub):
    xn = _modnorm(x_ref[...], nw_ref[sub:sub + 1, :], mod_ref[3 * sub:3 * sub + 1, :],
                  mod_ref[3 * sub + 1:3 * sub + 2, :])
    o_ref[...] = xn.astype(BF16)


def _norm_call(xs, mod, norm_w, *, layer, sub, rows_per_mod, fixed_row):
    t, d = xs.shape
    tm = 1024
    row = _mod_row_map(rows_per_mod, tm, fixed_row)
    return pl.pallas_call(
        functools.partial(_norm_kernel, sub=sub),
        grid=(t // tm,),
        in_specs=[
            pl.BlockSpec((tm, d), lambda i: (i, 0)),
            pl.BlockSpec((None, None, N_ADA, d), lambda i: (layer, row(i), 0, 0)),
            pl.BlockSpec((None, 3, d), lambda i: (layer, 0, 0)),
        ],
        out_specs=pl.BlockSpec((tm, d), lambda i: (i, 0)),
        out_shape=jax.ShapeDtypeStruct((t, d), BF16),
        compiler_params=_params(("parallel",)),
        name="ffn_norm",
    )(xs, mod, norm_w)


def _ffn_up_kernel(xn_ref, wa_ref, wg_ref, h_ref, wab_ref, wgb_ref):
    @pl.when(pl.program_id(1) == 0)
    def _():
        wab_ref[...] = wa_ref[...].astype(BF16)
        wgb_ref[...] = wg_ref[...].astype(BF16)

    xn = xn_ref[...]
    a = _mm(xn, wab_ref[...])
    g = _mm(xn, wgb_ref[...])
    h_ref[...] = (_silu(g) * a).astype(BF16)


def _ffn_up_call(xn, wi, *, layer):
    t, d = xn.shape
    tm = min(2048, t)
    back = (FF_STEPS * FF_TILE - D_FF) // LANES
    hid_blk = lambda f: f * (FF_TILE // LANES) - (f // (FF_STEPS - 1)) * back
    hid0 = lambda f: hid_blk(f) * LANES
    gate0 = lambda f: (D_FF // LANES + hid_blk(f)) * LANES
    return pl.pallas_call(
        _ffn_up_kernel,
        grid=(FF_STEPS, t // tm),
        in_specs=[
            pl.BlockSpec((tm, d), lambda f, i: (i, 0)),
            pl.BlockSpec((None, pl.Element(d), pl.Element(FF_TILE)), lambda f, i: (layer, 0, hid0(f))),
            pl.BlockSpec((None, pl.Element(d), pl.Element(FF_TILE)), lambda f, i: (layer, 0, gate0(f))),
        ],
        out_specs=pl.BlockSpec((pl.Element(tm), pl.Element(FF_TILE)), lambda f, i: (i * tm, hid0(f))),
        out_shape=jax.ShapeDtypeStruct((t, D_FF), BF16),
        scratch_shapes=[pltpu.VMEM((d, FF_TILE), BF16), pltpu.VMEM((d, FF_TILE), BF16)],
        compiler_params=_params(("arbitrary", "arbitrary")),
        name="ffn_up",
    )(xn, wi, wi)


def _ffn_down_kernel(h_ref, w_ref, x_ref, mod_ref, o_ref, wb_ref, *, sub):
    @pl.when(pl.program_id(1) == 0)
    def _():
        wb_ref[...] = w_ref[...].astype(BF16)

    y = _mm(h_ref[...], wb_ref[...])
    o_ref[...] = x_ref[...] + 0.5 * mod_ref[3 * sub + 2:3 * sub + 3, :] * y


def _ffn_down_call(h, wo, xs, mod, *, layer, sub, rows_per_mod, fixed_row):
    t, d = xs.shape
    tm = 512
    tn = 512
    row = _mod_row_map(rows_per_mod, tm, fixed_row)
    return pl.pallas_call(
        functools.partial(_ffn_down_kernel, sub=sub),
        grid=(d // tn, t // tm),
        in_specs=[
            pl.BlockSpec((tm, D_FF), lambda n, i: (i, 0)),
            pl.BlockSpec((None, D_FF, tn), lambda n, i: (layer, 0, n)),
            pl.BlockSpec((tm, tn), lambda n, i: (i, n)),
            pl.BlockSpec((None, None, N_ADA, tn), lambda n, i: (layer, row(i), 0, n)),
        ],
        out_specs=pl.BlockSpec((tm, tn), lambda n, i: (i, n)),
        out_shape=jax.ShapeDtypeStruct((t, d), F32),
        scratch_shapes=[pltpu.VMEM((D_FF, tn), BF16)],
        compiler_params=_params(("arbitrary", "arbitrary")),
        name="ffn_down",
    )(h, wo, xs, mod)


def _ffn(xs, xn, mod, wi, wo, *, layer, sub, rows_per_mod, fixed_row):
    h = _ffn_up_call(xn, wi, layer=layer)
    return _ffn_down_call(h, wo, xs, mod, layer=layer, sub=sub, rows_per_mod=rows_per_mod, fixed_row=fixed_row)


def _final_norm_kernel(x_ref, w_ref, o_ref):
    x = x_ref[...]
    ms = jnp.mean(x * x, axis=-1, keepdims=True)
    o_ref[...] = x * lax.rsqrt(ms + EPS) * w_ref[...]


def _final_norm_call(xs, w):
    t, d = xs.shape
    tm = 1024
    return pl.pallas_call(
        _final_norm_kernel,
        grid=(t // tm,),
        in_specs=[pl.BlockSpec((tm, d), lambda i: (i, 0)), pl.BlockSpec((1, d), lambda i: (0, 0))],
        out_specs=pl.BlockSpec((tm, d), lambda i: (i, 0)),
        out_shape=jax.ShapeDtypeStruct((t, d), F32),
        compiler_params=_params(("parallel",)),
        name="final_norm",
    )(xs, w)


def _mixnorm_kernel(x_ref, mod_ref, nw_ref, wl_ref, xn_ref, gl_ref):
    xn = _modnorm(x_ref[...], nw_ref[1:2, :], mod_ref[3:4, :], mod_ref[4:5, :]).astype(BF16)
    xn_ref[...] = xn
    gl_ref[...] = _mm_nt(xn, wl_ref[...].astype(BF16)).astype(BF16)


def _mixnorm_call(xs, mod, norm_w, w_t, *, layer, rows_per_mod, fixed_row):
    t, d = xs.shape
    tm = 1024
    row = _mod_row_map(rows_per_mod, tm, fixed_row)
    return pl.pallas_call(
        _mixnorm_kernel,
        grid=(t // tm,),
        in_specs=[
            pl.BlockSpec((tm, d), lambda i: (i, 0)),
            pl.BlockSpec((None, None, N_ADA, d), lambda i: (layer, row(i), 0, 0)),
            pl.BlockSpec((None, 3, d), lambda i: (layer, 0, 0)),
            pl.BlockSpec((None, LANES, d), lambda i: (layer, W_GLOW // LANES, 0)),
        ],
        out_specs=[pl.BlockSpec((tm, d), lambda i: (i, 0)), pl.BlockSpec((tm, LANES), lambda i: (i, 0))],
        out_shape=[jax.ShapeDtypeStruct((t, d), BF16), jax.ShapeDtypeStruct((t, LANES), BF16)],
        compiler_params=_params(("parallel",)),
        name="mix_norm",
    )(xs, mod, norm_w, w_t)


P_TILES = P_TOTAL // P_TILE
P_SKIP = 2 * GLA_RANK
P_GATE_TILE = P_GATE // P_TILE


def _proj_row0(n):
    sublanes = 8
    first = W_GLOW // P_TILE
    after = (n + (P_TILES - first)) // P_TILES
    return (n * (P_TILE // sublanes) + after * (P_SKIP // sublanes)) * sublanes


def _proj_kernel(xn_ref, w_ref, o_ref, wb_ref):
    @pl.when(pl.program_id(1) == 0)
    def _():
        wb_ref[...] = w_ref[...].astype(BF16)

    @pl.when(pl.program_id(0) < P_GATE_TILE)
    def _():
        o_ref[...] = _mm_nt(xn_ref[...], wb_ref[...]).astype(BF16)

    @pl.when(pl.program_id(0) >= P_GATE_TILE)
    def _():
        o_ref[...] = _sigmoid(_mm_nt(xn_ref[...], wb_ref[...])).astype(BF16)


def _proj_call(xn, w_t, *, layer):
    t, d = xn.shape
    tm = min(2048, t)
    return pl.pallas_call(
        _proj_kernel,
        grid=(P_TILES, t // tm),
        in_specs=[
            pl.BlockSpec((tm, d), lambda n, i: (i, 0)),
            pl.BlockSpec((None, pl.Element(P_TILE), pl.Element(d)), lambda n, i: (layer, _proj_row0(n), 0)),
        ],
        out_specs=pl.BlockSpec((tm, P_TILE), lambda n, i: (i, n)),
        out_shape=jax.ShapeDtypeStruct((t, P_TOTAL), BF16),
        scratch_shapes=[pltpu.VMEM((P_TILE, d), BF16)],
        compiler_params=_params(("arbitrary", "arbitrary")),
        name="mix_proj",
    )(xn, w_t)


def _log_sigmoid(x):
    return jnp.minimum(x, 0.0) - jnp.log(1.0 + jnp.exp(-jnp.abs(x)))


def _chunk_scan(x, reverse):
    n, w = x.shape
    sub = 8
    tiles = GLA_CHUNK // sub
    x3 = x.reshape(n // sub, sub, w)
    pos = lax.broadcasted_iota(jnp.int32, x3.shape, 1)
    s = 1
    while s < sub:
        if reverse:
            x3 = x3 + jnp.where(pos < sub - s, pltpu.roll(x3, sub - s, axis=1), 0.0)
        else:
            x3 = x3 + jnp.where(pos >= s, pltpu.roll(x3, s, axis=1), 0.0)
        s *= 2
    tot = jnp.broadcast_to(x3[:, 0:1, :] if reverse else x3[:, sub - 1:sub, :], x3.shape)
    tpos = lax.broadcasted_iota(jnp.int32, x3.shape, 0) % tiles
    acc = tot
    s = 1
    while s < tiles:
        pad = jnp.zeros((s, sub, w), F32)
        if reverse:
            acc = acc + jnp.where(tpos < tiles - s, jnp.concatenate([acc[s:], pad], axis=0), 0.0)
        else:
            acc = acc + jnp.where(tpos >= s, jnp.concatenate([pad, acc[:-s]], axis=0), 0.0)
        s *= 2
    return (x3 + (acc - tot)).reshape(n, w)


def _gla_kernel(q_ref, k_ref, v_ref, r_ref, gl_ref, w2f_ref, w2b_ref, gbf_ref, gbb_ref, nw_ref,
                s0f_ref, s0b_ref, y_ref, sf_ref, sb_ref, qd_ref, kd_ref, ke_ref, dec_ref, of_ref, ob_ref,
                att_ref, sp_ref, *, seq):
    c = GLA_CHUNK
    n_chunks = seq // c
    glow = gl_ref[...]
    q = q_ref[...].astype(F32) * GLA_DK ** -0.5
    k = k_ref[...].astype(F32)
    for d, (w2_ref, gb_ref) in enumerate(((w2f_ref, gbf_ref), (w2b_ref, gbb_ref))):
        lg = _log_sigmoid(_mm(glow, w2_ref[...]) + gb_ref[...]) * (1.0 / GLA_TAU)
        gc = _chunk_scan(lg, reverse=d == 1)
        gc3 = gc.reshape(n_chunks, c, GLA_DK)
        gt = gc3[:, 0:1, :] if d == 1 else gc3[:, c - 1:c, :]
        qd_ref[d] = (q * jnp.exp(gc)).astype(BF16)
        kd_ref[d] = (k * jnp.exp(-gc)).astype(BF16)
        ke_ref[d] = (k.reshape(n_chunks, c, GLA_DK) * jnp.exp(gt - gc3)).reshape(seq, GLA_DK).astype(BF16)
        dec_ref[d] = jnp.exp(gt)
    sf_ref[...] = s0f_ref[...]
    sb_ref[...] = s0b_ref[...]

    ri = lax.broadcasted_iota(jnp.int32, (c, c), 0)
    ci = lax.broadcasted_iota(jnp.int32, (c, c), 1)

    unroll = min(8, n_chunks)
    dirs = ((0, ri >= ci, sf_ref, of_ref), (1, ri <= ci, sb_ref, ob_ref))

    def local_body(i, carry):
        for d, mask, st_ref, _ in dirs:
            n = i if d == 0 else n_chunks - 1 - i
            r0 = pl.multiple_of(n * c, c)
            scores = _mm_nt(qd_ref[d, pl.ds(r0, c), :], kd_ref[d, pl.ds(r0, c), :])
            att_ref[d, n] = jnp.where(mask, scores, 0.0).astype(BF16)
            st = st_ref[...]
            sp_ref[d, n] = st.astype(BF16)
            st_ref[...] = st * dec_ref[d, n] + _mm_tn(v_ref[pl.ds(r0, c), :], ke_ref[d, pl.ds(r0, c), :])
        return carry

    lax.fori_loop(0, n_chunks, local_body, 0, unroll=unroll)

    def out_body(n, carry):
        r0 = pl.multiple_of(n * c, c)
        v = v_ref[pl.ds(r0, c), :]
        for d, _, _, o_ref in dirs:
            o_ref[pl.ds(r0, c), :] = _mm(att_ref[d, n], v) + _mm_nt(qd_ref[d, pl.ds(r0, c), :], sp_ref[d, n])
        return carry

    lax.fori_loop(0, n_chunks, out_body, 0, unroll=unroll)

    o = of_ref[...] + ob_ref[...]
    ms = jnp.mean(o * o, axis=-1, keepdims=True)
    y = o * lax.rsqrt(ms + EPS) * nw_ref[...] * _silu(r_ref[...].astype(F32))
    y_ref[...] = y.astype(BF16)


def _gla_call(proj, glow, w2f, w2b, gbf, gbb, gla_nw, s0f, s0b, *, batch, seq):
    t = proj.shape[0]
    h = GLA_HEADS
    qb, kb = P_Q // GLA_DK, P_K // GLA_DK
    vb, rb = P_V // GLA_DV, P_R // GLA_DV
    st_spec = pl.BlockSpec((None, None, GLA_DV, GLA_DK), lambda b, hh: (b, hh, 0, 0))
    st_shape = jax.ShapeDtypeStruct((batch, h, GLA_DV, GLA_DK), F32)
    return pl.pallas_call(
        functools.partial(_gla_kernel, seq=seq),
        grid=(batch, h),
        in_specs=[
            pl.BlockSpec((seq, GLA_DK), lambda b, hh: (b, qb + hh)),
            pl.BlockSpec((seq, GLA_DK), lambda b, hh: (b, kb + hh)),
            pl.BlockSpec((seq, GLA_DV), lambda b, hh: (b, vb + hh)),
            pl.BlockSpec((seq, GLA_DV), lambda b, hh: (b, rb + hh)),
            pl.BlockSpec((seq, LANES), lambda b, hh: (b, 0)),
            pl.BlockSpec((None, LANES, GLA_DK), lambda b, hh: (hh, 0, 0)),
            pl.BlockSpec((None, LANES, GLA_DK), lambda b, hh: (hh, 0, 0)),
            pl.BlockSpec((None, 1, GLA_DK), lambda b, hh: (hh, 0, 0)),
            pl.BlockSpec((None, 1, GLA_DK), lambda b, hh: (hh, 0, 0)),
            pl.BlockSpec((None, 1, GLA_DV), lambda b, hh: (hh, 0, 0)),
            st_spec,
            st_spec,
        ],
        out_specs=[pl.BlockSpec((seq, GLA_DV), lambda b, hh: (b, hh)), st_spec, st_spec],
        out_shape=[jax.ShapeDtypeStruct((t, GLA_VT), BF16), st_shape, st_shape],
        scratch_shapes=[
            pltpu.VMEM((2, seq, GLA_DK), BF16),
            pltpu.VMEM((2, seq, GLA_DK), BF16),
            pltpu.VMEM((2, seq, GLA_DK), BF16),
            pltpu.VMEM((2, seq // GLA_CHUNK, 1, GLA_DK), F32),
            pltpu.VMEM((seq, GLA_DV), F32),
            pltpu.VMEM((seq, GLA_DV), F32),
            pltpu.VMEM((2, seq // GLA_CHUNK, GLA_CHUNK, GLA_CHUNK), BF16),
            pltpu.VMEM((2, seq // GLA_CHUNK, GLA_DV, GLA_DK), BF16),
        ],
        compiler_params=_params(("parallel", "parallel")),
        name="gla",
    )(proj, proj, proj, proj, glow, w2f, w2b, gbf, gbb, gla_nw, s0f, s0b)


def _fnet_kernel(u_ref, cl_ref, sl_ref, cg_ref, sg_ref, o_ref, *, scale):
    u = u_ref[...]
    p = _mm(cl_ref[...], u).astype(BF16)
    q = _mm(sl_ref[...], u).astype(BF16)
    y = _mm(p, cg_ref[...]) - _mm(q, sg_ref[...])
    o_ref[...] = (y * scale).astype(BF16)


def _fnet_call(proj, cl, sl, cg, sg, *, batch, seq):
    t = proj.shape[0]
    tr = min(512, seq)
    nr = seq // tr
    ub = P_FN // FNET_W
    return pl.pallas_call(
        functools.partial(_fnet_kernel, scale=1.0 / math.sqrt(seq * FNET_GW)),
        grid=(nr, batch),
        in_specs=[
            pl.BlockSpec((seq, FNET_W), lambda r, b: (b, ub)),
            pl.BlockSpec((tr, seq), lambda r, b: (r, 0)),
            pl.BlockSpec((tr, seq), lambda r, b: (r, 0)),
            pl.BlockSpec((FNET_W, FNET_W), lambda r, b: (0, 0)),
            pl.BlockSpec((FNET_W, FNET_W), lambda r, b: (0, 0)),
        ],
        out_specs=pl.BlockSpec((tr, FNET_W), lambda r, b: (b * nr + r, 0)),
        out_shape=jax.ShapeDtypeStruct((t, FNET_W), BF16),
        compiler_params=_params(("arbitrary", "arbitrary")),
        name="fnet",
    )(proj, cl, sl, cg, sg)


def _hy_posenc_kernel(c_ref, s_ref, *, seq):
    ti = lax.broadcasted_iota(jnp.int32, (seq, LANES), 0).astype(F32)
    lane = lax.broadcasted_iota(jnp.int32, (seq, LANES), 1)
    band_step = (HY_BANDS - 1 - 1e-4) / (HY_BANDS - 1)
    bands = jnp.where(lane < HY_BANDS, 1e-4 + lane.astype(F32) * band_step, 0.0)
    ang = 2.0 * math.pi * ti * bands / seq
    c_ref[...] = jnp.cos(ang)
    s_ref[...] = -jnp.sin(ang)


def _hy_posenc_call(seq):
    shape = jax.ShapeDtypeStruct((seq, LANES), F32)
    return pl.pallas_call(
        functools.partial(_hy_posenc_kernel, seq=seq),
        out_shape=[shape, shape],
        compiler_params=pltpu.CompilerParams(vmem_limit_bytes=VMEM_LIMIT_BYTES),
        name="hy_posenc",
    )()


def _hy_filter_kernel(zc_ref, zs_ref, w1t_ref, w1c_ref, w1s_ref, b1_ref, w2_ref, b2_ref, w3_ref, fr_ref, o_ref,
                      h2_ref, *, seq):
    t_col = lax.broadcasted_iota(jnp.int32, (seq, 1), 0).astype(F32) / (seq - 1.0)

    @pl.when(pl.program_id(0) == 0)
    def _():
        fr = fr_ref[...]
        pre1 = t_col * w1t_ref[...] + _mm3(zc_ref[...], w1c_ref[...]) + _mm3(zs_ref[...], w1s_ref[...]) + b1_ref[...]
        h1 = jnp.sin(fr * pre1)
        h2_ref[...] = jnp.sin(fr * (_mm3(h1, w2_ref[...]) + b2_ref[...]))

    h2 = h2_ref[...]
    ch = lax.broadcasted_iota(jnp.int32, (1, HY_W), 1).astype(F32)
    d0 = math.log(HY_TARGET) / HY_FAST
    d1 = math.log(HY_TARGET) / HY_SLOW
    deltas = jnp.abs(d0 + ch * ((d1 - d0) / (HY_W - 1)))
    win = jnp.exp(-t_col * deltas)
    ss = jnp.zeros((1, HY_W), F32)
    for lo in (0, HY_W):
        hd = _mm3(h2, w3_ref[:, lo:lo + HY_W]) * win
        o_ref[:, lo:lo + HY_W] = hd
        ss = ss + jnp.sum(hd * hd, axis=0, keepdims=True)
    inv = lax.rsqrt(ss + EPS)
    for lo in (0, HY_W):
        o_ref[:, lo:lo + HY_W] = o_ref[:, lo:lo + HY_W] * inv


def _hy_filter_call(zc, zs, w1t, w1c, w1s, b1, w2, b2, w3, fr, *, seq):
    hid = HY_HID
    full = lambda shape: pl.BlockSpec(shape, lambda o: (0,) * len(shape))
    return pl.pallas_call(
        functools.partial(_hy_filter_kernel, seq=seq),
        grid=(2,),
        in_specs=[
            full((seq, LANES)), full((seq, LANES)),
            full((1, hid)), full((LANES, hid)), full((LANES, hid)), full((1, hid)),
            full((hid, hid)), full((1, hid)),
            pl.BlockSpec((hid, 2 * HY_W), lambda o: (0, o)),
            full((1, hid)),
        ],
        out_specs=pl.BlockSpec((seq, 2 * HY_W), lambda o: (0, o)),
        out_shape=jax.ShapeDtypeStruct((seq, 4 * HY_W), F32),
        scratch_shapes=[pltpu.VMEM((seq, hid), F32)],
        compiler_params=_params(("arbitrary",)),
        name="hy_filter",
    )(zc, zs, w1t, w1c, w1s, b1, w2, b2, w3, fr)


def _hy_spec_kernel(h_ref, c_ref, s_ref, rc_ref, rs_ref, o_ref, hb_ref, *, seq, tf):
    f = pl.program_id(1)

    @pl.when(f == 0)
    def _():
        hb_ref[...] = h_ref[...].astype(BF16)

    cm = c_ref[...]
    sm = s_ref[...]
    hfw = hb_ref[:, :HY_W]
    hbw = hb_ref[:, HY_W:]
    pf = _mm(cm, hfw)
    pb = _mm(cm, hbw)
    qf = _mm(sm, hfw)
    qb = _mm(sm, hbw)
    reps = HY_W // LANES
    cf = jnp.concatenate([rc_ref[...]] * reps, axis=1)
    sf = jnp.concatenate([rs_ref[...]] * reps, axis=1)
    kre = pf + cf * pb - sf * qb
    kim = cf * qb + sf * pb - qf
    row0 = (lax.broadcasted_iota(jnp.int32, (tf, HY_W), 0) + f * tf) == 0
    wgt = jnp.where(row0, 0.5 / seq, 1.0 / seq)
    k1 = kre * wgt
    o_ref[0] = k1
    o_ref[1] = jnp.where(row0, 0.0, kim * wgt)
    o_ref[2] = jnp.where(row0, (qf - qb) * (0.5 / seq), k1)


def _hy_spec_call(h, cmat, smat, rot_c, rot_s, *, seq):
    tf = min(512, seq)
    nf = seq // tf
    return pl.pallas_call(
        functools.partial(_hy_spec_kernel, seq=seq, tf=tf),
        grid=(2, nf),
        in_specs=[
            pl.BlockSpec((seq, 2 * HY_W), lambda o, f: (0, o)),
            pl.BlockSpec((tf, seq), lambda o, f: (f, 0)),
            pl.BlockSpec((tf, seq), lambda o, f: (f, 0)),
            pl.BlockSpec((tf, LANES), lambda o, f: (f, 0)),
            pl.BlockSpec((tf, LANES), lambda o, f: (f, 0)),
        ],
        out_specs=pl.BlockSpec((None, 3, tf, HY_W), lambda o, f: (o, 0, f, 0)),
        out_shape=jax.ShapeDtypeStruct((2, 3, seq, HY_W), F32),
        scratch_shapes=[pltpu.VMEM((seq, 2 * HY_W), BF16)],
        compiler_params=_params(("arbitrary", "arbitrary")),
        name="hy_spectrum",
    )(h, cmat, smat, rot_c, rot_s)


def _short_conv(u, w, b, seg):
    n = u.shape[0]
    pos = lax.broadcasted_iota(jnp.int32, u.shape, 0) % seg
    prev = jnp.where(pos == 0, 0.0, pltpu.roll(u, 1, axis=0))
    nxt = jnp.where(pos == seg - 1, 0.0, pltpu.roll(u, n - 1, axis=0))
    return prev * w[0:1, :] + u * w[1:2, :] + nxt * w[2:3, :] + b


def _hy_conv_kernel(u_ref, m_ref, wu_ref, bu_ref, wm_ref, bm_ref, bias_ref, k_ref, cr_ref, sr_ref, cc_ref, sc_ref,
                    o_ref, ub_ref, acc_ref, *, conv_u, seg, nf, batch, seq):
    f = pl.program_id(1)
    tc = o_ref.shape[1]

    def load_u(b):
        u = u_ref[b * seq:(b + 1) * seq, :].astype(F32)
        if conv_u:
            u = _short_conv(u, wu_ref[...], bu_ref[...], seg)
        return u

    @pl.when(f == 0)
    def _():
        for b in range(batch):
            ub_ref[:, b * tc:(b + 1) * tc] = load_u(b).astype(BF16)
        acc_ref[...] = jnp.zeros(acc_ref.shape, F32)

    ub = ub_ref[...]
    p = _mm(cr_ref[...], ub)
    q = _mm(sr_ref[...], ub)
    k1, k2, k3 = (jnp.concatenate([k_ref[j]] * batch, axis=1) for j in range(3))
    av = (p * k1 + q * k2).astype(BF16)
    bv = (q * k3 - p * k2).astype(BF16)
    acc_ref[...] += _mm(cc_ref[...], av) + _mm(sc_ref[...], bv)

    @pl.when(f == nf - 1)
    def _():
        for b in range(batch):
            m = _short_conv(m_ref[b * seq:(b + 1) * seq, :].astype(F32), wm_ref[...], bm_ref[...], seg)
            y = acc_ref[:, b * tc:(b + 1) * tc] + bias_ref[...] * ub_ref[:, b * tc:(b + 1) * tc].astype(F32)
            o_ref[b * seq:(b + 1) * seq, :] = (m * y).astype(o_ref.dtype)


def _hy_conv_call(u_arr, u_col, m_arr, m_col, conv_w, conv_b, u_sect, m_sect, bias, kspec, cmat, smat, smat_inv,
                  *, batch, seq, seg, conv_u):
    t = u_arr.shape[0]
    tc = 256
    nch = HY_W // tc
    tf = min(256, seq)
    nf = seq // tf
    ucb, mcb = u_col // tc, m_col // tc
    usb, msb = u_sect * nch, m_sect * nch
    once = pl.Buffered(1)
    return pl.pallas_call(
        functools.partial(_hy_conv_kernel, conv_u=conv_u, seg=seg, nf=nf, batch=batch, seq=seq),
        grid=(nch, nf),
        in_specs=[
            pl.BlockSpec((t, tc), lambda ch, f: (0, ucb + ch), pipeline_mode=once),
            pl.BlockSpec((t, tc), lambda ch, f: (0, mcb + ch), pipeline_mode=once),
            pl.BlockSpec((3, tc), lambda ch, f: (0, usb + ch)),
            pl.BlockSpec((1, tc), lambda ch, f: (0, usb + ch)),
            pl.BlockSpec((3, tc), lambda ch, f: (0, msb + ch)),
            pl.BlockSpec((1, tc), lambda ch, f: (0, msb + ch)),
            pl.BlockSpec((1, tc), lambda ch, f: (0, ch)),
            pl.BlockSpec((3, tf, tc), lambda ch, f: (0, f, ch)),
            pl.BlockSpec((tf, seq), lambda ch, f: (f, 0)),
            pl.BlockSpec((tf, seq), lambda ch, f: (f, 0)),
            pl.BlockSpec((seq, tf), lambda ch, f: (0, f)),
            pl.BlockSpec((seq, tf), lambda ch, f: (0, f)),
        ],
        out_specs=pl.BlockSpec((t, tc), lambda ch, f: (0, ch)),
        out_shape=jax.ShapeDtypeStruct((t, HY_W), BF16),
        scratch_shapes=[pltpu.VMEM((seq, batch * tc), BF16), pltpu.VMEM((seq, batch * tc), F32)],
        compiler_params=_params(("arbitrary", "arbitrary")),
        name="hy_conv",
    )(u_arr, m_arr, conv_w, conv_b, conv_w, conv_b, bias, kspec, cmat, smat, cmat, smat_inv)


MERGE_CHUNK = 512


def _merge_kernel(x_ref, mod_ref, nw_ref, yg_ref, yf_ref, yh_ref, g0_ref, g1_ref, g2_ref, wg_ref, wf_ref, wh_ref,
                  wo_ref, o_ref, xn_ref, mg_ref):
    d = o_ref.shape[1]
    yg = yg_ref[...]
    yf = yf_ref[...]
    yh = yh_ref[...]
    for c0 in range(0, d, MERGE_CHUNK):
        cols = slice(c0, c0 + MERGE_CHUNK)
        m = g0_ref[:, cols].astype(F32) * _mm(yg, wg_ref[:, cols])
        m = m + g1_ref[:, cols].astype(F32) * _mm(yf, wf_ref[:, cols])
        m = m + g2_ref[:, cols].astype(F32) * _mm(yh, wh_ref[:, cols])
        mg_ref[:, cols] = m.astype(BF16)
    mg = mg_ref[...]
    for c0 in range(0, d, MERGE_CHUNK):
        cols = slice(c0, c0 + MERGE_CHUNK)
        o_ref[:, cols] = x_ref[:, cols] + mod_ref[5:6, cols] * _mm(mg, wo_ref[:, cols])
    xn_ref[...] = _modnorm(o_ref[...], nw_ref[2:3, :], mod_ref[6:7, :], mod_ref[7:8, :]).astype(BF16)


def _merge_call(xs, mod, norm_w, proj, y_gla, y_fnet, y_hy, wbg, wbf, wbh, wo, *, layer, rows_per_mod, fixed_row):
    t, d = xs.shape
    tm = 512
    row = _mod_row_map(rows_per_mod, tm, fixed_row)
    gate = lambda j: pl.BlockSpec((pl.Element(tm), pl.Element(d)), lambda i: (i * tm, P_GATE + j * d))
    weight = lambda k: pl.BlockSpec((None, k, d), lambda i: (layer, 0, 0), pipeline_mode=pl.Buffered(1))
    return pl.pallas_call(
        _merge_kernel,
        grid=(t // tm,),
        in_specs=[
            pl.BlockSpec((tm, d), lambda i: (i, 0)),
            pl.BlockSpec((None, None, N_ADA, d), lambda i: (layer, row(i), 0, 0)),
            pl.BlockSpec((None, 3, d), lambda i: (layer, 0, 0)),
            pl.BlockSpec((tm, GLA_VT), lambda i: (i, 0)),
            pl.BlockSpec((tm, FNET_W), lambda i: (i, 0)),
            pl.BlockSpec((tm, HY_W), lambda i: (i, 0)),
            gate(0),
            gate(1),
            gate(2),
            weight(GLA_VT),
            weight(FNET_W),
            weight(HY_W),
            weight(d),
        ],
        out_specs=[pl.BlockSpec((tm, d), lambda i: (i, 0)), pl.BlockSpec((tm, d), lambda i: (i, 0))],
        out_shape=[jax.ShapeDtypeStruct((t, d), F32), jax.ShapeDtypeStruct((t, d), BF16)],
        scratch_shapes=[pltpu.VMEM((tm, d), BF16)],
        compiler_params=_params(("parallel",)),
        name="merge_out",
    )(xs, mod, norm_w, y_gla, y_fnet, y_hy, proj, proj, proj, wbg, wbf, wbh, wo)


def _trig_tables(n, period):
    split = 32
    c = lax.broadcasted_iota(jnp.int32, (1, n), 1)

    def rows(r):
        ang = ((r * c) % period).astype(F32) * (2.0 * math.pi / period)
        return jnp.cos(ang), jnp.sin(ang)

    c_lo, s_lo = rows(lax.broadcasted_iota(jnp.int32, (split, 1), 0))
    c_hi, s_hi = rows(lax.broadcasted_iota(jnp.int32, (n // split, 1), 0) * split)
    cos = c_hi[:, None, :] * c_lo[None] - s_hi[:, None, :] * s_lo[None]
    sin = s_hi[:, None, :] * c_lo[None] + c_hi[:, None, :] * s_lo[None]
    return cos.reshape(n, n), sin.reshape(n, n)


def _fnet_tables(seq):
    cl, sl = _trig_tables(seq, seq)
    cg, sg = _trig_tables(FNET_GW, FNET_GW)
    eye = jnp.eye(FNET_GROUPS, dtype=F32)
    return cl.astype(BF16), sl.astype(BF16), jnp.kron(eye, cg).astype(BF16), jnp.kron(eye, sg).astype(BF16)


def _hyena_tables(seq):
    cm, sm = _trig_tables(seq, 2 * seq)
    r = lax.broadcasted_iota(jnp.int32, (seq, seq), 0)
    c = lax.broadcasted_iota(jnp.int32, (seq, seq), 1)
    sm_fwd = jnp.where(r == 0, (1 - 2 * (c % 2)).astype(F32), sm)
    sm_inv = jnp.where(c == 0, (1 - 2 * (r % 2)).astype(F32), sm)
    fr = lax.broadcasted_iota(jnp.int32, (seq, LANES), 0).astype(F32) * (math.pi / seq)
    return cm.astype(BF16), sm_fwd.astype(BF16), sm_inv.astype(BF16), jnp.cos(fr), jnp.sin(fr)


def _gla_gate_weights(gla_w2, gla_gb):
    depth = gla_w2.shape[0]
    w = gla_w2.reshape(depth, 2, GLA_RANK, GLA_HEADS, GLA_DK).transpose(0, 1, 3, 2, 4)
    zf = jnp.zeros((depth, GLA_HEADS, LANES, GLA_DK), F32)
    w2f = zf.at[:, :, :GLA_RANK].set(w[:, 0]).astype(BF16)
    w2b = zf.at[:, :, GLA_RANK:2 * GLA_RANK].set(w[:, 1]).astype(BF16)
    gb = gla_gb.reshape(depth, 2, GLA_HEADS, 1, GLA_DK)
    return w2f, w2b, gb[:, 0], gb[:, 1]


def kernel(x, c, ctx, c_ctx, ada_w, ada_b, norm_w, ffn1_wi, ffn1_wo, ffn2_wi, ffn2_wo, w_in, gla_w2, gla_gb,
           gla_norm_w, hy_conv_w, hy_conv_b, hy_f1_w, hy_f1_b, hy_f2_w, hy_f2_b, hy_f3_w, hy_freq, hy_bias,
           w_br_gla, w_br_fnet, w_br_hy, w_o, final_norm_w):
    batch, seq, d = x.shape
    ctx_len = ctx.shape[1]
    xs = x.reshape(batch * seq, d)
    cs = ctx.reshape(batch * ctx_len, d)

    cond = jnp.concatenate([c, c_ctx[None, :], jnp.zeros((8 - batch - 1, d), F32)], axis=0)
    mod = _ada_call(cond, ada_w, ada_b)
    ctx_row = batch

    w_t = jnp.swapaxes(w_in, 1, 2)
    wbg, wbf, wbh, wob = (w.astype(BF16) for w in (w_br_gla, w_br_fnet, w_br_hy, w_o))
    w2f, w2b, gbf, gbb = _gla_gate_weights(gla_w2, gla_gb)
    gla_nw = gla_norm_w.reshape(DEPTH, GLA_HEADS, 1, GLA_DV)

    fnet_tab = {n: _fnet_tables(n) for n in (seq, ctx_len)}
    hy_tab = {n: _hyena_tables(n) for n in (seq, ctx_len)}
    hy_pos = {n: _hy_posenc_call(n) for n in (seq, ctx_len)}

    def mixers(proj, glow, layer, n, seg, s0f, s0b, need_y):
        y_gla, sf, sb = _gla_call(proj, glow, w2f[layer], w2b[layer], gbf[layer], gbb[layer], gla_nw[layer],
                                  s0f, s0b, batch=batch, seq=n)
        if not need_y:
            return None, sf, sb
        y_fnet = _fnet_call(proj, *fnet_tab[n], batch=batch, seq=n)
        cm, sm, sm_inv, rot_c, rot_s = hy_tab[n]
        w1 = hy_f1_w[layer]
        zpad = jnp.zeros((LANES - HY_BANDS, HY_HID), F32)
        h = _hy_filter_call(*hy_pos[n], w1[0:1], jnp.concatenate([w1[1:1 + HY_BANDS], zpad], 0),
                            jnp.concatenate([w1[1 + HY_BANDS:], zpad], 0), hy_f1_b[layer][None, :],
                            hy_f2_w[layer], hy_f2_b[layer][None, :], hy_f3_w[layer], hy_freq[layer][None, :], seq=n)
        kspec = _hy_spec_call(h, cm, sm, rot_c, rot_s, seq=n)
        cw, cb = hy_conv_w[layer], hy_conv_b[layer][None, :]
        z = _hy_conv_call(proj, P_HY, proj, P_HY + HY_W, cw, cb, 0, 1, hy_bias[layer, 0][None, :], kspec[0], cm, sm,
                          sm_inv, batch=batch, seq=n, seg=seg, conv_u=True)
        y_hy = _hy_conv_call(z, 0, proj, P_HY + 2 * HY_W, cw, cb, 0, 2, hy_bias[layer, 1][None, :], kspec[1], cm, sm,
                             sm_inv, batch=batch, seq=n, seg=seg, conv_u=False)
        return (y_gla, y_fnet, y_hy), sf, sb

    s_zero = jnp.zeros((batch, GLA_HEADS, GLA_DV, GLA_DK), F32)
    for layer in range(DEPTH):
        last = layer == DEPTH - 1
        x_kw = dict(layer=layer, rows_per_mod=seq, fixed_row=None)
        c_kw = dict(layer=layer, rows_per_mod=ctx_len, fixed_row=ctx_row)
        xs = _ffn(xs, _norm_call(xs, mod, norm_w, sub=0, **x_kw), mod, ffn1_wi, ffn1_wo, sub=0, **x_kw)
        cs = _ffn(cs, _norm_call(cs, mod, norm_w, sub=0, **c_kw), mod, ffn1_wi, ffn1_wo, sub=0, **c_kw)
        cn, glow_c = _mixnorm_call(cs, mod, norm_w, w_t, **c_kw)
        proj_c = _proj_call(cn, w_t, layer=layer)
        ys_c, sf, sb = mixers(proj_c, glow_c, layer, ctx_len, ctx_len, s_zero, s_zero, not last)
        if not last:
            cs, cn2 = _merge_call(cs, mod, norm_w, proj_c, *ys_c, wbg, wbf, wbh, wob, **c_kw)
            cs = _ffn(cs, cn2, mod, ffn2_wi, ffn2_wo, sub=2, **c_kw)
        xn, glow_x = _mixnorm_call(xs, mod, norm_w, w_t, **x_kw)
        proj_x = _proj_call(xn, w_t, layer=layer)
        ys_x, _, _ = mixers(proj_x, glow_x, layer, seq, GRID_W, sf, sb, True)
        xs, xn2 = _merge_call(xs, mod, norm_w, proj_x, *ys_x, wbg, wbf, wbh, wob, **x_kw)
        xs = _ffn(xs, xn2, mod, ffn2_wi, ffn2_wo, sub=2, **x_kw)
    return _final_norm_call(xs, final_norm_w.reshape(1, d)).reshape(batch, seq, d)
```

```python
import functools
import math

import jax
import jax.numpy as jnp
from jax import lax
from jax.experimental import pallas as pl
from jax.experimental.pallas import tpu as pltpu

F32 = jnp.float32
BF16 = jnp.bfloat16

D_MODEL = 2048
DEPTH = 4
GRID_W = 64
N_ADA = 9
D_FF = 5504
GLA_HEADS = 4
GLA_DK = 128
GLA_DV = 256
GLA_KT = GLA_HEADS * GLA_DK
GLA_VT = GLA_HEADS * GLA_DV
GLA_RANK = 16
GLA_TAU = 16.0
GLA_CHUNK = 64
FNET_GROUPS = 4
FNET_GW = 128
FNET_W = FNET_GROUPS * FNET_GW
HY_W = 512
HY_BANDS = 16
HY_HID = 64
HY_FAST = 0.3
HY_SLOW = 1.5
HY_TARGET = 1e-2
EPS = 1e-6

LANES = 128
VMEM_LIMIT_BYTES = 56 * 1024 * 1024

FF_TILE = 512
FF_STEPS = -(-D_FF // FF_TILE)

W_GLOW = 2 * GLA_KT + GLA_VT
W_IN_COLS = W_GLOW + 2 * GLA_RANK + GLA_VT + FNET_W + 3 * HY_W + 3 * D_MODEL
P_TILE = 1024
P_Q = 0
P_K = GLA_KT
P_V = 2 * GLA_KT
P_R = W_GLOW
P_FN = P_R + GLA_VT
P_HY = P_FN + FNET_W
P_GATE = P_HY + 3 * HY_W
P_TOTAL = P_GATE + 3 * D_MODEL


def _mm(a, b):
    return jnp.dot(a, b, preferred_element_type=F32)


def _mm_nt(a, b):
    return lax.dot_general(a, b, (((1,), (1,)), ((), ())), preferred_element_type=F32)


def _mm_tn(a, b):
    return lax.dot_general(a, b, (((0,), (0,)), ((), ())), preferred_element_type=F32)


def _split2(a):
    hi = a.astype(BF16)
    lo = (a - hi.astype(F32)).astype(BF16)
    return hi, lo


def _mm3(a, b):
    ah, al = _split2(a)
    bh, bl = _split2(b)
    return _mm(ah, bh) + (_mm(ah, bl) + _mm(al, bh))


def _sigmoid(x):
    return 0.5 * jnp.tanh(0.5 * x) + 0.5


def _silu(x):
    return x * _sigmoid(x)


def _params(sem):
    return pltpu.CompilerParams(dimension_semantics=sem, vmem_limit_bytes=VMEM_LIMIT_BYTES)


def _modnorm(x, nw, shift, scale):
    ms = jnp.mean(x * x, axis=-1, keepdims=True)
    y = x * lax.rsqrt(ms + EPS) * nw
    return y * (1.0 + scale) + shift


def _ada_kernel(c_ref, w_ref, b_ref, o_ref):
    a = _silu(c_ref[...]).astype(BF16)
    o_ref[...] = _mm(a, w_ref[...].astype(BF16)) + b_ref[...]


def _ada_call(cond, ada_w, ada_b):
    depth, d, n = ada_w.shape
    rows = cond.shape[0]
    tn = 1024
    out = pl.pallas_call(
        _ada_kernel,
        grid=(depth, n // tn),
        in_specs=[
            pl.BlockSpec((rows, d), lambda l, j: (0, 0)),
            pl.BlockSpec((None, d, tn), lambda l, j: (l, 0, j)),
            pl.BlockSpec((None, 1, tn), lambda l, j: (l, 0, j)),
        ],
        out_specs=pl.BlockSpec((None, rows, tn), lambda l, j: (l, 0, j)),
        out_shape=jax.ShapeDtypeStruct((depth, rows, n), F32),
        compiler_params=_params(("arbitrary", "arbitrary")),
        name="ada_mod",
    )(cond, ada_w, ada_b.reshape(depth, 1, n))
    return out.reshape(depth, rows, N_ADA, d)


def _mod_row_map(rows_per_mod, tm, fixed_row):
    if fixed_row is not None:
        return lambda i: fixed_row
    per = rows_per_mod // tm
    return lambda i: i // per


def _norm_kernel(x_ref, mod_ref, nw_ref, o_ref, *, sub):
    xn = _modnorm(x_ref[...], nw_ref[sub:sub + 1, :], mod_ref[3 * sub:3 * sub + 1, :],
                  mod_ref[3 * sub + 1:3 * sub + 2, :])
    o_ref[...] = xn.astype(BF16)


def _norm_call(xs, mod, norm_w, *, layer, sub, rows_per_mod, fixed_row):
    t, d = xs.shape
    tm = 1024
    row = _mod_row_map(rows_per_mod, tm, fixed_row)
    return pl.pallas_call(
        functools.partial(_norm_kernel, sub=sub),
        grid=(t // tm,),
        in_specs=[
            pl.BlockSpec((tm, d), lambda i: (i, 0)),
            pl.BlockSpec((None, None, N_ADA, d), lambda i: (layer, row(i), 0, 0)),
            pl.BlockSpec((None, 3, d), lambda i: (layer, 0, 0)),
        ],
        out_specs=pl.BlockSpec((tm, d), lambda i: (i, 0)),
        out_shape=jax.ShapeDtypeStruct((t, d), BF16),
        compiler_params=_params(("parallel",)),
        name="ffn_norm",
    )(xs, mod, norm_w)


def _ffn_up_kernel(xn_ref, wa_ref, wg_ref, h_ref, wab_ref, wgb_ref):
    @pl.when(pl.program_id(1) == 0)
    def _():
        wab_ref[...] = wa_ref[...].astype(BF16)
        wgb_ref[...] = wg_ref[...].astype(BF16)

    xn = xn_ref[...]
    a = _mm(xn, wab_ref[...])
    g = _mm(xn, wgb_ref[...])
    h_ref[...] = (_silu(g) * a).astype(BF16)


def _ffn_up_call(xn, wi, *, layer):
    t, d = xn.shape
    tm = min(2048, t)
    back = (FF_STEPS * FF_TILE - D_FF) // LANES
    hid_blk = lambda f: f * (FF_TILE // LANES) - (f // (FF_STEPS - 1)) * back
    hid0 = lambda f: hid_blk(f) * LANES
    gate0 = lambda f: (D_FF // LANES + hid_blk(f)) * LANES
    return pl.pallas_call(
        _ffn_up_kernel,
        grid=(FF_STEPS, t // tm),
        in_specs=[
            pl.BlockSpec((tm, d), lambda f, i: (i, 0)),
            pl.BlockSpec((None, pl.Element(d), pl.Element(FF_TILE)), lambda f, i: (layer, 0, hid0(f))),
            pl.BlockSpec((None, pl.Element(d), pl.Element(FF_TILE)), lambda f, i: (layer, 0, gate0(f))),
        ],
        out_specs=pl.BlockSpec((pl.Element(tm), pl.Element(FF_TILE)), lambda f, i: (i * tm, hid0(f))),
        out_shape=jax.ShapeDtypeStruct((t, D_FF), BF16),
        scratch_shapes=[pltpu.VMEM((d, FF_TILE), BF16), pltpu.VMEM((d, FF_TILE), BF16)],
        compiler_params=_params(("arbitrary", "arbitrary")),
        name="ffn_up",
    )(xn, wi, wi)


def _ffn_down_kernel(h_ref, w_ref, x_ref, mod_ref, o_ref, wb_ref, *, sub):
    @pl.when(pl.program_id(1) == 0)
    def _():
        wb_ref[...] = w_ref[...].astype(BF16)

    y = _mm(h_ref[...], wb_ref[...])
    o_ref[...] = x_ref[...] + 0.5 * mod_ref[3 * sub + 2:3 * sub + 3, :] * y


def _ffn_down_call(h, wo, xs, mod, *, layer, sub, rows_per_mod, fixed_row):
    t, d = xs.shape
    tm = 512
    tn = 512
    row = _mod_row_map(rows_per_mod, tm, fixed_row)
    return pl.pallas_call(
        functools.partial(_ffn_down_kernel, sub=sub),
        grid=(d // tn, t // tm),
        in_specs=[
            pl.BlockSpec((tm, D_FF), lambda n, i: (i, 0)),
            pl.BlockSpec((None, D_FF, tn), lambda n, i: (layer, 0, n)),
            pl.BlockSpec((tm, tn), lambda n, i: (i, n)),
            pl.BlockSpec((None, None, N_ADA, tn), lambda n, i: (layer, row(i), 0, n)),
        ],
        out_specs=pl.BlockSpec((tm, tn), lambda n, i: (i, n)),
        out_shape=jax.ShapeDtypeStruct((t, d), F32),
        scratch_shapes=[pltpu.VMEM((D_FF, tn), BF16)],
        compiler_params=_params(("arbitrary", "arbitrary")),
        name="ffn_down",
    )(h, wo, xs, mod)


def _ffn(xs, xn, mod, wi, wo, *, layer, sub, rows_per_mod, fixed_row):
    h = _ffn_up_call(xn, wi, layer=layer)
    return _ffn_down_call(h, wo, xs, mod, layer=layer, sub=sub, rows_per_mod=rows_per_mod, fixed_row=fixed_row)


def _final_norm_kernel(x_ref, w_ref, o_ref):
    x = x_ref[...]
    ms = jnp.mean(x * x, axis=-1, keepdims=True)
    o_ref[...] = x * lax.rsqrt(ms + EPS) * w_ref[...]


def _final_norm_call(xs, w):
    t, d = xs.shape
    tm = 1024
    return pl.pallas_call(
        _final_norm_kernel,
        grid=(t // tm,),
        in_specs=[pl.BlockSpec((tm, d), lambda i: (i, 0)), pl.BlockSpec((1, d), lambda i: (0, 0))],
        out_specs=pl.BlockSpec((tm, d), lambda i: (i, 0)),
        out_shape=jax.ShapeDtypeStruct((t, d), F32),
        compiler_params=_params(("parallel",)),
        name="final_norm",
    )(xs, w)


def _mixnorm_kernel(x_ref, mod_ref, nw_ref, wl_ref, xn_ref, gl_ref):
    xn = _modnorm(x_ref[...], nw_ref[1:2, :], mod_ref[3:4, :], mod_ref[4:5, :]).astype(BF16)
    xn_ref[...] = xn
    gl_ref[...] = _mm_nt(xn, wl_ref[...].astype(BF16)).astype(BF16)


def _mixnorm_call(xs, mod, norm_w, w_t, *, layer, rows_per_mod, fixed_row):
    t, d = xs.shape
    tm = 1024
    row = _mod_row_map(rows_per_mod, tm, fixed_row)
    return pl.pallas_call(
        _mixnorm_kernel,
        grid=(t // tm,),
        in_specs=[
            pl.BlockSpec((tm, d), lambda i: (i, 0)),
            pl.BlockSpec((None, None, N_ADA, d), lambda i: (layer, row(i), 0, 0)),
            pl.BlockSpec((None, 3, d), lambda i: (layer, 0, 0)),
            pl.BlockSpec((None, LANES, d), lambda i: (layer, W_GLOW // LANES, 0)),
        ],
        out_specs=[pl.BlockSpec((tm, d), lambda i: (i, 0)), pl.BlockSpec((tm, LANES), lambda i: (i, 0))],
        out_shape=[jax.ShapeDtypeStruct((t, d), BF16), jax.ShapeDtypeStruct((t, LANES), BF16)],
        compiler_params=_params(("parallel",)),
        name="mix_norm",
    )(xs, mod, norm_w, w_t)


P_TILES = P_TOTAL // P_TILE
P_SKIP = 2 * GLA_RANK
P_GATE_TILE = P_GATE // P_TILE


def _proj_row0(n):
    sublanes = 8
    first = W_GLOW // P_TILE
    after = (n + (P_TILES - first)) // P_TILES
    return (n * (P_TILE // sublanes) + after * (P_SKIP // sublanes)) * sublanes


def _proj_kernel(xn_ref, w_ref, o_ref, wb_ref):
    @pl.when(pl.program_id(1) == 0)
    def _():
        wb_ref[...] = w_ref[...].astype(BF16)

    @pl.when(pl.program_id(0) < P_GATE_TILE)
    def _():
        o_ref[...] = _mm_nt(xn_ref[...], wb_ref[...]).astype(BF16)

    @pl.when(pl.program_id(0) >= P_GATE_TILE)
    def _():
        o_ref[...] = _sigmoid(_mm_nt(xn_ref[...], wb_ref[...])).astype(BF16)


def _proj_call(xn, w_t, *, layer):
    t, d = xn.shape
    tm = min(2048, t)
    return pl.pallas_call(
        _proj_kernel,
        grid=(P_TILES, t // tm),
        in_specs=[
            pl.BlockSpec((tm, d), lambda n, i: (i, 0)),
            pl.BlockSpec((None, pl.Element(P_TILE), pl.Element(d)), lambda n, i: (layer, _proj_row0(n), 0)),
        ],
        out_specs=pl.BlockSpec((tm, P_TILE), lambda n, i: (i, n)),
        out_shape=jax.ShapeDtypeStruct((t, P_TOTAL), BF16),
        scratch_shapes=[pltpu.VMEM((P_TILE, d), BF16)],
        compiler_params=_params(("arbitrary", "arbitrary")),
        name="mix_proj",
    )(xn, w_t)


def _log_sigmoid(x):
    return jnp.minimum(x, 0.0) - jnp.log(1.0 + jnp.exp(-jnp.abs(x)))


def _chunk_scan(x, reverse):
    n, w = x.shape
    sub = 8
    tiles = GLA_CHUNK // sub
    x3 = x.reshape(n // sub, sub, w)
    pos = lax.broadcasted_iota(jnp.int32, x3.shape, 1)
    s = 1
    while s < sub:
        if reverse:
            x3 = x3 + jnp.where(pos < sub - s, pltpu.roll(x3, sub - s, axis=1), 0.0)
        else:
            x3 = x3 + jnp.where(pos >= s, pltpu.roll(x3, s, axis=1), 0.0)
        s *= 2
    tot = jnp.broadcast_to(x3[:, 0:1, :] if reverse else x3[:, sub - 1:sub, :], x3.shape)
    tpos = lax.broadcasted_iota(jnp.int32, x3.shape, 0) % tiles
    acc = tot
    s = 1
    while s < tiles:
        pad = jnp.zeros((s, sub, w), F32)
        if reverse:
            acc = acc + jnp.where(tpos < tiles - s, jnp.concatenate([acc[s:], pad], axis=0), 0.0)
        else:
            acc = acc + jnp.where(tpos >= s, jnp.concatenate([pad, acc[:-s]], axis=0), 0.0)
        s *= 2
    return (x3 + (acc - tot)).reshape(n, w)


def _gla_kernel(q_ref, k_ref, v_ref, r_ref, gl_ref, w2f_ref, w2b_ref, gbf_ref, gbb_ref, nw_ref,
                s0f_ref, s0b_ref, y_ref, sf_ref, sb_ref, qd_ref, kd_ref, ke_ref, dec_ref, of_ref, ob_ref,
                att_ref, sp_ref, *, seq):
    c = GLA_CHUNK
    n_chunks = seq // c
    glow = gl_ref[...]
    q = q_ref[...].astype(F32) * GLA_DK ** -0.5
    k = k_ref[...].astype(F32)
    for d, (w2_ref, gb_ref) in enumerate(((w2f_ref, gbf_ref), (w2b_ref, gbb_ref))):
        lg = _log_sigmoid(_mm(glow, w2_ref[...]) + gb_ref[...]) * (1.0 / GLA_TAU)
        gc = _chunk_scan(lg, reverse=d == 1)
        gc3 = gc.reshape(n_chunks, c, GLA_DK)
        gt = gc3[:, 0:1, :] if d == 1 else gc3[:, c - 1:c, :]
        qd_ref[d] = (q * jnp.exp(gc)).astype(BF16)
        kd_ref[d] = (k * jnp.exp(-gc)).astype(BF16)
        ke_ref[d] = (k.reshape(n_chunks, c, GLA_DK) * jnp.exp(gt - gc3)).reshape(seq, GLA_DK).astype(BF16)
        dec_ref[d] = jnp.exp(gt)
    sf_ref[...] = s0f_ref[...]
    sb_ref[...] = s0b_ref[...]

    ri = lax.broadcasted_iota(jnp.int32, (c, c), 0)
    ci = lax.broadcasted_iota(jnp.int32, (c, c), 1)

    unroll = min(8, n_chunks)
    dirs = ((0, ri >= ci, sf_ref, of_ref), (1, ri <= ci, sb_ref, ob_ref))

    def local_body(i, carry):
        for d, mask, st_ref, _ in dirs:
            n = i if d == 0 else n_chunks - 1 - i
            r0 = pl.multiple_of(n * c, c)
            scores = _mm_nt(qd_ref[d, pl.ds(r0, c), :], kd_ref[d, pl.ds(r0, c), :])
            att_ref[d, n] = jnp.where(mask, scores, 0.0).astype(BF16)
            st = st_ref[...]
            sp_ref[d, n] = st.astype(BF16)
            st_ref[...] = st * dec_ref[d, n] + _mm_tn(v_ref[pl.ds(r0, c), :], ke_ref[d, pl.ds(r0, c), :])
        return carry

    lax.fori_loop(0, n_chunks, local_body, 0, unroll=unroll)

    def out_body(n, carry):
        r0 = pl.multiple_of(n * c, c)
        v = v_ref[pl.ds(r0, c), :]
        for d, _, _, o_ref in dirs:
            o_ref[pl.ds(r0, c), :] = _mm(att_ref[d, n], v) + _mm_nt(qd_ref[d, pl.ds(r0, c), :], sp_ref[d, n])
        return carry

    lax.fori_loop(0, n_chunks, out_body, 0, unroll=unroll)

    o = of_ref[...] + ob_ref[...]
    ms = jnp.mean(o * o, axis=-1, keepdims=True)
    y = o * lax.rsqrt(ms + EPS) * nw_ref[...] * _silu(r_ref[...].astype(F32))
    y_ref[...] = y.astype(BF16)


def _gla_call(proj, glow, w2f, w2b, gbf, gbb, gla_nw, s0f, s0b, *, batch, seq):
    t = proj.shape[0]
    h = GLA_HEADS
    qb, kb = P_Q // GLA_DK, P_K // GLA_DK
    vb, rb = P_V // GLA_DV, P_R // GLA_DV
    st_spec = pl.BlockSpec((None, None, GLA_DV, GLA_DK), lambda b, hh: (b, hh, 0, 0))
    st_shape = jax.ShapeDtypeStruct((batch, h, GLA_DV, GLA_DK), F32)
    return pl.pallas_call(
        functools.partial(_gla_kernel, seq=seq),
        grid=(batch, h),
        in_specs=[
            pl.BlockSpec((seq, GLA_DK), lambda b, hh: (b, qb + hh)),
            pl.BlockSpec((seq, GLA_DK), lambda b, hh: (b, kb + hh)),
            pl.BlockSpec((seq, GLA_DV), lambda b, hh: (b, vb + hh)),
            pl.BlockSpec((seq, GLA_DV), lambda b, hh: (b, rb + hh)),
            pl.BlockSpec((seq, LANES), lambda b, hh: (b, 0)),
            pl.BlockSpec((None, LANES, GLA_DK), lambda b, hh: (hh, 0, 0)),
            pl.BlockSpec((None, LANES, GLA_DK), lambda b, hh: (hh, 0, 0)),
            pl.BlockSpec((None, 1, GLA_DK), lambda b, hh: (hh, 0, 0)),
            pl.BlockSpec((None, 1, GLA_DK), lambda b, hh: (hh, 0, 0)),
            pl.BlockSpec((None, 1, GLA_DV), lambda b, hh: (hh, 0, 0)),
            st_spec,
            st_spec,
        ],
        out_specs=[pl.BlockSpec((seq, GLA_DV), lambda b, hh: (b, hh)), st_spec, st_spec],
        out_shape=[jax.ShapeDtypeStruct((t, GLA_VT), BF16), st_shape, st_shape],
        scratch_shapes=[
            pltpu.VMEM((2, seq, GLA_DK), BF16),
            pltpu.VMEM((2, seq, GLA_DK), BF16),
            pltpu.VMEM((2, seq, GLA_DK), BF16),
            pltpu.VMEM((2, seq // GLA_CHUNK, 1, GLA_DK), F32),
            pltpu.VMEM((seq, GLA_DV), F32),
            pltpu.VMEM((seq, GLA_DV), F32),
            pltpu.VMEM((2, seq // GLA_CHUNK, GLA_CHUNK, GLA_CHUNK), BF16),
            pltpu.VMEM((2, seq // GLA_CHUNK, GLA_DV, GLA_DK), BF16),
        ],
        compiler_params=_params(("parallel", "parallel")),
        name="gla",
    )(proj, proj, proj, proj, glow, w2f, w2b, gbf, gbb, gla_nw, s0f, s0b)


def _fnet_kernel(u_ref, cl_ref, sl_ref, cg_ref, sg_ref, o_ref, *, scale):
    u = u_ref[...]
    p = _mm(cl_ref[...], u).astype(BF16)
    q = _mm(sl_ref[...], u).astype(BF16)
    y = _mm(p, cg_ref[...]) - _mm(q, sg_ref[...])
    o_ref[...] = (y * scale).astype(BF16)


def _fnet_call(proj, cl, sl, cg, sg, *, batch, seq):
    t = proj.shape[0]
    tr = min(512, seq)
    nr = seq // tr
    ub = P_FN // FNET_W
    return pl.pallas_call(
        functools.partial(_fnet_kernel, scale=1.0 / math.sqrt(seq * FNET_GW)),
        grid=(nr, batch),
        in_specs=[
            pl.BlockSpec((seq, FNET_W), lambda r, b: (b, ub)),
            pl.BlockSpec((tr, seq), lambda r, b: (r, 0)),
            pl.BlockSpec((tr, seq), lambda r, b: (r, 0)),
            pl.BlockSpec((FNET_W, FNET_W), lambda r, b: (0, 0)),
            pl.BlockSpec((FNET_W, FNET_W), lambda r, b: (0, 0)),
        ],
        out_specs=pl.BlockSpec((tr, FNET_W), lambda r, b: (b * nr + r, 0)),
        out_shape=jax.ShapeDtypeStruct((t, FNET_W), BF16),
        compiler_params=_params(("arbitrary", "arbitrary")),
        name="fnet",
    )(proj, cl, sl, cg, sg)


def _hy_posenc_kernel(c_ref, s_ref, *, seq):
    ti = lax.broadcasted_iota(jnp.int32, (seq, LANES), 0).astype(F32)
    lane = lax.broadcasted_iota(jnp.int32, (seq, LANES), 1)
    band_step = (HY_BANDS - 1 - 1e-4) / (HY_BANDS - 1)
    bands = jnp.where(lane < HY_BANDS, 1e-4 + lane.astype(F32) * band_step, 0.0)
    ang = 2.0 * math.pi * ti * bands / seq
    c_ref[...] = jnp.cos(ang)
    s_ref[...] = -jnp.sin(ang)


def _hy_posenc_call(seq):
    shape = jax.ShapeDtypeStruct((seq, LANES), F32)
    return pl.pallas_call(
        functools.partial(_hy_posenc_kernel, seq=seq),
        out_shape=[shape, shape],
        compiler_params=pltpu.CompilerParams(vmem_limit_bytes=VMEM_LIMIT_BYTES),
        name="hy_posenc",
    )()


def _hy_filter_kernel(zc_ref, zs_ref, w1t_ref, w1c_ref, w1s_ref, b1_ref, w2_ref, b2_ref, w3_ref, fr_ref, o_ref,
                      h2_ref, *, seq):
    t_col = lax.broadcasted_iota(jnp.int32, (seq, 1), 0).astype(F32) / (seq - 1.0)

    @pl.when(pl.program_id(0) == 0)
    def _():
        fr = fr_ref[...]
        pre1 = t_col * w1t_ref[...] + _mm3(zc_ref[...], w1c_ref[...]) + _mm3(zs_ref[...], w1s_ref[...]) + b1_ref[...]
        h1 = jnp.sin(fr * pre1)
        h2_ref[...] = jnp.sin(fr * (_mm3(h1, w2_ref[...]) + b2_ref[...]))

    h2 = h2_ref[...]
    ch = lax.broadcasted_iota(jnp.int32, (1, HY_W), 1).astype(F32)
    d0 = math.log(HY_TARGET) / HY_FAST
    d1 = math.log(HY_TARGET) / HY_SLOW
    deltas = jnp.abs(d0 + ch * ((d1 - d0) / (HY_W - 1)))
    win = jnp.exp(-t_col * deltas)
    ss = jnp.zeros((1, HY_W), F32)
    for lo in (0, HY_W):
        hd = _mm3(h2, w3_ref[:, lo:lo + HY_W]) * win
        o_ref[:, lo:lo + HY_W] = hd
        ss = ss + jnp.sum(hd * hd, axis=0, keepdims=True)
    inv = lax.rsqrt(ss + EPS)
    for lo in (0, HY_W):
        o_ref[:, lo:lo + HY_W] = o_ref[:, lo:lo + HY_W] * inv


def _hy_filter_call(zc, zs, w1t, w1c, w1s, b1, w2, b2, w3, fr, *, seq):
    hid = HY_HID
    full = lambda shape: pl.BlockSpec(shape, lambda o: (0,) * len(shape))
    return pl.pallas_call(
        functools.partial(_hy_filter_kernel, seq=seq),
        grid=(2,),
        in_specs=[
            full((seq, LANES)), full((seq, LANES)),
            full((1, hid)), full((LANES, hid)), full((LANES, hid)), full((1, hid)),
            full((hid, hid)), full((1, hid)),
            pl.BlockSpec((hid, 2 * HY_W), lambda o: (0, o)),
            full((1, hid)),
        ],
        out_specs=pl.BlockSpec((seq, 2 * HY_W), lambda o: (0, o)),
        out_shape=jax.ShapeDtypeStruct((seq, 4 * HY_W), F32),
        scratch_shapes=[pltpu.VMEM((seq, hid), F32)],
        compiler_params=_params(("arbitrary",)),
        name="hy_filter",
    )(zc, zs, w1t, w1c, w1s, b1, w2, b2, w3, fr)


def _hy_spec_kernel(h_ref, c_ref, s_ref, rc_ref, rs_ref, o_ref, hb_ref, *, seq, tf):
    f = pl.program_id(1)

    @pl.when(f == 0)
    def _():
        hb_ref[...] = h_ref[...].astype(BF16)

    cm = c_ref[...]
    sm = s_ref[...]
    hfw = hb_ref[:, :HY_W]
    hbw = hb_ref[:, HY_W:]
    pf = _mm(cm, hfw)
    pb = _mm(cm, hbw)
    qf = _mm(sm, hfw)
    qb = _mm(sm, hbw)
    reps = HY_W // LANES
    cf = jnp.concatenate([rc_ref[...]] * reps, axis=1)
    sf = jnp.concatenate([rs_ref[...]] * reps, axis=1)
    kre = pf + cf * pb - sf * qb
    kim = cf * qb + sf * pb - qf
    row0 = (lax.broadcasted_iota(jnp.int32, (tf, HY_W), 0) + f * tf) == 0
    wgt = jnp.where(row0, 0.5 / seq, 1.0 / seq)
    k1 = kre * wgt
    o_ref[0] = k1
    o_ref[1] = jnp.where(row0, 0.0, kim * wgt)
    o_ref[2] = jnp.where(row0, (qf - qb) * (0.5 / seq), k1)


def _hy_spec_call(h, cmat, smat, rot_c, rot_s, *, seq):
    tf = min(512, seq)
    nf = seq // tf
    return pl.pallas_call(
        functools.partial(_hy_spec_kernel, seq=seq, tf=tf),
        grid=(2, nf),
        in_specs=[
            pl.BlockSpec((seq, 2 * HY_W), lambda o, f: (0, o)),
            pl.BlockSpec((tf, seq), lambda o, f: (f, 0)),
            pl.BlockSpec((tf, seq), lambda o, f: (f, 0)),
            pl.BlockSpec((tf, LANES), lambda o, f: (f, 0)),
            pl.BlockSpec((tf, LANES), lambda o, f: (f, 0)),
        ],
        out_specs=pl.BlockSpec((None, 3, tf, HY_W), lambda o, f: (o, 0, f, 0)),
        out_shape=jax.ShapeDtypeStruct((2, 3, seq, HY_W), F32),
        scratch_shapes=[pltpu.VMEM((seq, 2 * HY_W), BF16)],
        compiler_params=_params(("arbitrary", "arbitrary")),
        name="hy_spectrum",
    )(h, cmat, smat, rot_c, rot_s)


def _short_conv(u, w, b, seg):
    n = u.shape[0]
    pos = lax.broadcasted_iota(jnp.int32, u.shape, 0) % seg
    prev = jnp.where(pos == 0, 0.0, pltpu.roll(u, 1, axis=0))
    nxt = jnp.where(pos == seg - 1, 0.0, pltpu.roll(u, n - 1, axis=0))
    return prev * w[0:1, :] + u * w[1:2, :] + nxt * w[2:3, :] + b


def _hy_conv_kernel(u_ref, m_ref, wu_ref, bu_ref, wm_ref, bm_ref, bias_ref, k_ref, cr_ref, sr_ref, cc_ref, sc_ref,
                    o_ref, ub_ref, acc_ref, *, conv_u, seg, nf, batch, seq):
    f = pl.program_id(1)
    tc = o_ref.shape[1]

    def load_u(b):
        u = u_ref[b * seq:(b + 1) * seq, :].astype(F32)
        if conv_u:
            u = _short_conv(u, wu_ref[...], bu_ref[...], seg)
        return u

    @pl.when(f == 0)
    def _():
        for b in range(batch):
            ub_ref[:, b * tc:(b + 1) * tc] = load_u(b).astype(BF16)
        acc_ref[...] = jnp.zeros(acc_ref.shape, F32)

    ub = ub_ref[...]
    p = _mm(cr_ref[...], ub)
    q = _mm(sr_ref[...], ub)
    k1, k2, k3 = (jnp.concatenate([k_ref[j]] * batch, axis=1) for j in range(3))
    av = (p * k1 + q * k2).astype(BF16)
    bv = (q * k3 - p * k2).astype(BF16)
    acc_ref[...] += _mm(cc_ref[...], av) + _mm(sc_ref[...], bv)

    @pl.when(f == nf - 1)
    def _():
        for b in range(batch):
            m = _short_conv(m_ref[b * seq:(b + 1) * seq, :].astype(F32), wm_ref[...], bm_ref[...], seg)
            y = acc_ref[:, b * tc:(b + 1) * tc] + bias_ref[...] * ub_ref[:, b * tc:(b + 1) * tc].astype(F32)
            o_ref[b * seq:(b + 1) * seq, :] = (m * y).astype(o_ref.dtype)


def _hy_conv_call(u_arr, u_col, m_arr, m_col, conv_w, conv_b, u_sect, m_sect, bias, kspec, order, cmat, smat,
                  smat_inv, *, batch, seq, seg, conv_u):
    t = u_arr.shape[0]
    tc = 256
    nch = HY_W // tc
    tf = min(256, seq)
    nf = seq // tf
    ucb, mcb = u_col // tc, m_col // tc
    usb, msb = u_sect * nch, m_sect * nch
    once = pl.Buffered(1)
    return pl.pallas_call(
        functools.partial(_hy_conv_kernel, conv_u=conv_u, seg=seg, nf=nf, batch=batch, seq=seq),
        grid=(nch, nf),
        in_specs=[
            pl.BlockSpec((t, tc), lambda ch, f: (0, ucb + ch), pipeline_mode=once),
            pl.BlockSpec((t, tc), lambda ch, f: (0, mcb + ch), pipeline_mode=once),
            pl.BlockSpec((3, tc), lambda ch, f: (0, usb + ch)),
            pl.BlockSpec((1, tc), lambda ch, f: (0, usb + ch)),
            pl.BlockSpec((3, tc), lambda ch, f: (0, msb + ch)),
            pl.BlockSpec((1, tc), lambda ch, f: (0, msb + ch)),
            pl.BlockSpec((1, tc), lambda ch, f: (0, ch)),
            pl.BlockSpec((None, 3, tf, tc), lambda ch, f: (order, 0, f, ch)),
            pl.BlockSpec((tf, seq), lambda ch, f: (f, 0)),
            pl.BlockSpec((tf, seq), lambda ch, f: (f, 0)),
            pl.BlockSpec((seq, tf), lambda ch, f: (0, f)),
            pl.BlockSpec((seq, tf), lambda ch, f: (0, f)),
        ],
        out_specs=pl.BlockSpec((t, tc), lambda ch, f: (0, ch)),
        out_shape=jax.ShapeDtypeStruct((t, HY_W), BF16),
        scratch_shapes=[pltpu.VMEM((seq, batch * tc), BF16), pltpu.VMEM((seq, batch * tc), F32)],
        compiler_params=_params(("arbitrary", "arbitrary")),
        name="hy_conv",
    )(u_arr, m_arr, conv_w, conv_b, conv_w, conv_b, bias, kspec, cmat, smat, cmat, smat_inv)


MERGE_CHUNK = 512


def _merge_kernel(x_ref, mod_ref, nw_ref, yg_ref, yf_ref, yh_ref, g0_ref, g1_ref, g2_ref, wg_ref, wf_ref, wh_ref,
                  wo_ref, o_ref, xn_ref, mg_ref):
    d = o_ref.shape[1]
    yg = yg_ref[...]
    yf = yf_ref[...]
    yh = yh_ref[...]
    for c0 in range(0, d, MERGE_CHUNK):
        cols = slice(c0, c0 + MERGE_CHUNK)
        m = g0_ref[:, cols].astype(F32) * _mm(yg, wg_ref[:, cols])
        m = m + g1_ref[:, cols].astype(F32) * _mm(yf, wf_ref[:, cols])
        m = m + g2_ref[:, cols].astype(F32) * _mm(yh, wh_ref[:, cols])
        mg_ref[:, cols] = m.astype(BF16)
    mg = mg_ref[...]
    for c0 in range(0, d, MERGE_CHUNK):
        cols = slice(c0, c0 + MERGE_CHUNK)
        o_ref[:, cols] = x_ref[:, cols] + mod_ref[5:6, cols] * _mm(mg, wo_ref[:, cols])
    xn_ref[...] = _modnorm(o_ref[...], nw_ref[2:3, :], mod_ref[6:7, :], mod_ref[7:8, :]).astype(BF16)


def _merge_call(xs, mod, norm_w, proj, y_gla, y_fnet, y_hy, wbg, wbf, wbh, wo, *, layer, rows_per_mod, fixed_row):
    t, d = xs.shape
    tm = 512
    row = _mod_row_map(rows_per_mod, tm, fixed_row)
    gate = lambda j: pl.BlockSpec((pl.Element(tm), pl.Element(d)), lambda i: (i * tm, P_GATE + j * d))
    weight = lambda k: pl.BlockSpec((None, k, d), lambda i: (layer, 0, 0), pipeline_mode=pl.Buffered(1))
    return pl.pallas_call(
        _merge_kernel,
        grid=(t // tm,),
        in_specs=[
            pl.BlockSpec((tm, d), lambda i: (i, 0)),
            pl.BlockSpec((None, None, N_ADA, d), lambda i: (layer, row(i), 0, 0)),
            pl.BlockSpec((None, 3, d), lambda i: (layer, 0, 0)),
            pl.BlockSpec((tm, GLA_VT), lambda i: (i, 0)),
            pl.BlockSpec((tm, FNET_W), lambda i: (i, 0)),
            pl.BlockSpec((tm, HY_W), lambda i: (i, 0)),
            gate(0),
            gate(1),
            gate(2),
            weight(GLA_VT),
            weight(FNET_W),
            weight(HY_W),
            weight(d),
        ],
        out_specs=[pl.BlockSpec((tm, d), lambda i: (i, 0)), pl.BlockSpec((tm, d), lambda i: (i, 0))],
        out_shape=[jax.ShapeDtypeStruct((t, d), F32), jax.ShapeDtypeStruct((t, d), BF16)],
        scratch_shapes=[pltpu.VMEM((tm, d), BF16)],
        compiler_params=_params(("parallel",)),
        name="merge_out",
    )(xs, mod, norm_w, y_gla, y_fnet, y_hy, proj, proj, proj, wbg, wbf, wbh, wo)


def _trig_tables(n, period):
    split = 32
    c = lax.broadcasted_iota(jnp.int32, (1, n), 1)

    def rows(r):
        ang = ((r * c) % period).astype(F32) * (2.0 * math.pi / period)
        return jnp.cos(ang), jnp.sin(ang)

    c_lo, s_lo = rows(lax.broadcasted_iota(jnp.int32, (split, 1), 0))
    c_hi, s_hi = rows(lax.broadcasted_iota(jnp.int32, (n // split, 1), 0) * split)
    cos = c_hi[:, None, :] * c_lo[None] - s_hi[:, None, :] * s_lo[None]
    sin = s_hi[:, None, :] * c_lo[None] + c_hi[:, None, :] * s_lo[None]
    return cos.reshape(n, n), sin.reshape(n, n)


def _fnet_tables(seq):
    cl, sl = _trig_tables(seq, seq)
    cg, sg = _trig_tables(FNET_GW, FNET_GW)
    eye = jnp.eye(FNET_GROUPS, dtype=F32)
    return cl.astype(BF16), sl.astype(BF16), jnp.kron(eye, cg).astype(BF16), jnp.kron(eye, sg).astype(BF16)


def _hyena_tables(seq):
    cm, sm = _trig_tables(seq, 2 * seq)
    r = lax.broadcasted_iota(jnp.int32, (seq, seq), 0)
    c = lax.broadcasted_iota(jnp.int32, (seq, seq), 1)
    sm_fwd = jnp.where(r == 0, (1 - 2 * (c % 2)).astype(F32), sm)
    sm_inv = jnp.where(c == 0, (1 - 2 * (r % 2)).astype(F32), sm)
    fr = lax.broadcasted_iota(jnp.int32, (seq, LANES), 0).astype(F32) * (math.pi / seq)
    return cm.astype(BF16), sm_fwd.astype(BF16), sm_inv.astype(BF16), jnp.cos(fr), jnp.sin(fr)


def _gla_gate_weights(gla_w2, gla_gb):
    depth = gla_w2.shape[0]
    w = gla_w2.reshape(depth, 2, GLA_RANK, GLA_HEADS, GLA_DK).transpose(0, 1, 3, 2, 4)
    zf = jnp.zeros((depth, GLA_HEADS, LANES, GLA_DK), F32)
    w2f = zf.at[:, :, :GLA_RANK].set(w[:, 0]).astype(BF16)
    w2b = zf.at[:, :, GLA_RANK:2 * GLA_RANK].set(w[:, 1]).astype(BF16)
    gb = gla_gb.reshape(depth, 2, GLA_HEADS, 1, GLA_DK)
    return w2f, w2b, gb[:, 0], gb[:, 1]


def kernel(x, c, ctx, c_ctx, ada_w, ada_b, norm_w, ffn1_wi, ffn1_wo, ffn2_wi, ffn2_wo, w_in, gla_w2, gla_gb,
           gla_norm_w, hy_conv_w, hy_conv_b, hy_f1_w, hy_f1_b, hy_f2_w, hy_f2_b, hy_f3_w, hy_freq, hy_bias,
           w_br_gla, w_br_fnet, w_br_hy, w_o, final_norm_w):
    batch, seq, d = x.shape
    ctx_len = ctx.shape[1]
    xs = x.reshape(batch * seq, d)
    cs = ctx.reshape(batch * ctx_len, d)

    cond = jnp.concatenate([c, c_ctx[None, :], jnp.zeros((8 - batch - 1, d), F32)], axis=0)
    mod = _ada_call(cond, ada_w, ada_b)
    ctx_row = batch

    w_t = jnp.swapaxes(w_in, 1, 2)
    wbg, wbf, wbh, wob = (w.astype(BF16) for w in (w_br_gla, w_br_fnet, w_br_hy, w_o))
    w2f, w2b, gbf, gbb = _gla_gate_weights(gla_w2, gla_gb)
    gla_nw = gla_norm_w.reshape(DEPTH, GLA_HEADS, 1, GLA_DV)

    fnet_tab = {n: _fnet_tables(n) for n in (seq, ctx_len)}
    hy_tab = {n: _hyena_tables(n) for n in (seq, ctx_len)}
    hy_pos = {n: _hy_posenc_call(n) for n in (seq, ctx_len)}

    def mixers(proj, glow, layer, n, seg, s0f, s0b, need_y):
        y_gla, sf, sb = _gla_call(proj, glow, w2f[layer], w2b[layer], gbf[layer], gbb[layer], gla_nw[layer],
                                  s0f, s0b, batch=batch, seq=n)
        if not need_y:
            return None, sf, sb
        y_fnet = _fnet_call(proj, *fnet_tab[n], batch=batch, seq=n)
        cm, sm, sm_inv, rot_c, rot_s = hy_tab[n]
        w1 = hy_f1_w[layer]
        zpad = jnp.zeros((LANES - HY_BANDS, HY_HID), F32)
        h = _hy_filter_call(*hy_pos[n], w1[0:1], jnp.concatenate([w1[1:1 + HY_BANDS], zpad], 0),
                            jnp.concatenate([w1[1 + HY_BANDS:], zpad], 0), hy_f1_b[layer][None, :],
                            hy_f2_w[layer], hy_f2_b[layer][None, :], hy_f3_w[layer], hy_freq[layer][None, :], seq=n)
        kspec = _hy_spec_call(h, cm, sm, rot_c, rot_s, seq=n)
        cw, cb = hy_conv_w[layer], hy_conv_b[layer][None, :]
        z = _hy_conv_call(proj, P_HY, proj, P_HY + HY_W, cw, cb, 0, 1, hy_bias[layer, 0][None, :], kspec, 0, cm, sm,
                          sm_inv, batch=batch, seq=n, seg=seg, conv_u=True)
        y_hy = _hy_conv_call(z, 0, proj, P_HY + 2 * HY_W, cw, cb, 0, 2, hy_bias[layer, 1][None, :], kspec, 1, cm, sm,
                             sm_inv, batch=batch, seq=n, seg=seg, conv_u=False)
        return (y_gla, y_fnet, y_hy), sf, sb

    s_zero = jnp.zeros((batch, GLA_HEADS, GLA_DV, GLA_DK), F32)
    for layer in range(DEPTH):
        last = layer == DEPTH - 1
        x_kw = dict(layer=layer, rows_per_mod=seq, fixed_row=None)
        c_kw = dict(layer=layer, rows_per_mod=ctx_len, fixed_row=ctx_row)
        xs = _ffn(xs, _norm_call(xs, mod, norm_w, sub=0, **x_kw), mod, ffn1_wi, ffn1_wo, sub=0, **x_kw)
        cs = _ffn(cs, _norm_call(cs, mod, norm_w, sub=0, **c_kw), mod, ffn1_wi, ffn1_wo, sub=0, **c_kw)
        cn, glow_c = _mixnorm_call(cs, mod, norm_w, w_t, **c_kw)
        proj_c = _proj_call(cn, w_t, layer=layer)
        ys_c, sf, sb = mixers(proj_c, glow_c, layer, ctx_len, ctx_len, s_zero, s_zero, not last)
        if not last:
            cs, cn2 = _merge_call(cs, mod, norm_w, proj_c, *ys_c, wbg, wbf, wbh, wob, **c_kw)
            cs = _ffn(cs, cn2, mod, ffn2_wi, ffn2_wo, sub=2, **c_kw)
        xn, glow_x = _mixnorm_call(xs, mod, norm_w, w_t, **x_kw)
        proj_x = _proj_call(xn, w_t, layer=layer)
        ys_x, _, _ = mixers(proj_x, glow_x, layer, seq, GRID_W, sf, sb, True)
        xs, xn2 = _merge_call(xs, mod, norm_w, proj_x, *ys_x, wbg, wbf, wbh, wob, **x_kw)
        xs = _ffn(xs, xn2, mod, ffn2_wi, ffn2_wo, sub=2, **x_kw)
    return _final_norm_call(xs, final_norm_w.reshape(1, d)).reshape(batch, seq, d)
```

```python
import functools
import math

import jax
import jax.numpy as jnp
from jax import lax
from jax.experimental import pallas as pl
from jax.experimental.pallas import tpu as pltpu

F32 = jnp.float32
BF16 = jnp.bfloat16

D_MODEL = 2048
DEPTH = 4
GRID_W = 64
N_ADA = 9
D_FF = 5504
GLA_HEADS = 4
GLA_DK = 128
GLA_DV = 256
GLA_KT = GLA_HEADS * GLA_DK
GLA_VT = GLA_HEADS * GLA_DV
GLA_RANK = 16
GLA_TAU = 16.0
GLA_CHUNK = 64
FNET_GROUPS = 4
FNET_GW = 128
FNET_W = FNET_GROUPS * FNET_GW
HY_W = 512
HY_BANDS = 16
HY_HID = 64
HY_FAST = 0.3
HY_SLOW = 1.5
HY_TARGET = 1e-2
EPS = 1e-6

LANES = 128
VMEM_LIMIT_BYTES = 56 * 1024 * 1024

ADA_TN = 1024
NORM_TM = 1024
WS_TM = 2048
DOWN_TM = 512
DOWN_TN = 512
MERGE_TM = 512
DFT_TILE = 512
CONV_TC = 256
CONV_TF = 256
GLA_UNROLL = 8

FF_TILE = 512
FF_STEPS = -(-D_FF // FF_TILE)

W_GLOW = 2 * GLA_KT + GLA_VT
W_IN_COLS = W_GLOW + 2 * GLA_RANK + GLA_VT + FNET_W + 3 * HY_W + 3 * D_MODEL
P_TILE = 1024
P_Q = 0
P_K = GLA_KT
P_V = 2 * GLA_KT
P_R = W_GLOW
P_FN = P_R + GLA_VT
P_HY = P_FN + FNET_W
P_GATE = P_HY + 3 * HY_W
P_TOTAL = P_GATE + 3 * D_MODEL


def _mm(a, b):
    return jnp.dot(a, b, preferred_element_type=F32)


def _mm_nt(a, b):
    return lax.dot_general(a, b, (((1,), (1,)), ((), ())), preferred_element_type=F32)


def _mm_tn(a, b):
    return lax.dot_general(a, b, (((0,), (0,)), ((), ())), preferred_element_type=F32)


def _split2(a):
    hi = a.astype(BF16)
    lo = (a - hi.astype(F32)).astype(BF16)
    return hi, lo


def _mm3(a, b):
    ah, al = _split2(a)
    bh, bl = _split2(b)
    return _mm(ah, bh) + (_mm(ah, bl) + _mm(al, bh))


def _sigmoid(x):
    return 0.5 * jnp.tanh(0.5 * x) + 0.5


def _silu(x):
    return x * _sigmoid(x)


def _params(sem):
    return pltpu.CompilerParams(dimension_semantics=sem, vmem_limit_bytes=VMEM_LIMIT_BYTES)


def _modnorm(x, nw, shift, scale):
    ms = jnp.mean(x * x, axis=-1, keepdims=True)
    y = x * lax.rsqrt(ms + EPS) * nw
    return y * (1.0 + scale) + shift


def _ada_kernel(c_ref, w_ref, b_ref, o_ref):
    a = _silu(c_ref[...]).astype(BF16)
    o_ref[...] = _mm(a, w_ref[...].astype(BF16)) + b_ref[...]


def _ada_call(cond, ada_w, ada_b):
    depth, d, n = ada_w.shape
    rows = cond.shape[0]
    tn = ADA_TN
    out = pl.pallas_call(
        _ada_kernel,
        grid=(depth, n // tn),
        in_specs=[
            pl.BlockSpec((rows, d), lambda l, j: (0, 0)),
            pl.BlockSpec((None, d, tn), lambda l, j: (l, 0, j)),
            pl.BlockSpec((None, 1, tn), lambda l, j: (l, 0, j)),
        ],
        out_specs=pl.BlockSpec((None, rows, tn), lambda l, j: (l, 0, j)),
        out_shape=jax.ShapeDtypeStruct((depth, rows, n), F32),
        compiler_params=_params(("arbitrary", "arbitrary")),
        name="ada_mod",
    )(cond, ada_w, ada_b.reshape(depth, 1, n))
    return out.reshape(depth, rows, N_ADA, d)


def _mod_row_map(rows_per_mod, tm, fixed_row):
    if fixed_row is not None:
        return lambda i: fixed_row
    per = rows_per_mod // tm
    return lambda i: i // per


def _norm_kernel(x_ref, mod_ref, nw_ref, o_ref, *, sub):
    xn = _modnorm(x_ref[...], nw_ref[sub:sub + 1, :], mod_ref[3 * sub:3 * sub + 1, :],
                  mod_ref[3 * sub + 1:3 * sub + 2, :])
    o_ref[...] = xn.astype(BF16)


def _norm_call(xs, mod, norm_w, *, layer, sub, rows_per_mod, fixed_row):
    t, d = xs.shape
    tm = NORM_TM
    row = _mod_row_map(rows_per_mod, tm, fixed_row)
    return pl.pallas_call(
        functools.partial(_norm_kernel, sub=sub),
        grid=(t // tm,),
        in_specs=[
            pl.BlockSpec((tm, d), lambda i: (i, 0)),
            pl.BlockSpec((None, None, N_ADA, d), lambda i: (layer, row(i), 0, 0)),
            pl.BlockSpec((None, 3, d), lambda i: (layer, 0, 0)),
        ],
        out_specs=pl.BlockSpec((tm, d), lambda i: (i, 0)),
        out_shape=jax.ShapeDtypeStruct((t, d), BF16),
        compiler_params=_params(("parallel",)),
        name="ffn_norm",
    )(xs, mod, norm_w)


def _ffn_up_kernel(xn_ref, wa_ref, wg_ref, h_ref, wab_ref, wgb_ref):
    @pl.when(pl.program_id(1) == 0)
    def _():
        wab_ref[...] = wa_ref[...].astype(BF16)
        wgb_ref[...] = wg_ref[...].astype(BF16)

    xn = xn_ref[...]
    a = _mm(xn, wab_ref[...])
    g = _mm(xn, wgb_ref[...])
    h_ref[...] = (_silu(g) * a).astype(BF16)


def _ffn_up_call(xn, wi, *, layer):
    t, d = xn.shape
    tm = min(WS_TM, t)
    back = (FF_STEPS * FF_TILE - D_FF) // LANES
    hid_blk = lambda f: f * (FF_TILE // LANES) - (f // (FF_STEPS - 1)) * back
    hid0 = lambda f: hid_blk(f) * LANES
    gate0 = lambda f: (D_FF // LANES + hid_blk(f)) * LANES
    return pl.pallas_call(
        _ffn_up_kernel,
        grid=(FF_STEPS, t // tm),
        in_specs=[
            pl.BlockSpec((tm, d), lambda f, i: (i, 0)),
            pl.BlockSpec((None, pl.Element(d), pl.Element(FF_TILE)), lambda f, i: (layer, 0, hid0(f))),
            pl.BlockSpec((None, pl.Element(d), pl.Element(FF_TILE)), lambda f, i: (layer, 0, gate0(f))),
        ],
        out_specs=pl.BlockSpec((pl.Element(tm), pl.Element(FF_TILE)), lambda f, i: (i * tm, hid0(f))),
        out_shape=jax.ShapeDtypeStruct((t, D_FF), BF16),
        scratch_shapes=[pltpu.VMEM((d, FF_TILE), BF16), pltpu.VMEM((d, FF_TILE), BF16)],
        compiler_params=_params(("arbitrary", "arbitrary")),
        name="ffn_up",
    )(xn, wi, wi)


def _ffn_down_kernel(h_ref, w_ref, x_ref, mod_ref, o_ref, wb_ref, *, sub):
    @pl.when(pl.program_id(1) == 0)
    def _():
        wb_ref[...] = w_ref[...].astype(BF16)

    y = _mm(h_ref[...], wb_ref[...])
    o_ref[...] = x_ref[...] + 0.5 * mod_ref[3 * sub + 2:3 * sub + 3, :] * y


def _ffn_down_call(h, wo, xs, mod, *, layer, sub, rows_per_mod, fixed_row):
    t, d = xs.shape
    tm = DOWN_TM
    tn = DOWN_TN
    row = _mod_row_map(rows_per_mod, tm, fixed_row)
    return pl.pallas_call(
        functools.partial(_ffn_down_kernel, sub=sub),
        grid=(d // tn, t // tm),
        in_specs=[
            pl.BlockSpec((tm, D_FF), lambda n, i: (i, 0)),
            pl.BlockSpec((None, D_FF, tn), lambda n, i: (layer, 0, n)),
            pl.BlockSpec((tm, tn), lambda n, i: (i, n)),
            pl.BlockSpec((None, None, N_ADA, tn), lambda n, i: (layer, row(i), 0, n)),
        ],
        out_specs=pl.BlockSpec((tm, tn), lambda n, i: (i, n)),
        out_shape=jax.ShapeDtypeStruct((t, d), F32),
        scratch_shapes=[pltpu.VMEM((D_FF, tn), BF16)],
        compiler_params=_params(("arbitrary", "arbitrary")),
        name="ffn_down",
    )(h, wo, xs, mod)


def _ffn(xs, xn, mod, wi, wo, *, layer, sub, rows_per_mod, fixed_row):
    h = _ffn_up_call(xn, wi, layer=layer)
    return _ffn_down_call(h, wo, xs, mod, layer=layer, sub=sub, rows_per_mod=rows_per_mod, fixed_row=fixed_row)


def _final_norm_kernel(x_ref, w_ref, o_ref):
    x = x_ref[...]
    ms = jnp.mean(x * x, axis=-1, keepdims=True)
    o_ref[...] = x * lax.rsqrt(ms + EPS) * w_ref[...]


def _final_norm_call(xs, w):
    t, d = xs.shape
    tm = NORM_TM
    return pl.pallas_call(
        _final_norm_kernel,
        grid=(t // tm,),
        in_specs=[pl.BlockSpec((tm, d), lambda i: (i, 0)), pl.BlockSpec((1, d), lambda i: (0, 0))],
        out_specs=pl.BlockSpec((tm, d), lambda i: (i, 0)),
        out_shape=jax.ShapeDtypeStruct((t, d), F32),
        compiler_params=_params(("parallel",)),
        name="final_norm",
    )(xs, w)


def _mixnorm_kernel(x_ref, mod_ref, nw_ref, wl_ref, xn_ref, gl_ref):
    xn = _modnorm(x_ref[...], nw_ref[1:2, :], mod_ref[3:4, :], mod_ref[4:5, :]).astype(BF16)
    xn_ref[...] = xn
    gl_ref[...] = _mm_nt(xn, wl_ref[...].astype(BF16)).astype(BF16)


def _mixnorm_call(xs, mod, norm_w, w_t, *, layer, rows_per_mod, fixed_row):
    t, d = xs.shape
    tm = NORM_TM
    row = _mod_row_map(rows_per_mod, tm, fixed_row)
    return pl.pallas_call(
        _mixnorm_kernel,
        grid=(t // tm,),
        in_specs=[
            pl.BlockSpec((tm, d), lambda i: (i, 0)),
            pl.BlockSpec((None, None, N_ADA, d), lambda i: (layer, row(i), 0, 0)),
            pl.BlockSpec((None, 3, d), lambda i: (layer, 0, 0)),
            pl.BlockSpec((None, LANES, d), lambda i: (layer, W_GLOW // LANES, 0)),
        ],
        out_specs=[pl.BlockSpec((tm, d), lambda i: (i, 0)), pl.BlockSpec((tm, LANES), lambda i: (i, 0))],
        out_shape=[jax.ShapeDtypeStruct((t, d), BF16), jax.ShapeDtypeStruct((t, LANES), BF16)],
        compiler_params=_params(("parallel",)),
        name="mix_norm",
    )(xs, mod, norm_w, w_t)


P_TILES = P_TOTAL // P_TILE
P_SKIP = 2 * GLA_RANK
P_GATE_TILE = P_GATE // P_TILE


def _proj_row0(n):
    sublanes = 8
    first = W_GLOW // P_TILE
    after = (n + (P_TILES - first)) // P_TILES
    return (n * (P_TILE // sublanes) + after * (P_SKIP // sublanes)) * sublanes


def _proj_kernel(xn_ref, w_ref, cw_ref, cb_ref, o_ref, wb_ref, *, seg):
    n = pl.program_id(0)
    conv_lo = P_HY // P_TILE

    @pl.when(pl.program_id(1) == 0)
    def _():
        wb_ref[...] = w_ref[...].astype(BF16)

    @pl.when(n < conv_lo)
    def _():
        o_ref[...] = _mm_nt(xn_ref[...], wb_ref[...]).astype(BF16)

    @pl.when(jnp.logical_and(n >= conv_lo, n < P_GATE_TILE))
    def _():
        y = _mm_nt(xn_ref[...], wb_ref[...])
        o_ref[...] = _short_conv(y, cw_ref[...], cb_ref[...], seg).astype(BF16)

    @pl.when(n >= P_GATE_TILE)
    def _():
        o_ref[...] = _sigmoid(_mm_nt(xn_ref[...], wb_ref[...])).astype(BF16)


def _proj_call(xn, w_t, conv_w, conv_b, *, layer, seg):
    t, d = xn.shape
    tm = min(WS_TM, t)
    assert tm % seg == 0
    return pl.pallas_call(
        functools.partial(_proj_kernel, seg=seg),
        grid=(P_TILES, t // tm),
        in_specs=[
            pl.BlockSpec((tm, d), lambda n, i: (i, 0)),
            pl.BlockSpec((None, pl.Element(P_TILE), pl.Element(d)), lambda n, i: (layer, _proj_row0(n), 0)),
            pl.BlockSpec((3, P_TILE), lambda n, i: (0, n)),
            pl.BlockSpec((1, P_TILE), lambda n, i: (0, n)),
        ],
        out_specs=pl.BlockSpec((tm, P_TILE), lambda n, i: (i, n)),
        out_shape=jax.ShapeDtypeStruct((t, P_TOTAL), BF16),
        scratch_shapes=[pltpu.VMEM((P_TILE, d), BF16)],
        compiler_params=_params(("arbitrary", "arbitrary")),
        name="mix_proj",
    )(xn, w_t, conv_w, conv_b)


def _log_sigmoid(x):
    return jnp.minimum(x, 0.0) - jnp.log(1.0 + jnp.exp(-jnp.abs(x)))


def _chunk_scan(x, reverse):
    n, w = x.shape
    sub = 8
    tiles = GLA_CHUNK // sub
    x3 = x.reshape(n // sub, sub, w)
    pos = lax.broadcasted_iota(jnp.int32, x3.shape, 1)
    s = 1
    while s < sub:
        if reverse:
            x3 = x3 + jnp.where(pos < sub - s, pltpu.roll(x3, sub - s, axis=1), 0.0)
        else:
            x3 = x3 + jnp.where(pos >= s, pltpu.roll(x3, s, axis=1), 0.0)
        s *= 2
    tot = jnp.broadcast_to(x3[:, 0:1, :] if reverse else x3[:, sub - 1:sub, :], x3.shape)
    tpos = lax.broadcasted_iota(jnp.int32, x3.shape, 0) % tiles
    acc = tot
    s = 1
    while s < tiles:
        pad = jnp.zeros((s, sub, w), F32)
        if reverse:
            acc = acc + jnp.where(tpos < tiles - s, jnp.concatenate([acc[s:], pad], axis=0), 0.0)
        else:
            acc = acc + jnp.where(tpos >= s, jnp.concatenate([pad, acc[:-s]], axis=0), 0.0)
        s *= 2
    return (x3 + (acc - tot)).reshape(n, w)


def _gla_kernel(q_ref, k_ref, v_ref, r_ref, gl_ref, w2f_ref, w2b_ref, gbf_ref, gbb_ref, nw_ref,
                s0f_ref, s0b_ref, y_ref, sf_ref, sb_ref, qd_ref, kd_ref, ke_ref, dec_ref, of_ref, ob_ref,
                att_ref, sp_ref, *, seq):
    c = GLA_CHUNK
    n_chunks = seq // c
    glow = gl_ref[...]
    q = q_ref[...].astype(F32) * GLA_DK ** -0.5
    k = k_ref[...].astype(F32)
    for d, (w2_ref, gb_ref) in enumerate(((w2f_ref, gbf_ref), (w2b_ref, gbb_ref))):
        lg = _log_sigmoid(_mm(glow, w2_ref[...]) + gb_ref[...]) * (1.0 / GLA_TAU)
        gc = _chunk_scan(lg, reverse=d == 1)
        gc3 = gc.reshape(n_chunks, c, GLA_DK)
        gt = gc3[:, 0:1, :] if d == 1 else gc3[:, c - 1:c, :]
        qd_ref[d] = (q * jnp.exp(gc)).astype(BF16)
        kd_ref[d] = (k * jnp.exp(-gc)).astype(BF16)
        ke_ref[d] = (k.reshape(n_chunks, c, GLA_DK) * jnp.exp(gt - gc3)).reshape(seq, GLA_DK).astype(BF16)
        dec_ref[d] = jnp.exp(gt)
    sf_ref[...] = s0f_ref[...]
    sb_ref[...] = s0b_ref[...]

    ri = lax.broadcasted_iota(jnp.int32, (c, c), 0)
    ci = lax.broadcasted_iota(jnp.int32, (c, c), 1)

    unroll = min(GLA_UNROLL, n_chunks)
    dirs = ((0, ri >= ci, sf_ref, of_ref), (1, ri <= ci, sb_ref, ob_ref))

    def local_body(i, carry):
        for d, mask, st_ref, _ in dirs:
            n = i if d == 0 else n_chunks - 1 - i
            r0 = pl.multiple_of(n * c, c)
            scores = _mm_nt(qd_ref[d, pl.ds(r0, c), :], kd_ref[d, pl.ds(r0, c), :])
            att_ref[d, n] = jnp.where(mask, scores, 0.0).astype(BF16)
            st = st_ref[...]
            sp_ref[d, n] = st.astype(BF16)
            st_ref[...] = st * dec_ref[d, n] + _mm_tn(v_ref[pl.ds(r0, c), :], ke_ref[d, pl.ds(r0, c), :])
        return carry

    lax.fori_loop(0, n_chunks, local_body, 0, unroll=unroll)

    def out_body(n, carry):
        r0 = pl.multiple_of(n * c, c)
        v = v_ref[pl.ds(r0, c), :]
        for d, _, _, o_ref in dirs:
            o_ref[pl.ds(r0, c), :] = _mm(att_ref[d, n], v) + _mm_nt(qd_ref[d, pl.ds(r0, c), :], sp_ref[d, n])
        return carry

    lax.fori_loop(0, n_chunks, out_body, 0, unroll=unroll)

    o = of_ref[...] + ob_ref[...]
    ms = jnp.mean(o * o, axis=-1, keepdims=True)
    y = o * lax.rsqrt(ms + EPS) * nw_ref[...] * _silu(r_ref[...].astype(F32))
    y_ref[...] = y.astype(BF16)


def _gla_call(proj, glow, w2f, w2b, gbf, gbb, gla_nw, s0f, s0b, *, batch, seq):
    t = proj.shape[0]
    h = GLA_HEADS
    qb, kb = P_Q // GLA_DK, P_K // GLA_DK
    vb, rb = P_V // GLA_DV, P_R // GLA_DV
    st_spec = pl.BlockSpec((None, None, GLA_DV, GLA_DK), lambda b, hh: (b, hh, 0, 0))
    st_shape = jax.ShapeDtypeStruct((batch, h, GLA_DV, GLA_DK), F32)
    return pl.pallas_call(
        functools.partial(_gla_kernel, seq=seq),
        grid=(batch, h),
        in_specs=[
            pl.BlockSpec((seq, GLA_DK), lambda b, hh: (b, qb + hh)),
            pl.BlockSpec((seq, GLA_DK), lambda b, hh: (b, kb + hh)),
            pl.BlockSpec((seq, GLA_DV), lambda b, hh: (b, vb + hh)),
            pl.BlockSpec((seq, GLA_DV), lambda b, hh: (b, rb + hh)),
            pl.BlockSpec((seq, LANES), lambda b, hh: (b, 0)),
            pl.BlockSpec((None, LANES, GLA_DK), lambda b, hh: (hh, 0, 0)),
            pl.BlockSpec((None, LANES, GLA_DK), lambda b, hh: (hh, 0, 0)),
            pl.BlockSpec((None, 1, GLA_DK), lambda b, hh: (hh, 0, 0)),
            pl.BlockSpec((None, 1, GLA_DK), lambda b, hh: (hh, 0, 0)),
            pl.BlockSpec((None, 1, GLA_DV), lambda b, hh: (hh, 0, 0)),
            st_spec,
            st_spec,
        ],
        out_specs=[pl.BlockSpec((seq, GLA_DV), lambda b, hh: (b, hh)), st_spec, st_spec],
        out_shape=[jax.ShapeDtypeStruct((t, GLA_VT), BF16), st_shape, st_shape],
        scratch_shapes=[
            pltpu.VMEM((2, seq, GLA_DK), BF16),
            pltpu.VMEM((2, seq, GLA_DK), BF16),
            pltpu.VMEM((2, seq, GLA_DK), BF16),
            pltpu.VMEM((2, seq // GLA_CHUNK, 1, GLA_DK), F32),
            pltpu.VMEM((seq, GLA_DV), F32),
            pltpu.VMEM((seq, GLA_DV), F32),
            pltpu.VMEM((2, seq // GLA_CHUNK, GLA_CHUNK, GLA_CHUNK), BF16),
            pltpu.VMEM((2, seq // GLA_CHUNK, GLA_DV, GLA_DK), BF16),
        ],
        compiler_params=_params(("parallel", "parallel")),
        name="gla",
    )(proj, proj, proj, proj, glow, w2f, w2b, gbf, gbb, gla_nw, s0f, s0b)


def _fnet_kernel(u_ref, cl_ref, sl_ref, cg_ref, sg_ref, o_ref, *, scale):
    u = u_ref[...]
    p = _mm(cl_ref[...], u).astype(BF16)
    q = _mm(sl_ref[...], u).astype(BF16)
    y = _mm(p, cg_ref[...]) - _mm(q, sg_ref[...])
    o_ref[...] = (y * scale).astype(BF16)


def _fnet_call(proj, cl, sl, cg, sg, *, batch, seq):
    t = proj.shape[0]
    tr = min(DFT_TILE, seq)
    nr = seq // tr
    ub = P_FN // FNET_W
    return pl.pallas_call(
        functools.partial(_fnet_kernel, scale=1.0 / math.sqrt(seq * FNET_GW)),
        grid=(nr, batch),
        in_specs=[
            pl.BlockSpec((seq, FNET_W), lambda r, b: (b, ub)),
            pl.BlockSpec((tr, seq), lambda r, b: (r, 0)),
            pl.BlockSpec((tr, seq), lambda r, b: (r, 0)),
            pl.BlockSpec((FNET_W, FNET_W), lambda r, b: (0, 0)),
            pl.BlockSpec((FNET_W, FNET_W), lambda r, b: (0, 0)),
        ],
        out_specs=pl.BlockSpec((tr, FNET_W), lambda r, b: (b * nr + r, 0)),
        out_shape=jax.ShapeDtypeStruct((t, FNET_W), BF16),
        compiler_params=_params(("arbitrary", "arbitrary")),
        name="fnet",
    )(proj, cl, sl, cg, sg)


def _hy_posenc_kernel(c_ref, s_ref, *, seq):
    ti = lax.broadcasted_iota(jnp.int32, (seq, LANES), 0).astype(F32)
    lane = lax.broadcasted_iota(jnp.int32, (seq, LANES), 1)
    band_step = (HY_BANDS - 1 - 1e-4) / (HY_BANDS - 1)
    bands = jnp.where(lane < HY_BANDS, 1e-4 + lane.astype(F32) * band_step, 0.0)
    ang = 2.0 * math.pi * ti * bands / seq
    c_ref[...] = jnp.cos(ang)
    s_ref[...] = -jnp.sin(ang)


def _hy_posenc_call(seq):
    shape = jax.ShapeDtypeStruct((seq, LANES), F32)
    return pl.pallas_call(
        functools.partial(_hy_posenc_kernel, seq=seq),
        out_shape=[shape, shape],
        compiler_params=pltpu.CompilerParams(vmem_limit_bytes=VMEM_LIMIT_BYTES),
        name="hy_posenc",
    )()


def _hy_filter_kernel(zc_ref, zs_ref, w1t_ref, w1c_ref, w1s_ref, b1_ref, w2_ref, b2_ref, w3_ref, fr_ref, o_ref,
                      h2_ref, *, seq):
    t_col = lax.broadcasted_iota(jnp.int32, (seq, 1), 0).astype(F32) / (seq - 1.0)

    @pl.when(pl.program_id(0) == 0)
    def _():
        fr = fr_ref[...]
        pre1 = t_col * w1t_ref[...] + _mm3(zc_ref[...], w1c_ref[...]) + _mm3(zs_ref[...], w1s_ref[...]) + b1_ref[...]
        h1 = jnp.sin(fr * pre1)
        h2_ref[...] = jnp.sin(fr * (_mm3(h1, w2_ref[...]) + b2_ref[...]))

    h2 = h2_ref[...]
    ch = lax.broadcasted_iota(jnp.int32, (1, HY_W), 1).astype(F32)
    d0 = math.log(HY_TARGET) / HY_FAST
    d1 = math.log(HY_TARGET) / HY_SLOW
    deltas = jnp.abs(d0 + ch * ((d1 - d0) / (HY_W - 1)))
    win = jnp.exp(-t_col * deltas)
    ss = jnp.zeros((1, HY_W), F32)
    for lo in (0, HY_W):
        hd = _mm3(h2, w3_ref[:, lo:lo + HY_W]) * win
        o_ref[:, lo:lo + HY_W] = hd
        ss = ss + jnp.sum(hd * hd, axis=0, keepdims=True)
    inv = lax.rsqrt(ss + EPS)
    for lo in (0, HY_W):
        o_ref[:, lo:lo + HY_W] = o_ref[:, lo:lo + HY_W] * inv


def _hy_filter_call(zc, zs, w1t, w1c, w1s, b1, w2, b2, w3, fr, *, seq):
    hid = HY_HID
    full = lambda shape: pl.BlockSpec(shape, lambda o: (0,) * len(shape))
    return pl.pallas_call(
        functools.partial(_hy_filter_kernel, seq=seq),
        grid=(2,),
        in_specs=[
            full((seq, LANES)), full((seq, LANES)),
            full((1, hid)), full((LANES, hid)), full((LANES, hid)), full((1, hid)),
            full((hid, hid)), full((1, hid)),
            pl.BlockSpec((hid, 2 * HY_W), lambda o: (0, o)),
            full((1, hid)),
        ],
        out_specs=pl.BlockSpec((seq, 2 * HY_W), lambda o: (0, o)),
        out_shape=jax.ShapeDtypeStruct((seq, 4 * HY_W), F32),
        scratch_shapes=[pltpu.VMEM((seq, hid), F32)],
        compiler_params=_params(("arbitrary",)),
        name="hy_filter",
    )(zc, zs, w1t, w1c, w1s, b1, w2, b2, w3, fr)


def _hy_spec_kernel(h_ref, c_ref, s_ref, rc_ref, rs_ref, o_ref, hb_ref, *, seq, tf):
    f = pl.program_id(1)

    @pl.when(f == 0)
    def _():
        hb_ref[...] = h_ref[...].astype(BF16)

    cm = c_ref[...]
    sm = s_ref[...]
    hfw = hb_ref[:, :HY_W]
    hbw = hb_ref[:, HY_W:]
    pf = _mm(cm, hfw)
    pb = _mm(cm, hbw)
    qf = _mm(sm, hfw)
    qb = _mm(sm, hbw)
    reps = HY_W // LANES
    cf = jnp.concatenate([rc_ref[...]] * reps, axis=1)
    sf = jnp.concatenate([rs_ref[...]] * reps, axis=1)
    kre = pf + cf * pb - sf * qb
    kim = cf * qb + sf * pb - qf
    row0 = (lax.broadcasted_iota(jnp.int32, (tf, HY_W), 0) + f * tf) == 0
    wgt = jnp.where(row0, 0.5 / seq, 1.0 / seq)
    k1 = kre * wgt
    o_ref[0] = k1
    o_ref[1] = jnp.where(row0, 0.0, kim * wgt)
    o_ref[2] = jnp.where(row0, (qf - qb) * (0.5 / seq), k1)


def _hy_spec_call(h, cmat, smat, rot_c, rot_s, *, seq):
    tf = min(DFT_TILE, seq)
    nf = seq // tf
    return pl.pallas_call(
        functools.partial(_hy_spec_kernel, seq=seq, tf=tf),
        grid=(2, nf),
        in_specs=[
            pl.BlockSpec((seq, 2 * HY_W), lambda o, f: (0, o)),
            pl.BlockSpec((tf, seq), lambda o, f: (f, 0)),
            pl.BlockSpec((tf, seq), lambda o, f: (f, 0)),
            pl.BlockSpec((tf, LANES), lambda o, f: (f, 0)),
            pl.BlockSpec((tf, LANES), lambda o, f: (f, 0)),
        ],
        out_specs=pl.BlockSpec((None, 3, tf, HY_W), lambda o, f: (o, 0, f, 0)),
        out_shape=jax.ShapeDtypeStruct((2, 3, seq, HY_W), F32),
        scratch_shapes=[pltpu.VMEM((seq, 2 * HY_W), BF16)],
        compiler_params=_params(("arbitrary", "arbitrary")),
        name="hy_spectrum",
    )(h, cmat, smat, rot_c, rot_s)


def _short_conv(u, w, b, seg):
    n = u.shape[0]
    pos = lax.broadcasted_iota(jnp.int32, u.shape, 0) % seg
    prev = jnp.where(pos == 0, 0.0, pltpu.roll(u, 1, axis=0))
    nxt = jnp.where(pos == seg - 1, 0.0, pltpu.roll(u, n - 1, axis=0))
    return prev * w[0:1, :] + u * w[1:2, :] + nxt * w[2:3, :] + b


def _hy_conv_kernel(u_ref, m_ref, bias_ref, k_ref, cr_ref, sr_ref, cc_ref, sc_ref, o_ref, ub_ref, acc_ref,
                    *, nf, batch, seq):
    f = pl.program_id(1)
    tc = o_ref.shape[1]

    @pl.when(f == 0)
    def _():
        for b in range(batch):
            ub_ref[:, b * tc:(b + 1) * tc] = u_ref[b * seq:(b + 1) * seq, :]
        acc_ref[...] = jnp.zeros(acc_ref.shape, F32)

    ub = ub_ref[...]
    p = _mm(cr_ref[...], ub)
    q = _mm(sr_ref[...], ub)
    k1, k2, k3 = (jnp.concatenate([k_ref[j]] * batch, axis=1) for j in range(3))
    av = (p * k1 + q * k2).astype(BF16)
    bv = (q * k3 - p * k2).astype(BF16)
    acc_ref[...] += _mm(cc_ref[...], av) + _mm(sc_ref[...], bv)

    @pl.when(f == nf - 1)
    def _():
        for b in range(batch):
            rows = slice(b * seq, (b + 1) * seq)
            y = acc_ref[:, b * tc:(b + 1) * tc] + bias_ref[...] * u_ref[rows, :].astype(F32)
            o_ref[rows, :] = (m_ref[rows, :].astype(F32) * y).astype(o_ref.dtype)


def _hy_conv_call(u_arr, u_col, m_arr, m_col, bias, kspec, order, cmat, smat, smat_inv, *, batch, seq):
    t = u_arr.shape[0]
    tc = CONV_TC
    nch = HY_W // tc
    tf = min(CONV_TF, seq)
    nf = seq // tf
    ucb, mcb = u_col // tc, m_col // tc
    once = pl.Buffered(1)
    return pl.pallas_call(
        functools.partial(_hy_conv_kernel, nf=nf, batch=batch, seq=seq),
        grid=(nch, nf),
        in_specs=[
            pl.BlockSpec((t, tc), lambda ch, f: (0, ucb + ch), pipeline_mode=once),
            pl.BlockSpec((t, tc), lambda ch, f: (0, mcb + ch), pipeline_mode=once),
            pl.BlockSpec((1, tc), lambda ch, f: (0, ch)),
            pl.BlockSpec((None, 3, tf, tc), lambda ch, f: (order, 0, f, ch)),
            pl.BlockSpec((tf, seq), lambda ch, f: (f, 0)),
            pl.BlockSpec((tf, seq), lambda ch, f: (f, 0)),
            pl.BlockSpec((seq, tf), lambda ch, f: (0, f)),
            pl.BlockSpec((seq, tf), lambda ch, f: (0, f)),
        ],
        out_specs=pl.BlockSpec((t, tc), lambda ch, f: (0, ch)),
        out_shape=jax.ShapeDtypeStruct((t, HY_W), BF16),
        scratch_shapes=[pltpu.VMEM((seq, batch * tc), BF16), pltpu.VMEM((seq, batch * tc), F32)],
        compiler_params=_params(("arbitrary", "arbitrary")),
        name="hy_conv",
    )(u_arr, m_arr, bias, kspec, cmat, smat, cmat, smat_inv)


MERGE_CHUNK = 512


def _merge_kernel(x_ref, mod_ref, nw_ref, yg_ref, yf_ref, yh_ref, g0_ref, g1_ref, g2_ref, wg_ref, wf_ref, wh_ref,
                  wo_ref, o_ref, xn_ref, mg_ref):
    d = o_ref.shape[1]
    yg = yg_ref[...]
    yf = yf_ref[...]
    yh = yh_ref[...]
    for c0 in range(0, d, MERGE_CHUNK):
        cols = slice(c0, c0 + MERGE_CHUNK)
        m = g0_ref[:, cols].astype(F32) * _mm(yg, wg_ref[:, cols])
        m = m + g1_ref[:, cols].astype(F32) * _mm(yf, wf_ref[:, cols])
        m = m + g2_ref[:, cols].astype(F32) * _mm(yh, wh_ref[:, cols])
        mg_ref[:, cols] = m.astype(BF16)
    mg = mg_ref[...]
    for c0 in range(0, d, MERGE_CHUNK):
        cols = slice(c0, c0 + MERGE_CHUNK)
        o_ref[:, cols] = x_ref[:, cols] + mod_ref[5:6, cols] * _mm(mg, wo_ref[:, cols])
    xn_ref[...] = _modnorm(o_ref[...], nw_ref[2:3, :], mod_ref[6:7, :], mod_ref[7:8, :]).astype(BF16)


def _merge_call(xs, mod, norm_w, proj, y_gla, y_fnet, y_hy, wbg, wbf, wbh, wo, *, layer, rows_per_mod, fixed_row):
    t, d = xs.shape
    tm = MERGE_TM
    row = _mod_row_map(rows_per_mod, tm, fixed_row)
    gate = lambda j: pl.BlockSpec((pl.Element(tm), pl.Element(d)), lambda i: (i * tm, P_GATE + j * d))
    weight = lambda k: pl.BlockSpec((None, k, d), lambda i: (layer, 0, 0), pipeline_mode=pl.Buffered(1))
    return pl.pallas_call(
        _merge_kernel,
        grid=(t // tm,),
        in_specs=[
            pl.BlockSpec((tm, d), lambda i: (i, 0)),
            pl.BlockSpec((None, None, N_ADA, d), lambda i: (layer, row(i), 0, 0)),
            pl.BlockSpec((None, 3, d), lambda i: (layer, 0, 0)),
            pl.BlockSpec((tm, GLA_VT), lambda i: (i, 0)),
            pl.BlockSpec((tm, FNET_W), lambda i: (i, 0)),
            pl.BlockSpec((tm, HY_W), lambda i: (i, 0)),
            gate(0),
            gate(1),
            gate(2),
            weight(GLA_VT),
            weight(FNET_W),
            weight(HY_W),
            weight(d),
        ],
        out_specs=[pl.BlockSpec((tm, d), lambda i: (i, 0)), pl.BlockSpec((tm, d), lambda i: (i, 0))],
        out_shape=[jax.ShapeDtypeStruct((t, d), F32), jax.ShapeDtypeStruct((t, d), BF16)],
        scratch_shapes=[pltpu.VMEM((tm, d), BF16)],
        compiler_params=_params(("parallel",)),
        name="merge_out",
    )(xs, mod, norm_w, y_gla, y_fnet, y_hy, proj, proj, proj, wbg, wbf, wbh, wo)


def _trig_tables(n, period):
    split = 32
    c = lax.broadcasted_iota(jnp.int32, (1, n), 1)

    def rows(r):
        ang = ((r * c) % period).astype(F32) * (2.0 * math.pi / period)
        return jnp.cos(ang), jnp.sin(ang)

    c_lo, s_lo = rows(lax.broadcasted_iota(jnp.int32, (split, 1), 0))
    c_hi, s_hi = rows(lax.broadcasted_iota(jnp.int32, (n // split, 1), 0) * split)
    cos = c_hi[:, None, :] * c_lo[None] - s_hi[:, None, :] * s_lo[None]
    sin = s_hi[:, None, :] * c_lo[None] + c_hi[:, None, :] * s_lo[None]
    return cos.reshape(n, n), sin.reshape(n, n)


def _fnet_tables(seq):
    cl, sl = _trig_tables(seq, seq)
    cg, sg = _trig_tables(FNET_GW, FNET_GW)
    eye = jnp.eye(FNET_GROUPS, dtype=F32)
    return cl.astype(BF16), sl.astype(BF16), jnp.kron(eye, cg).astype(BF16), jnp.kron(eye, sg).astype(BF16)


def _hyena_tables(seq):
    cm, sm = _trig_tables(seq, 2 * seq)
    r = lax.broadcasted_iota(jnp.int32, (seq, seq), 0)
    c = lax.broadcasted_iota(jnp.int32, (seq, seq), 1)
    sm_fwd = jnp.where(r == 0, (1 - 2 * (c % 2)).astype(F32), sm)
    sm_inv = jnp.where(c == 0, (1 - 2 * (r % 2)).astype(F32), sm)
    fr = lax.broadcasted_iota(jnp.int32, (seq, LANES), 0).astype(F32) * (math.pi / seq)
    return cm.astype(BF16), sm_fwd.astype(BF16), sm_inv.astype(BF16), jnp.cos(fr), jnp.sin(fr)


def _proj_conv_taps(hy_conv_w, hy_conv_b):
    depth = hy_conv_w.shape[0]
    hy = slice(P_HY, P_HY + 3 * HY_W)
    taps = jnp.zeros((depth, 3, P_TOTAL), F32).at[:, 1, :].set(1.0).at[:, :, hy].set(hy_conv_w)
    bias = jnp.zeros((depth, 1, P_TOTAL), F32).at[:, 0, hy].set(hy_conv_b)
    return taps, bias


def _gla_gate_weights(gla_w2, gla_gb):
    depth = gla_w2.shape[0]
    w = gla_w2.reshape(depth, 2, GLA_RANK, GLA_HEADS, GLA_DK).transpose(0, 1, 3, 2, 4)
    zf = jnp.zeros((depth, GLA_HEADS, LANES, GLA_DK), F32)
    w2f = zf.at[:, :, :GLA_RANK].set(w[:, 0]).astype(BF16)
    w2b = zf.at[:, :, GLA_RANK:2 * GLA_RANK].set(w[:, 1]).astype(BF16)
    gb = gla_gb.reshape(depth, 2, GLA_HEADS, 1, GLA_DK)
    return w2f, w2b, gb[:, 0], gb[:, 1]


def kernel(x, c, ctx, c_ctx, ada_w, ada_b, norm_w, ffn1_wi, ffn1_wo, ffn2_wi, ffn2_wo, w_in, gla_w2, gla_gb,
           gla_norm_w, hy_conv_w, hy_conv_b, hy_f1_w, hy_f1_b, hy_f2_w, hy_f2_b, hy_f3_w, hy_freq, hy_bias,
           w_br_gla, w_br_fnet, w_br_hy, w_o, final_norm_w):
    batch, seq, d = x.shape
    ctx_len = ctx.shape[1]
    xs = x.reshape(batch * seq, d)
    cs = ctx.reshape(batch * ctx_len, d)

    cond = jnp.concatenate([c, c_ctx[None, :], jnp.zeros((8 - batch - 1, d), F32)], axis=0)
    mod = _ada_call(cond, ada_w, ada_b)
    ctx_row = batch

    w_t = jnp.swapaxes(w_in, 1, 2)
    wbg, wbf, wbh, wob = (w.astype(BF16) for w in (w_br_gla, w_br_fnet, w_br_hy, w_o))
    w2f, w2b, gbf, gbb = _gla_gate_weights(gla_w2, gla_gb)
    conv_w, conv_b = _proj_conv_taps(hy_conv_w, hy_conv_b)
    gla_nw = gla_norm_w.reshape(DEPTH, GLA_HEADS, 1, GLA_DV)

    fnet_tab = {n: _fnet_tables(n) for n in (seq, ctx_len)}
    hy_tab = {n: _hyena_tables(n) for n in (seq, ctx_len)}
    hy_pos = {n: _hy_posenc_call(n) for n in (seq, ctx_len)}

    def mixers(proj, glow, layer, n, s0f, s0b, need_y):
        y_gla, sf, sb = _gla_call(proj, glow, w2f[layer], w2b[layer], gbf[layer], gbb[layer], gla_nw[layer],
                                  s0f, s0b, batch=batch, seq=n)
        if not need_y:
            return None, sf, sb
        y_fnet = _fnet_call(proj, *fnet_tab[n], batch=batch, seq=n)
        cm, sm, sm_inv, rot_c, rot_s = hy_tab[n]
        w1 = hy_f1_w[layer]
        zpad = jnp.zeros((LANES - HY_BANDS, HY_HID), F32)
        h = _hy_filter_call(*hy_pos[n], w1[0:1], jnp.concatenate([w1[1:1 + HY_BANDS], zpad], 0),
                            jnp.concatenate([w1[1 + HY_BANDS:], zpad], 0), hy_f1_b[layer][None, :],
                            hy_f2_w[layer], hy_f2_b[layer][None, :], hy_f3_w[layer], hy_freq[layer][None, :], seq=n)
        kspec = _hy_spec_call(h, cm, sm, rot_c, rot_s, seq=n)
        z = _hy_conv_call(proj, P_HY, proj, P_HY + HY_W, hy_bias[layer, 0][None, :], kspec, 0, cm, sm, sm_inv,
                          batch=batch, seq=n)
        y_hy = _hy_conv_call(z, 0, proj, P_HY + 2 * HY_W, hy_bias[layer, 1][None, :], kspec, 1, cm, sm, sm_inv,
                             batch=batch, seq=n)
        return (y_gla, y_fnet, y_hy), sf, sb

    s_zero = jnp.zeros((batch, GLA_HEADS, GLA_DV, GLA_DK), F32)
    for layer in range(DEPTH):
        last = layer == DEPTH - 1
        x_kw = dict(layer=layer, rows_per_mod=seq, fixed_row=None)
        c_kw = dict(layer=layer, rows_per_mod=ctx_len, fixed_row=ctx_row)
        xs = _ffn(xs, _norm_call(xs, mod, norm_w, sub=0, **x_kw), mod, ffn1_wi, ffn1_wo, sub=0, **x_kw)
        cs = _ffn(cs, _norm_call(cs, mod, norm_w, sub=0, **c_kw), mod, ffn1_wi, ffn1_wo, sub=0, **c_kw)
        cn, glow_c = _mixnorm_call(cs, mod, norm_w, w_t, **c_kw)
        proj_c = _proj_call(cn, w_t, conv_w[layer], conv_b[layer], layer=layer, seg=ctx_len)
        ys_c, sf, sb = mixers(proj_c, glow_c, layer, ctx_len, s_zero, s_zero, not last)
        if not last:
            cs, cn2 = _merge_call(cs, mod, norm_w, proj_c, *ys_c, wbg, wbf, wbh, wob, **c_kw)
            cs = _ffn(cs, cn2, mod, ffn2_wi, ffn2_wo, sub=2, **c_kw)
        xn, glow_x = _mixnorm_call(xs, mod, norm_w, w_t, **x_kw)
        proj_x = _proj_call(xn, w_t, conv_w[layer], conv_b[layer], layer=layer, seg=GRID_W)
        ys_x, _, _ = mixers(proj_x, glow_x, layer, seq, sf, sb, True)
        xs, xn2 = _merge_call(xs, mod, norm_w, proj_x, *ys_x, wbg, wbf, wbh, wob, **x_kw)
        xs = _ffn(xs, xn2, mod, ffn2_wi, ffn2_wo, sub=2, **x_kw)
    return _final_norm_call(xs, final_norm_w.reshape(1, d)).reshape(batch, seq, d)
```

```python
import functools
import math

import jax
import jax.numpy as jnp
from jax import lax
from jax.experimental import pallas as pl
from jax.experimental.pallas import tpu as pltpu

F32 = jnp.float32
BF16 = jnp.bfloat16

D_MODEL = 2048
DEPTH = 4
GRID_W = 64
N_ADA = 9
D_FF = 5504
GLA_HEADS = 4
GLA_DK = 128
GLA_DV = 256
GLA_KT = GLA_HEADS * GLA_DK
GLA_VT = GLA_HEADS * GLA_DV
GLA_RANK = 16
GLA_TAU = 16.0
GLA_CHUNK = 64
FNET_GROUPS = 4
FNET_GW = 128
FNET_W = FNET_GROUPS * FNET_GW
HY_W = 512
HY_BANDS = 16
HY_HID = 64
HY_FAST = 0.3
HY_SLOW = 1.5
HY_TARGET = 1e-2
EPS = 1e-6

LANES = 128
VMEM_LIMIT_BYTES = 56 * 1024 * 1024

ADA_TN = 1024
NORM_TM = 1024
WS_TM = 2048
DOWN_TM = 512
DOWN_TN = 512
MERGE_TM = 512
DFT_TILE = 512
CONV_TC = 256
CONV_TF = 256
GLA_UNROLL = 8

FF_TILE = 512
FF_STEPS = -(-D_FF // FF_TILE)

W_GLOW = 2 * GLA_KT + GLA_VT
W_IN_COLS = W_GLOW + 2 * GLA_RANK + GLA_VT + FNET_W + 3 * HY_W + 3 * D_MODEL
P_TILE = 1024
P_Q = 0
P_K = GLA_KT
P_V = 2 * GLA_KT
P_R = W_GLOW
P_FN = P_R + GLA_VT
P_HY = P_FN + FNET_W
P_GATE = P_HY + 3 * HY_W
P_TOTAL = P_GATE + 3 * D_MODEL


def _mm(a, b):
    return jnp.dot(a, b, preferred_element_type=F32)


def _mm_nt(a, b):
    return lax.dot_general(a, b, (((1,), (1,)), ((), ())), preferred_element_type=F32)


def _mm_tn(a, b):
    return lax.dot_general(a, b, (((0,), (0,)), ((), ())), preferred_element_type=F32)


def _split2(a):
    hi = a.astype(BF16)
    lo = (a - hi.astype(F32)).astype(BF16)
    return hi, lo


def _mm3(a, b):
    ah, al = _split2(a)
    bh, bl = _split2(b)
    return _mm(ah, bh) + (_mm(ah, bl) + _mm(al, bh))


def _sigmoid(x):
    return 0.5 * jnp.tanh(0.5 * x) + 0.5


def _silu(x):
    return x * _sigmoid(x)


def _params(sem):
    return pltpu.CompilerParams(dimension_semantics=sem, vmem_limit_bytes=VMEM_LIMIT_BYTES)


def _modnorm(x, nw, shift, scale):
    ms = jnp.mean(x * x, axis=-1, keepdims=True)
    y = x * lax.rsqrt(ms + EPS) * nw
    return y * (1.0 + scale) + shift


def _ada_kernel(c_ref, w_ref, b_ref, o_ref):
    a = _silu(c_ref[...]).astype(BF16)
    o_ref[...] = _mm(a, w_ref[...].astype(BF16)) + b_ref[...]


def _ada_call(cond, ada_w, ada_b):
    depth, d, n = ada_w.shape
    rows = cond.shape[0]
    tn = ADA_TN
    out = pl.pallas_call(
        _ada_kernel,
        grid=(depth, n // tn),
        in_specs=[
            pl.BlockSpec((rows, d), lambda l, j: (0, 0)),
            pl.BlockSpec((None, d, tn), lambda l, j: (l, 0, j)),
            pl.BlockSpec((None, 1, tn), lambda l, j: (l, 0, j)),
        ],
        out_specs=pl.BlockSpec((None, rows, tn), lambda l, j: (l, 0, j)),
        out_shape=jax.ShapeDtypeStruct((depth, rows, n), F32),
        compiler_params=_params(("arbitrary", "arbitrary")),
        name="ada_mod",
    )(cond, ada_w, ada_b.reshape(depth, 1, n))
    return out.reshape(depth, rows, N_ADA, d)


def _mod_row_map(rows_per_mod, tm, fixed_row):
    if fixed_row is not None:
        return lambda i: fixed_row
    per = rows_per_mod // tm
    return lambda i: i // per


def _norm_kernel(x_ref, mod_ref, nw_ref, o_ref, *, sub):
    xn = _modnorm(x_ref[...], nw_ref[sub:sub + 1, :], mod_ref[3 * sub:3 * sub + 1, :],
                  mod_ref[3 * sub + 1:3 * sub + 2, :])
    o_ref[...] = xn.astype(BF16)


def _norm_call(xs, mod, norm_w, *, layer, sub, rows_per_mod, fixed_row):
    t, d = xs.shape
    tm = NORM_TM
    row = _mod_row_map(rows_per_mod, tm, fixed_row)
    return pl.pallas_call(
        functools.partial(_norm_kernel, sub=sub),
        grid=(t // tm,),
        in_specs=[
            pl.BlockSpec((tm, d), lambda i: (i, 0)),
            pl.BlockSpec((None, None, N_ADA, d), lambda i: (layer, row(i), 0, 0)),
            pl.BlockSpec((None, 3, d), lambda i: (layer, 0, 0)),
        ],
        out_specs=pl.BlockSpec((tm, d), lambda i: (i, 0)),
        out_shape=jax.ShapeDtypeStruct((t, d), BF16),
        compiler_params=_params(("parallel",)),
        name="ffn_norm",
    )(xs, mod, norm_w)


def _ffn_up_kernel(xn_ref, wa_ref, wg_ref, h_ref, wab_ref, wgb_ref):
    @pl.when(pl.program_id(1) == 0)
    def _():
        wab_ref[...] = wa_ref[...].astype(BF16)
        wgb_ref[...] = wg_ref[...].astype(BF16)

    xn = xn_ref[...]
    a = _mm(xn, wab_ref[...])
    g = _mm(xn, wgb_ref[...])
    h_ref[...] = (_silu(g) * a).astype(BF16)


def _ffn_up_call(xn, wi, *, layer):
    t, d = xn.shape
    tm = min(WS_TM, t)
    back = (FF_STEPS * FF_TILE - D_FF) // LANES
    hid_blk = lambda f: f * (FF_TILE // LANES) - (f // (FF_STEPS - 1)) * back
    hid0 = lambda f: hid_blk(f) * LANES
    gate0 = lambda f: (D_FF // LANES + hid_blk(f)) * LANES
    return pl.pallas_call(
        _ffn_up_kernel,
        grid=(FF_STEPS, t // tm),
        in_specs=[
            pl.BlockSpec((tm, d), lambda f, i: (i, 0)),
            pl.BlockSpec((None, pl.Element(d), pl.Element(FF_TILE)), lambda f, i: (layer, 0, hid0(f))),
            pl.BlockSpec((None, pl.Element(d), pl.Element(FF_TILE)), lambda f, i: (layer, 0, gate0(f))),
        ],
        out_specs=pl.BlockSpec((pl.Element(tm), pl.Element(FF_TILE)), lambda f, i: (i * tm, hid0(f))),
        out_shape=jax.ShapeDtypeStruct((t, D_FF), BF16),
        scratch_shapes=[pltpu.VMEM((d, FF_TILE), BF16), pltpu.VMEM((d, FF_TILE), BF16)],
        compiler_params=_params(("arbitrary", "arbitrary")),
        name="ffn_up",
    )(xn, wi, wi)


def _ffn_down_kernel(h_ref, w_ref, x_ref, mod_ref, o_ref, wb_ref, *, sub):
    @pl.when(pl.program_id(1) == 0)
    def _():
        wb_ref[...] = w_ref[...].astype(BF16)

    y = _mm(h_ref[...], wb_ref[...])
    o_ref[...] = x_ref[...] + 0.5 * mod_ref[3 * sub + 2:3 * sub + 3, :] * y


def _ffn_down_call(h, wo, xs, mod, *, layer, sub, rows_per_mod, fixed_row):
    t, d = xs.shape
    tm = DOWN_TM
    tn = DOWN_TN
    row = _mod_row_map(rows_per_mod, tm, fixed_row)
    return pl.pallas_call(
        functools.partial(_ffn_down_kernel, sub=sub),
        grid=(d // tn, t // tm),
        in_specs=[
            pl.BlockSpec((tm, D_FF), lambda n, i: (i, 0)),
            pl.BlockSpec((None, D_FF, tn), lambda n, i: (layer, 0, n)),
            pl.BlockSpec((tm, tn), lambda n, i: (i, n)),
            pl.BlockSpec((None, None, N_ADA, tn), lambda n, i: (layer, row(i), 0, n)),
        ],
        out_specs=pl.BlockSpec((tm, tn), lambda n, i: (i, n)),
        out_shape=jax.ShapeDtypeStruct((t, d), F32),
        scratch_shapes=[pltpu.VMEM((D_FF, tn), BF16)],
        compiler_params=_params(("arbitrary", "arbitrary")),
        name="ffn_down",
    )(h, wo, xs, mod)


def _ffn(xs, xn, mod, wi, wo, *, layer, sub, rows_per_mod, fixed_row):
    h = _ffn_up_call(xn, wi, layer=layer)
    return _ffn_down_call(h, wo, xs, mod, layer=layer, sub=sub, rows_per_mod=rows_per_mod, fixed_row=fixed_row)


def _final_norm_kernel(x_ref, w_ref, o_ref):
    x = x_ref[...]
    ms = jnp.mean(x * x, axis=-1, keepdims=True)
    o_ref[...] = x * lax.rsqrt(ms + EPS) * w_ref[...]


def _final_norm_call(xs, w):
    t, d = xs.shape
    tm = NORM_TM
    return pl.pallas_call(
        _final_norm_kernel,
        grid=(t // tm,),
        in_specs=[pl.BlockSpec((tm, d), lambda i: (i, 0)), pl.BlockSpec((1, d), lambda i: (0, 0))],
        out_specs=pl.BlockSpec((tm, d), lambda i: (i, 0)),
        out_shape=jax.ShapeDtypeStruct((t, d), F32),
        compiler_params=_params(("parallel",)),
        name="final_norm",
    )(xs, w)


def _mixnorm_kernel(x_ref, mod_ref, nw_ref, wl_ref, xn_ref, gl_ref):
    xn = _modnorm(x_ref[...], nw_ref[1:2, :], mod_ref[3:4, :], mod_ref[4:5, :]).astype(BF16)
    xn_ref[...] = xn
    gl_ref[...] = _mm_nt(xn, wl_ref[...].astype(BF16)).astype(BF16)


def _mixnorm_call(xs, mod, norm_w, w_t, *, layer, rows_per_mod, fixed_row):
    t, d = xs.shape
    tm = NORM_TM
    row = _mod_row_map(rows_per_mod, tm, fixed_row)
    return pl.pallas_call(
        _mixnorm_kernel,
        grid=(t // tm,),
        in_specs=[
            pl.BlockSpec((tm, d), lambda i: (i, 0)),
            pl.BlockSpec((None, None, N_ADA, d), lambda i: (layer, row(i), 0, 0)),
            pl.BlockSpec((None, 3, d), lambda i: (layer, 0, 0)),
            pl.BlockSpec((None, LANES, d), lambda i: (layer, W_GLOW // LANES, 0)),
        ],
        out_specs=[pl.BlockSpec((tm, d), lambda i: (i, 0)), pl.BlockSpec((tm, LANES), lambda i: (i, 0))],
        out_shape=[jax.ShapeDtypeStruct((t, d), BF16), jax.ShapeDtypeStruct((t, LANES), BF16)],
        compiler_params=_params(("parallel",)),
        name="mix_norm",
    )(xs, mod, norm_w, w_t)


P_TILES = P_TOTAL // P_TILE
P_SKIP = 2 * GLA_RANK
P_GATE_TILE = P_GATE // P_TILE


def _proj_row0(n):
    sublanes = 8
    first = W_GLOW // P_TILE
    after = (n + (P_TILES - first)) // P_TILES
    return (n * (P_TILE // sublanes) + after * (P_SKIP // sublanes)) * sublanes


def _proj_kernel(xn_ref, w_ref, o_ref, wb_ref):
    @pl.when(pl.program_id(1) == 0)
    def _():
        wb_ref[...] = w_ref[...].astype(BF16)

    @pl.when(pl.program_id(0) < P_GATE_TILE)
    def _():
        o_ref[...] = _mm_nt(xn_ref[...], wb_ref[...]).astype(BF16)

    @pl.when(pl.program_id(0) >= P_GATE_TILE)
    def _():
        o_ref[...] = _sigmoid(_mm_nt(xn_ref[...], wb_ref[...])).astype(BF16)


def _proj_call(xn, w_t, *, layer):
    t, d = xn.shape
    tm = min(WS_TM, t)
    return pl.pallas_call(
        _proj_kernel,
        grid=(P_TILES, t // tm),
        in_specs=[
            pl.BlockSpec((tm, d), lambda n, i: (i, 0)),
            pl.BlockSpec((None, pl.Element(P_TILE), pl.Element(d)), lambda n, i: (layer, _proj_row0(n), 0)),
        ],
        out_specs=pl.BlockSpec((tm, P_TILE), lambda n, i: (i, n)),
        out_shape=jax.ShapeDtypeStruct((t, P_TOTAL), BF16),
        scratch_shapes=[pltpu.VMEM((P_TILE, d), BF16)],
        compiler_params=_params(("arbitrary", "arbitrary")),
        name="mix_proj",
    )(xn, w_t)


def _log_sigmoid(x):
    return jnp.minimum(x, 0.0) - jnp.log(1.0 + jnp.exp(-jnp.abs(x)))


def _chunk_scan(x, reverse):
    n, w = x.shape
    sub = 8
    tiles = GLA_CHUNK // sub
    x3 = x.reshape(n // sub, sub, w)
    pos = lax.broadcasted_iota(jnp.int32, x3.shape, 1)
    s = 1
    while s < sub:
        if reverse:
            x3 = x3 + jnp.where(pos < sub - s, pltpu.roll(x3, sub - s, axis=1), 0.0)
        else:
            x3 = x3 + jnp.where(pos >= s, pltpu.roll(x3, s, axis=1), 0.0)
        s *= 2
    tot = jnp.broadcast_to(x3[:, 0:1, :] if reverse else x3[:, sub - 1:sub, :], x3.shape)
    tpos = lax.broadcasted_iota(jnp.int32, x3.shape, 0) % tiles
    acc = tot
    s = 1
    while s < tiles:
        pad = jnp.zeros((s, sub, w), F32)
        if reverse:
            acc = acc + jnp.where(tpos < tiles - s, jnp.concatenate([acc[s:], pad], axis=0), 0.0)
        else:
            acc = acc + jnp.where(tpos >= s, jnp.concatenate([pad, acc[:-s]], axis=0), 0.0)
        s *= 2
    return (x3 + (acc - tot)).reshape(n, w)


def _gla_kernel(q_ref, k_ref, v_ref, r_ref, gl_ref, w2f_ref, w2b_ref, gbf_ref, gbb_ref, nw_ref,
                s0f_ref, s0b_ref, y_ref, sf_ref, sb_ref, qd_ref, kd_ref, ke_ref, dec_ref, of_ref, ob_ref,
                att_ref, sp_ref, *, seq):
    c = GLA_CHUNK
    n_chunks = seq // c
    glow = gl_ref[...]
    q = q_ref[...].astype(F32) * GLA_DK ** -0.5
    k = k_ref[...].astype(F32)
    for d, (w2_ref, gb_ref) in enumerate(((w2f_ref, gbf_ref), (w2b_ref, gbb_ref))):
        lg = _log_sigmoid(_mm(glow, w2_ref[...]) + gb_ref[...]) * (1.0 / GLA_TAU)
        gc = _chunk_scan(lg, reverse=d == 1)
        gc3 = gc.reshape(n_chunks, c, GLA_DK)
        gt = gc3[:, 0:1, :] if d == 1 else gc3[:, c - 1:c, :]
        qd_ref[d] = (q * jnp.exp(gc)).astype(BF16)
        kd_ref[d] = (k * jnp.exp(-gc)).astype(BF16)
        ke_ref[d] = (k.reshape(n_chunks, c, GLA_DK) * jnp.exp(gt - gc3)).reshape(seq, GLA_DK).astype(BF16)
        dec_ref[d] = jnp.exp(gt)
    sf_ref[...] = s0f_ref[...]
    sb_ref[...] = s0b_ref[...]

    ri = lax.broadcasted_iota(jnp.int32, (c, c), 0)
    ci = lax.broadcasted_iota(jnp.int32, (c, c), 1)

    unroll = min(GLA_UNROLL, n_chunks)
    dirs = ((0, ri >= ci, sf_ref, of_ref), (1, ri <= ci, sb_ref, ob_ref))

    def local_body(i, carry):
        for d, mask, st_ref, _ in dirs:
            n = i if d == 0 else n_chunks - 1 - i
            r0 = pl.multiple_of(n * c, c)
            scores = _mm_nt(qd_ref[d, pl.ds(r0, c), :], kd_ref[d, pl.ds(r0, c), :])
            att_ref[d, n] = jnp.where(mask, scores, 0.0).astype(BF16)
            st = st_ref[...]
            sp_ref[d, n] = st.astype(BF16)
            st_ref[...] = st * dec_ref[d, n] + _mm_tn(v_ref[pl.ds(r0, c), :], ke_ref[d, pl.ds(r0, c), :])
        return carry

    lax.fori_loop(0, n_chunks, local_body, 0, unroll=unroll)

    def out_body(n, carry):
        r0 = pl.multiple_of(n * c, c)
        v = v_ref[pl.ds(r0, c), :]
        for d, _, _, o_ref in dirs:
            o_ref[pl.ds(r0, c), :] = _mm(att_ref[d, n], v) + _mm_nt(qd_ref[d, pl.ds(r0, c), :], sp_ref[d, n])
        return carry

    lax.fori_loop(0, n_chunks, out_body, 0, unroll=unroll)

    o = of_ref[...] + ob_ref[...]
    ms = jnp.mean(o * o, axis=-1, keepdims=True)
    y = o * lax.rsqrt(ms + EPS) * nw_ref[...] * _silu(r_ref[...].astype(F32))
    y_ref[...] = y.astype(BF16)


def _gla_call(proj, glow, w2f, w2b, gbf, gbb, gla_nw, s0f, s0b, *, batch, seq):
    t = proj.shape[0]
    h = GLA_HEADS
    qb, kb = P_Q // GLA_DK, P_K // GLA_DK
    vb, rb = P_V // GLA_DV, P_R // GLA_DV
    st_spec = pl.BlockSpec((None, None, GLA_DV, GLA_DK), lambda b, hh: (b, hh, 0, 0))
    st_shape = jax.ShapeDtypeStruct((batch, h, GLA_DV, GLA_DK), F32)
    return pl.pallas_call(
        functools.partial(_gla_kernel, seq=seq),
        grid=(batch, h),
        in_specs=[
            pl.BlockSpec((seq, GLA_DK), lambda b, hh: (b, qb + hh)),
            pl.BlockSpec((seq, GLA_DK), lambda b, hh: (b, kb + hh)),
            pl.BlockSpec((seq, GLA_DV), lambda b, hh: (b, vb + hh)),
            pl.BlockSpec((seq, GLA_DV), lambda b, hh: (b, rb + hh)),
            pl.BlockSpec((seq, LANES), lambda b, hh: (b, 0)),
            pl.BlockSpec((None, LANES, GLA_DK), lambda b, hh: (hh, 0, 0)),
            pl.BlockSpec((None, LANES, GLA_DK), lambda b, hh: (hh, 0, 0)),
            pl.BlockSpec((None, 1, GLA_DK), lambda b, hh: (hh, 0, 0)),
            pl.BlockSpec((None, 1, GLA_DK), lambda b, hh: (hh, 0, 0)),
            pl.BlockSpec((None, 1, GLA_DV), lambda b, hh: (hh, 0, 0)),
            st_spec,
            st_spec,
        ],
        out_specs=[pl.BlockSpec((seq, GLA_DV), lambda b, hh: (b, hh)), st_spec, st_spec],
        out_shape=[jax.ShapeDtypeStruct((t, GLA_VT), BF16), st_shape, st_shape],
        scratch_shapes=[
            pltpu.VMEM((2, seq, GLA_DK), BF16),
            pltpu.VMEM((2, seq, GLA_DK), BF16),
            pltpu.VMEM((2, seq, GLA_DK), BF16),
            pltpu.VMEM((2, seq // GLA_CHUNK, 1, GLA_DK), F32),
            pltpu.VMEM((seq, GLA_DV), F32),
            pltpu.VMEM((seq, GLA_DV), F32),
            pltpu.VMEM((2, seq // GLA_CHUNK, GLA_CHUNK, GLA_CHUNK), BF16),
            pltpu.VMEM((2, seq // GLA_CHUNK, GLA_DV, GLA_DK), BF16),
        ],
        compiler_params=_params(("parallel", "parallel")),
        name="gla",
    )(proj, proj, proj, proj, glow, w2f, w2b, gbf, gbb, gla_nw, s0f, s0b)


def _fnet_kernel(u_ref, cl_ref, sl_ref, cg_ref, sg_ref, o_ref, *, scale):
    u = u_ref[...]
    p = _mm(cl_ref[...], u).astype(BF16)
    q = _mm(sl_ref[...], u).astype(BF16)
    y = _mm(p, cg_ref[...]) - _mm(q, sg_ref[...])
    o_ref[...] = (y * scale).astype(BF16)


def _fnet_call(proj, cl, sl, cg, sg, *, batch, seq):
    t = proj.shape[0]
    tr = min(DFT_TILE, seq)
    nr = seq // tr
    ub = P_FN // FNET_W
    return pl.pallas_call(
        functools.partial(_fnet_kernel, scale=1.0 / math.sqrt(seq * FNET_GW)),
        grid=(nr, batch),
        in_specs=[
            pl.BlockSpec((seq, FNET_W), lambda r, b: (b, ub)),
            pl.BlockSpec((tr, seq), lambda r, b: (r, 0)),
            pl.BlockSpec((tr, seq), lambda r, b: (r, 0)),
            pl.BlockSpec((FNET_W, FNET_W), lambda r, b: (0, 0)),
            pl.BlockSpec((FNET_W, FNET_W), lambda r, b: (0, 0)),
        ],
        out_specs=pl.BlockSpec((tr, FNET_W), lambda r, b: (b * nr + r, 0)),
        out_shape=jax.ShapeDtypeStruct((t, FNET_W), BF16),
        compiler_params=_params(("arbitrary", "arbitrary")),
        name="fnet",
    )(proj, cl, sl, cg, sg)


def _hy_posenc_kernel(c_ref, s_ref, *, seq):
    ti = lax.broadcasted_iota(jnp.int32, (seq, LANES), 0).astype(F32)
    lane = lax.broadcasted_iota(jnp.int32, (seq, LANES), 1)
    band_step = (HY_BANDS - 1 - 1e-4) / (HY_BANDS - 1)
    bands = jnp.where(lane < HY_BANDS, 1e-4 + lane.astype(F32) * band_step, 0.0)
    ang = 2.0 * math.pi * ti * bands / seq
    c_ref[...] = jnp.cos(ang)
    s_ref[...] = -jnp.sin(ang)


def _hy_posenc_call(seq):
    shape = jax.ShapeDtypeStruct((seq, LANES), F32)
    return pl.pallas_call(
        functools.partial(_hy_posenc_kernel, seq=seq),
        out_shape=[shape, shape],
        compiler_params=pltpu.CompilerParams(vmem_limit_bytes=VMEM_LIMIT_BYTES),
        name="hy_posenc",
    )()


def _hy_filter_kernel(zc_ref, zs_ref, w1t_ref, w1c_ref, w1s_ref, b1_ref, w2_ref, b2_ref, w3_ref, fr_ref, o_ref,
                      h2_ref, *, seq):
    t_col = lax.broadcasted_iota(jnp.int32, (seq, 1), 0).astype(F32) / (seq - 1.0)

    @pl.when(pl.program_id(0) == 0)
    def _():
        fr = fr_ref[...]
        pre1 = t_col * w1t_ref[...] + _mm3(zc_ref[...], w1c_ref[...]) + _mm3(zs_ref[...], w1s_ref[...]) + b1_ref[...]
        h1 = jnp.sin(fr * pre1)
        h2_ref[...] = jnp.sin(fr * (_mm3(h1, w2_ref[...]) + b2_ref[...]))

    h2 = h2_ref[...]
    ch = lax.broadcasted_iota(jnp.int32, (1, HY_W), 1).astype(F32)
    d0 = math.log(HY_TARGET) / HY_FAST
    d1 = math.log(HY_TARGET) / HY_SLOW
    deltas = jnp.abs(d0 + ch * ((d1 - d0) / (HY_W - 1)))
    win = jnp.exp(-t_col * deltas)
    ss = jnp.zeros((1, HY_W), F32)
    for lo in (0, HY_W):
        hd = _mm3(h2, w3_ref[:, lo:lo + HY_W]) * win
        o_ref[:, lo:lo + HY_W] = hd
        ss = ss + jnp.sum(hd * hd, axis=0, keepdims=True)
    inv = lax.rsqrt(ss + EPS)
    for lo in (0, HY_W):
        o_ref[:, lo:lo + HY_W] = o_ref[:, lo:lo + HY_W] * inv


def _hy_filter_call(zc, zs, w1t, w1c, w1s, b1, w2, b2, w3, fr, *, seq):
    hid = HY_HID
    full = lambda shape: pl.BlockSpec(shape, lambda o: (0,) * len(shape))
    return pl.pallas_call(
        functools.partial(_hy_filter_kernel, seq=seq),
        grid=(2,),
        in_specs=[
            full((seq, LANES)), full((seq, LANES)),
            full((1, hid)), full((LANES, hid)), full((LANES, hid)), full((1, hid)),
            full((hid, hid)), full((1, hid)),
            pl.BlockSpec((hid, 2 * HY_W), lambda o: (0, o)),
            full((1, hid)),
        ],
        out_specs=pl.BlockSpec((seq, 2 * HY_W), lambda o: (0, o)),
        out_shape=jax.ShapeDtypeStruct((seq, 4 * HY_W), F32),
        scratch_shapes=[pltpu.VMEM((seq, hid), F32)],
        compiler_params=_params(("arbitrary",)),
        name="hy_filter",
    )(zc, zs, w1t, w1c, w1s, b1, w2, b2, w3, fr)


def _hy_spec_kernel(h_ref, c_ref, s_ref, rc_ref, rs_ref, o_ref, hb_ref, *, seq, tf):
    f = pl.program_id(1)

    @pl.when(f == 0)
    def _():
        hb_ref[...] = h_ref[...].astype(BF16)

    cm = c_ref[...]
    sm = s_ref[...]
    hfw = hb_ref[:, :HY_W]
    hbw = hb_ref[:, HY_W:]
    pf = _mm(cm, hfw)
    pb = _mm(cm, hbw)
    qf = _mm(sm, hfw)
    qb = _mm(sm, hbw)
    reps = HY_W // LANES
    cf = jnp.concatenate([rc_ref[...]] * reps, axis=1)
    sf = jnp.concatenate([rs_ref[...]] * reps, axis=1)
    kre = pf + cf * pb - sf * qb
    kim = cf * qb + sf * pb - qf
    row0 = (lax.broadcasted_iota(jnp.int32, (tf, HY_W), 0) + f * tf) == 0
    wgt = jnp.where(row0, 0.5 / seq, 1.0 / seq)
    k1 = kre * wgt
    o_ref[0] = k1
    o_ref[1] = jnp.where(row0, 0.0, kim * wgt)
    o_ref[2] = jnp.where(row0, (qf - qb) * (0.5 / seq), k1)


def _hy_spec_call(h, cmat, smat, rot_c, rot_s, *, seq):
    tf = min(DFT_TILE, seq)
    nf = seq // tf
    return pl.pallas_call(
        functools.partial(_hy_spec_kernel, seq=seq, tf=tf),
        grid=(2, nf),
        in_specs=[
            pl.BlockSpec((seq, 2 * HY_W), lambda o, f: (0, o)),
            pl.BlockSpec((tf, seq), lambda o, f: (f, 0)),
            pl.BlockSpec((tf, seq), lambda o, f: (f, 0)),
            pl.BlockSpec((tf, LANES), lambda o, f: (f, 0)),
            pl.BlockSpec((tf, LANES), lambda o, f: (f, 0)),
        ],
        out_specs=pl.BlockSpec((None, 3, tf, HY_W), lambda o, f: (o, 0, f, 0)),
        out_shape=jax.ShapeDtypeStruct((2, 3, seq, HY_W), F32),
        scratch_shapes=[pltpu.VMEM((seq, 2 * HY_W), BF16)],
        compiler_params=_params(("arbitrary", "arbitrary")),
        name="hy_spectrum",
    )(h, cmat, smat, rot_c, rot_s)


def _short_conv(u, w, b, seg):
    n = u.shape[0]
    pos = lax.broadcasted_iota(jnp.int32, u.shape, 0) % seg
    prev = jnp.where(pos == 0, 0.0, pltpu.roll(u, 1, axis=0))
    nxt = jnp.where(pos == seg - 1, 0.0, pltpu.roll(u, n - 1, axis=0))
    return prev * w[0:1, :] + u * w[1:2, :] + nxt * w[2:3, :] + b


def _hy_conv_kernel(u_ref, m_ref, wu_ref, bu_ref, wm_ref, bm_ref, bias_ref, k_ref, cr_ref, sr_ref, cc_ref, sc_ref,
                    o_ref, ub_ref, acc_ref, *, conv_u, seg, nf, batch, seq):
    f = pl.program_id(1)
    tc = o_ref.shape[1]

    def load_u(b):
        u = u_ref[b * seq:(b + 1) * seq, :].astype(F32)
        if conv_u:
            u = _short_conv(u, wu_ref[...], bu_ref[...], seg)
        return u

    @pl.when(f == 0)
    def _():
        for b in range(batch):
            ub_ref[:, b * tc:(b + 1) * tc] = load_u(b).astype(BF16)
        acc_ref[...] = jnp.zeros(acc_ref.shape, F32)

    ub = ub_ref[...]
    p = _mm(cr_ref[...], ub)
    q = _mm(sr_ref[...], ub)
    k1, k2, k3 = (jnp.concatenate([k_ref[j]] * batch, axis=1) for j in range(3))
    av = (p * k1 + q * k2).astype(BF16)
    bv = (q * k3 - p * k2).astype(BF16)
    acc_ref[...] += _mm(cc_ref[...], av) + _mm(sc_ref[...], bv)

    @pl.when(f == nf - 1)
    def _():
        for b in range(batch):
            m = _short_conv(m_ref[b * seq:(b + 1) * seq, :].astype(F32), wm_ref[...], bm_ref[...], seg)
            y = acc_ref[:, b * tc:(b + 1) * tc] + bias_ref[...] * ub_ref[:, b * tc:(b + 1) * tc].astype(F32)
            o_ref[b * seq:(b + 1) * seq, :] = (m * y).astype(o_ref.dtype)


def _hy_conv_call(u_arr, u_col, m_arr, m_col, conv_w, conv_b, u_sect, m_sect, bias, kspec, order, cmat, smat,
                  smat_inv, *, batch, seq, seg, conv_u):
    t = u_arr.shape[0]
    tc = CONV_TC
    nch = HY_W // tc
    tf = min(CONV_TF, seq)
    nf = seq // tf
    ucb, mcb = u_col // tc, m_col // tc
    usb, msb = u_sect * nch, m_sect * nch
    once = pl.Buffered(1)
    return pl.pallas_call(
        functools.partial(_hy_conv_kernel, conv_u=conv_u, seg=seg, nf=nf, batch=batch, seq=seq),
        grid=(nch, nf),
        in_specs=[
            pl.BlockSpec((t, tc), lambda ch, f: (0, ucb + ch), pipeline_mode=once),
            pl.BlockSpec((t, tc), lambda ch, f: (0, mcb + ch), pipeline_mode=once),
            pl.BlockSpec((3, tc), lambda ch, f: (0, usb + ch)),
            pl.BlockSpec((1, tc), lambda ch, f: (0, usb + ch)),
            pl.BlockSpec((3, tc), lambda ch, f: (0, msb + ch)),
            pl.BlockSpec((1, tc), lambda ch, f: (0, msb + ch)),
            pl.BlockSpec((1, tc), lambda ch, f: (0, ch)),
            pl.BlockSpec((None, 3, tf, tc), lambda ch, f: (order, 0, f, ch)),
            pl.BlockSpec((tf, seq), lambda ch, f: (f, 0)),
            pl.BlockSpec((tf, seq), lambda ch, f: (f, 0)),
            pl.BlockSpec((seq, tf), lambda ch, f: (0, f)),
            pl.BlockSpec((seq, tf), lambda ch, f: (0, f)),
        ],
        out_specs=pl.BlockSpec((t, tc), lambda ch, f: (0, ch)),
        out_shape=jax.ShapeDtypeStruct((t, HY_W), BF16),
        scratch_shapes=[pltpu.VMEM((seq, batch * tc), BF16), pltpu.VMEM((seq, batch * tc), F32)],
        compiler_params=_params(("arbitrary", "arbitrary")),
        name="hy_conv",
    )(u_arr, m_arr, conv_w, conv_b, conv_w, conv_b, bias, kspec, cmat, smat, cmat, smat_inv)


MERGE_CHUNK = 512


def _merge_kernel(x_ref, mod_ref, nw_ref, yg_ref, yf_ref, yh_ref, g0_ref, g1_ref, g2_ref, wg_ref, wf_ref, wh_ref,
                  wo_ref, o_ref, xn_ref, mg_ref):
    d = o_ref.shape[1]
    yg = yg_ref[...]
    yf = yf_ref[...]
    yh = yh_ref[...]
    for c0 in range(0, d, MERGE_CHUNK):
        cols = slice(c0, c0 + MERGE_CHUNK)
        m = g0_ref[:, cols].astype(F32) * _mm(yg, wg_ref[:, cols])
        m = m + g1_ref[:, cols].astype(F32) * _mm(yf, wf_ref[:, cols])
        m = m + g2_ref[:, cols].astype(F32) * _mm(yh, wh_ref[:, cols])
        mg_ref[:, cols] = m.astype(BF16)
    mg = mg_ref[...]
    for c0 in range(0, d, MERGE_CHUNK):
        cols = slice(c0, c0 + MERGE_CHUNK)
        o_ref[:, cols] = x_ref[:, cols] + mod_ref[5:6, cols] * _mm(mg, wo_ref[:, cols])
    xn_ref[...] = _modnorm(o_ref[...], nw_ref[2:3, :], mod_ref[6:7, :], mod_ref[7:8, :]).astype(BF16)


def _merge_call(xs, mod, norm_w, proj, y_gla, y_fnet, y_hy, wbg, wbf, wbh, wo, *, layer, rows_per_mod, fixed_row):
    t, d = xs.shape
    tm = MERGE_TM
    row = _mod_row_map(rows_per_mod, tm, fixed_row)
    gate = lambda j: pl.BlockSpec((pl.Element(tm), pl.Element(d)), lambda i: (i * tm, P_GATE + j * d))
    weight = lambda k: pl.BlockSpec((None, k, d), lambda i: (layer, 0, 0), pipeline_mode=pl.Buffered(1))
    return pl.pallas_call(
        _merge_kernel,
        grid=(t // tm,),
        in_specs=[
            pl.BlockSpec((tm, d), lambda i: (i, 0)),
            pl.BlockSpec((None, None, N_ADA, d), lambda i: (layer, row(i), 0, 0)),
            pl.BlockSpec((None, 3, d), lambda i: (layer, 0, 0)),
            pl.BlockSpec((tm, GLA_VT), lambda i: (i, 0)),
            pl.BlockSpec((tm, FNET_W), lambda i: (i, 0)),
            pl.BlockSpec((tm, HY_W), lambda i: (i, 0)),
            gate(0),
            gate(1),
            gate(2),
            weight(GLA_VT),
            weight(FNET_W),
            weight(HY_W),
            weight(d),
        ],
        out_specs=[pl.BlockSpec((tm, d), lambda i: (i, 0)), pl.BlockSpec((tm, d), lambda i: (i, 0))],
        out_shape=[jax.ShapeDtypeStruct((t, d), F32), jax.ShapeDtypeStruct((t, d), BF16)],
        scratch_shapes=[pltpu.VMEM((tm, d), BF16)],
        compiler_params=_params(("parallel",)),
        name="merge_out",
    )(xs, mod, norm_w, y_gla, y_fnet, y_hy, proj, proj, proj, wbg, wbf, wbh, wo)


def _trig_tables(n, period):
    split = 32
    c = lax.broadcasted_iota(jnp.int32, (1, n), 1)

    def rows(r):
        ang = ((r * c) % period).astype(F32) * (2.0 * math.pi / period)
        return jnp.cos(ang), jnp.sin(ang)

    c_lo, s_lo = rows(lax.broadcasted_iota(jnp.int32, (split, 1), 0))
    c_hi, s_hi = rows(lax.broadcasted_iota(jnp.int32, (n // split, 1), 0) * split)
    cos = c_hi[:, None, :] * c_lo[None] - s_hi[:, None, :] * s_lo[None]
    sin = s_hi[:, None, :] * c_lo[None] + c_hi[:, None, :] * s_lo[None]
    return cos.reshape(n, n), sin.reshape(n, n)


def _fnet_tables(seq):
    cl, sl = _trig_tables(seq, seq)
    cg, sg = _trig_tables(FNET_GW, FNET_GW)
    eye = jnp.eye(FNET_GROUPS, dtype=F32)
    return cl.astype(BF16), sl.astype(BF16), jnp.kron(eye, cg).astype(BF16), jnp.kron(eye, sg).astype(BF16)


def _hyena_tables(seq):
    cm, sm = _trig_tables(seq, 2 * seq)
    r = lax.broadcasted_iota(jnp.int32, (seq, seq), 0)
    c = lax.broadcasted_iota(jnp.int32, (seq, seq), 1)
    sm_fwd = jnp.where(r == 0, (1 - 2 * (c % 2)).astype(F32), sm)
    sm_inv = jnp.where(c == 0, (1 - 2 * (r % 2)).astype(F32), sm)
    fr = lax.broadcasted_iota(jnp.int32, (seq, LANES), 0).astype(F32) * (math.pi / seq)
    return cm.astype(BF16), sm_fwd.astype(BF16), sm_inv.astype(BF16), jnp.cos(fr), jnp.sin(fr)


def _gla_gate_weights(gla_w2, gla_gb):
    depth = gla_w2.shape[0]
    w = gla_w2.reshape(depth, 2, GLA_RANK, GLA_HEADS, GLA_DK).transpose(0, 1, 3, 2, 4)
    zf = jnp.zeros((depth, GLA_HEADS, LANES, GLA_DK), F32)
    w2f = zf.at[:, :, :GLA_RANK].set(w[:, 0]).astype(BF16)
    w2b = zf.at[:, :, GLA_RANK:2 * GLA_RANK].set(w[:, 1]).astype(BF16)
    gb = gla_gb.reshape(depth, 2, GLA_HEADS, 1, GLA_DK)
    return w2f, w2b, gb[:, 0], gb[:, 1]


def kernel(x, c, ctx, c_ctx, ada_w, ada_b, norm_w, ffn1_wi, ffn1_wo, ffn2_wi, ffn2_wo, w_in, gla_w2, gla_gb,
           gla_norm_w, hy_conv_w, hy_conv_b, hy_f1_w, hy_f1_b, hy_f2_w, hy_f2_b, hy_f3_w, hy_freq, hy_bias,
           w_br_gla, w_br_fnet, w_br_hy, w_o, final_norm_w):
    batch, seq, d = x.shape
    ctx_len = ctx.shape[1]
    xs = x.reshape(batch * seq, d)
    cs = ctx.reshape(batch * ctx_len, d)

    cond = jnp.concatenate([c, c_ctx[None, :], jnp.zeros((8 - batch - 1, d), F32)], axis=0)
    mod = _ada_call(cond, ada_w, ada_b)
    ctx_row = batch

    w_t = jnp.swapaxes(w_in, 1, 2)
    wbg, wbf, wbh, wob = (w.astype(BF16) for w in (w_br_gla, w_br_fnet, w_br_hy, w_o))
    w2f, w2b, gbf, gbb = _gla_gate_weights(gla_w2, gla_gb)
    gla_nw = gla_norm_w.reshape(DEPTH, GLA_HEADS, 1, GLA_DV)

    fnet_tab = {n: _fnet_tables(n) for n in (seq, ctx_len)}
    hy_tab = {n: _hyena_tables(n) for n in (seq, ctx_len)}
    hy_pos = {n: _hy_posenc_call(n) for n in (seq, ctx_len)}

    def mixers(proj, glow, layer, n, seg, s0f, s0b, need_y):
        y_gla, sf, sb = _gla_call(proj, glow, w2f[layer], w2b[layer], gbf[layer], gbb[layer], gla_nw[layer],
                                  s0f, s0b, batch=batch, seq=n)
        if not need_y:
            return None, sf, sb
        y_fnet = _fnet_call(proj, *fnet_tab[n], batch=batch, seq=n)
        cm, sm, sm_inv, rot_c, rot_s = hy_tab[n]
        w1 = hy_f1_w[layer]
        zpad = jnp.zeros((LANES - HY_BANDS, HY_HID), F32)
        h = _hy_filter_call(*hy_pos[n], w1[0:1], jnp.concatenate([w1[1:1 + HY_BANDS], zpad], 0),
                            jnp.concatenate([w1[1 + HY_BANDS:], zpad], 0), hy_f1_b[layer][None, :],
                            hy_f2_w[layer], hy_f2_b[layer][None, :], hy_f3_w[layer], hy_freq[layer][None, :], seq=n)
        kspec = _hy_spec_call(h, cm, sm, rot_c, rot_s, seq=n)
        cw, cb = hy_conv_w[layer], hy_conv_b[layer][None, :]
        z = _hy_conv_call(proj, P_HY, proj, P_HY + HY_W, cw, cb, 0, 1, hy_bias[layer, 0][None, :], kspec, 0, cm, sm,
                          sm_inv, batch=batch, seq=n, seg=seg, conv_u=True)
        y_hy = _hy_conv_call(z, 0, proj, P_HY + 2 * HY_W, cw, cb, 0, 2, hy_bias[layer, 1][None, :], kspec, 1, cm, sm,
                             sm_inv, batch=batch, seq=n, seg=seg, conv_u=False)
        return (y_gla, y_fnet, y_hy), sf, sb

    s_zero = jnp.zeros((batch, GLA_HEADS, GLA_DV, GLA_DK), F32)
    for layer in range(DEPTH):
        last = layer == DEPTH - 1
        x_kw = dict(layer=layer, rows_per_mod=seq, fixed_row=None)
        c_kw = dict(layer=layer, rows_per_mod=ctx_len, fixed_row=ctx_row)
        xs = _ffn(xs, _norm_call(xs, mod, norm_w, sub=0, **x_kw), mod, ffn1_wi, ffn1_wo, sub=0, **x_kw)
        cs = _ffn(cs, _norm_call(cs, mod, norm_w, sub=0, **c_kw), mod, ffn1_wi, ffn1_wo, sub=0, **c_kw)
        cn, glow_c = _mixnorm_call(cs, mod, norm_w, w_t, **c_kw)
        proj_c = _proj_call(cn, w_t, layer=layer)
        ys_c, sf, sb = mixers(proj_c, glow_c, layer, ctx_len, ctx_len, s_zero, s_zero, not last)
        if not last:
            cs, cn2 = _merge_call(cs, mod, norm_w, proj_c, *ys_c, wbg, wbf, wbh, wob, **c_kw)
            cs = _ffn(cs, cn2, mod, ffn2_wi, ffn2_wo, sub=2, **c_kw)
        xn, glow_x = _mixnorm_call(xs, mod, norm_w, w_t, **x_kw)
        proj_x = _proj_call(xn, w_t, layer=layer)
        ys_x, _, _ = mixers(proj_x, glow_x, layer, seq, GRID_W, sf, sb, True)
        xs, xn2 = _merge_call(xs, mod, norm_w, proj_x, *ys_x, wbg, wbf, wbh, wob, **x_kw)
        xs = _ffn(xs, xn2, mod, ffn2_wi, ffn2_wo, sub=2, **x_kw)
    return _final_norm_call(xs, final_norm_w.reshape(1, d)).reshape(batch, seq, d)
```

```python
import functools
import math

import jax
import jax.numpy as jnp
from jax import lax
from jax.experimental import pallas as pl
from jax.experimental.pallas import tpu as pltpu

F32 = jnp.float32
BF16 = jnp.bfloat16

D_MODEL = 2048
DEPTH = 4
GRID_W = 64
N_ADA = 9
D_FF = 5504
GLA_HEADS = 4
GLA_DK = 128
GLA_DV = 256
GLA_KT = GLA_HEADS * GLA_DK
GLA_VT = GLA_HEADS * GLA_DV
GLA_RANK = 16
GLA_TAU = 16.0
GLA_CHUNK = 64
FNET_GROUPS = 4
FNET_GW = 128
FNET_W = FNET_GROUPS * FNET_GW
HY_W = 512
HY_BANDS = 16
HY_HID = 64
HY_FAST = 0.3
HY_SLOW = 1.5
HY_TARGET = 1e-2
EPS = 1e-6

LANES = 128
VMEM_LIMIT_BYTES = 56 * 1024 * 1024

ADA_TN = 1024
NORM_TM = 1024
WS_TM = 2048
DOWN_TM = 512
DOWN_TN = 512
MERGE_TM = 512
DFT_TILE = 512
CONV_TC = 256
CONV_TF = 256
GLA_UNROLL = 8

FF_TILE = 512
FF_STEPS = -(-D_FF // FF_TILE)
FF_PAD = FF_STEPS * FF_TILE
FF_SHARED = FF_PAD - D_FF

W_GLOW = 2 * GLA_KT + GLA_VT
W_IN_COLS = W_GLOW + 2 * GLA_RANK + GLA_VT + FNET_W + 3 * HY_W + 3 * D_MODEL
P_TILE = 1024
P_Q = 0
P_K = GLA_KT
P_V = 2 * GLA_KT
P_R = W_GLOW
P_FN = P_R + GLA_VT
P_HY = P_FN + FNET_W
P_GATE = P_HY + 3 * HY_W
P_TOTAL = P_GATE + 3 * D_MODEL


def _mm(a, b):
    return jnp.dot(a, b, preferred_element_type=F32)


def _mm_nt(a, b):
    return lax.dot_general(a, b, (((1,), (1,)), ((), ())), preferred_element_type=F32)


def _mm_tn(a, b):
    return lax.dot_general(a, b, (((0,), (0,)), ((), ())), preferred_element_type=F32)


def _split2(a):
    hi = a.astype(BF16)
    lo = (a - hi.astype(F32)).astype(BF16)
    return hi, lo


def _mm3(a, b):
    ah, al = _split2(a)
    bh, bl = _split2(b)
    return _mm(ah, bh) + (_mm(ah, bl) + _mm(al, bh))


def _sigmoid(x):
    return 0.5 * jnp.tanh(0.5 * x) + 0.5


def _silu(x):
    return x * _sigmoid(x)


def _params(sem):
    return pltpu.CompilerParams(dimension_semantics=sem, vmem_limit_bytes=VMEM_LIMIT_BYTES)


def _modnorm(x, nw, shift, scale):
    ms = jnp.mean(x * x, axis=-1, keepdims=True)
    y = x * lax.rsqrt(ms + EPS) * nw
    return y * (1.0 + scale) + shift


def _ada_kernel(c_ref, w_ref, b_ref, o_ref):
    a = _silu(c_ref[...]).astype(BF16)
    o_ref[...] = _mm(a, w_ref[...].astype(BF16)) + b_ref[...]


def _ada_call(cond, ada_w, ada_b):
    depth, d, n = ada_w.shape
    rows = cond.shape[0]
    tn = ADA_TN
    out = pl.pallas_call(
        _ada_kernel,
        grid=(depth, n // tn),
        in_specs=[
            pl.BlockSpec((rows, d), lambda l, j: (0, 0)),
            pl.BlockSpec((None, d, tn), lambda l, j: (l, 0, j)),
            pl.BlockSpec((None, 1, tn), lambda l, j: (l, 0, j)),
        ],
        out_specs=pl.BlockSpec((None, rows, tn), lambda l, j: (l, 0, j)),
        out_shape=jax.ShapeDtypeStruct((depth, rows, n), F32),
        compiler_params=_params(("arbitrary", "arbitrary")),
        name="ada_mod",
    )(cond, ada_w, ada_b.reshape(depth, 1, n))
    return out.reshape(depth, rows, N_ADA, d)


def _mod_row_map(rows_per_mod, tm, fixed_row):
    if fixed_row is not None:
        return lambda i: fixed_row
    per = rows_per_mod // tm
    return lambda i: i // per


def _norm_kernel(x_ref, mod_ref, nw_ref, o_ref, *, sub):
    xn = _modnorm(x_ref[...], nw_ref[sub:sub + 1, :], mod_ref[3 * sub:3 * sub + 1, :],
                  mod_ref[3 * sub + 1:3 * sub + 2, :])
    o_ref[...] = xn.astype(BF16)


def _norm_call(xs, mod, norm_w, *, layer, sub, rows_per_mod, fixed_row):
    t, d = xs.shape
    tm = NORM_TM
    row = _mod_row_map(rows_per_mod, tm, fixed_row)
    return pl.pallas_call(
        functools.partial(_norm_kernel, sub=sub),
        grid=(t // tm,),
        in_specs=[
            pl.BlockSpec((tm, d), lambda i: (i, 0)),
            pl.BlockSpec((None, None, N_ADA, d), lambda i: (layer, row(i), 0, 0)),
            pl.BlockSpec((None, 3, d), lambda i: (layer, 0, 0)),
        ],
        out_specs=pl.BlockSpec((tm, d), lambda i: (i, 0)),
        out_shape=jax.ShapeDtypeStruct((t, d), BF16),
        compiler_params=_params(("parallel",)),
        name="ffn_norm",
    )(xs, mod, norm_w)


def _ffn_up_kernel(xn_ref, wa_ref, wg_ref, h_ref, wab_ref, wgb_ref):
    @pl.when(pl.program_id(1) == 0)
    def _():
        last = pl.program_id(0) == FF_STEPS - 1
        for src_ref, dst_ref in ((wa_ref, wab_ref), (wg_ref, wgb_ref)):
            w = src_ref[...].astype(BF16)
            shifted = jnp.concatenate([w[:, FF_SHARED:], jnp.zeros((w.shape[0], FF_SHARED), BF16)], axis=1)
            dst_ref[...] = jnp.where(last, shifted, w)

    xn = xn_ref[...]
    a = _mm(xn, wab_ref[...])
    g = _mm(xn, wgb_ref[...])
    h_ref[...] = (_silu(g) * a).astype(BF16)


def _ffn_up_call(xn, wi, *, layer):
    t, d = xn.shape
    tm = min(WS_TM, t)
    back = FF_SHARED // LANES
    hid_blk = lambda f: f * (FF_TILE // LANES) - (f // (FF_STEPS - 1)) * back
    hid0 = lambda f: hid_blk(f) * LANES
    gate0 = lambda f: (D_FF // LANES + hid_blk(f)) * LANES
    return pl.pallas_call(
        _ffn_up_kernel,
        grid=(FF_STEPS, t // tm),
        in_specs=[
            pl.BlockSpec((tm, d), lambda f, i: (i, 0)),
            pl.BlockSpec((None, pl.Element(d), pl.Element(FF_TILE)), lambda f, i: (layer, 0, hid0(f))),
            pl.BlockSpec((None, pl.Element(d), pl.Element(FF_TILE)), lambda f, i: (layer, 0, gate0(f))),
        ],
        out_specs=pl.BlockSpec((tm, FF_TILE), lambda f, i: (i, f)),
        out_shape=jax.ShapeDtypeStruct((t, FF_PAD), BF16),
        scratch_shapes=[pltpu.VMEM((d, FF_TILE), BF16), pltpu.VMEM((d, FF_TILE), BF16)],
        compiler_params=_params(("arbitrary", "arbitrary")),
        name="ffn_up",
    )(xn, wi, wi)


def _ffn_down_kernel(h_ref, w_ref, x_ref, mod_ref, o_ref, wb_ref, *, sub):
    @pl.when(pl.program_id(1) == 0)
    def _():
        wb_ref[0:D_FF, :] = w_ref[...].astype(BF16)
        wb_ref[D_FF:FF_PAD, :] = jnp.zeros((FF_PAD - D_FF, wb_ref.shape[1]), BF16)

    y = _mm(h_ref[...], wb_ref[...])
    o_ref[...] = x_ref[...] + 0.5 * mod_ref[3 * sub + 2:3 * sub + 3, :] * y


def _ffn_down_call(h, wo, xs, mod, *, layer, sub, rows_per_mod, fixed_row):
    t, d = xs.shape
    tm = DOWN_TM
    tn = DOWN_TN
    row = _mod_row_map(rows_per_mod, tm, fixed_row)
    return pl.pallas_call(
        functools.partial(_ffn_down_kernel, sub=sub),
        grid=(d // tn, t // tm),
        in_specs=[
            pl.BlockSpec((tm, FF_PAD), lambda n, i: (i, 0)),
            pl.BlockSpec((None, D_FF, tn), lambda n, i: (layer, 0, n)),
            pl.BlockSpec((tm, tn), lambda n, i: (i, n)),
            pl.BlockSpec((None, None, N_ADA, tn), lambda n, i: (layer, row(i), 0, n)),
        ],
        out_specs=pl.BlockSpec((tm, tn), lambda n, i: (i, n)),
        out_shape=jax.ShapeDtypeStruct((t, d), F32),
        scratch_shapes=[pltpu.VMEM((FF_PAD, tn), BF16)],
        compiler_params=_params(("arbitrary", "arbitrary")),
        name="ffn_down",
    )(h, wo, xs, mod)


def _ffn(xs, xn, mod, wi, wo, *, layer, sub, rows_per_mod, fixed_row):
    h = _ffn_up_call(xn, wi, layer=layer)
    return _ffn_down_call(h, wo, xs, mod, layer=layer, sub=sub, rows_per_mod=rows_per_mod, fixed_row=fixed_row)


def _final_norm_kernel(x_ref, w_ref, o_ref):
    x = x_ref[...]
    ms = jnp.mean(x * x, axis=-1, keepdims=True)
    o_ref[...] = x * lax.rsqrt(ms + EPS) * w_ref[...]


def _final_norm_call(xs, w):
    t, d = xs.shape
    tm = NORM_TM
    return pl.pallas_call(
        _final_norm_kernel,
        grid=(t // tm,),
        in_specs=[pl.BlockSpec((tm, d), lambda i: (i, 0)), pl.BlockSpec((1, d), lambda i: (0, 0))],
        out_specs=pl.BlockSpec((tm, d), lambda i: (i, 0)),
        out_shape=jax.ShapeDtypeStruct((t, d), F32),
        compiler_params=_params(("parallel",)),
        name="final_norm",
    )(xs, w)


def _mixnorm_kernel(x_ref, mod_ref, nw_ref, wl_ref, xn_ref, gl_ref):
    xn = _modnorm(x_ref[...], nw_ref[1:2, :], mod_ref[3:4, :], mod_ref[4:5, :]).astype(BF16)
    xn_ref[...] = xn
    gl_ref[...] = _mm_nt(xn, wl_ref[...].astype(BF16)).astype(BF16)


def _mixnorm_call(xs, mod, norm_w, w_t, *, layer, rows_per_mod, fixed_row):
    t, d = xs.shape
    tm = NORM_TM
    row = _mod_row_map(rows_per_mod, tm, fixed_row)
    return pl.pallas_call(
        _mixnorm_kernel,
        grid=(t // tm,),
        in_specs=[
            pl.BlockSpec((tm, d), lambda i: (i, 0)),
            pl.BlockSpec((None, None, N_ADA, d), lambda i: (layer, row(i), 0, 0)),
            pl.BlockSpec((None, 3, d), lambda i: (layer, 0, 0)),
            pl.BlockSpec((None, LANES, d), lambda i: (layer, W_GLOW // LANES, 0)),
        ],
        out_specs=[pl.BlockSpec((tm, d), lambda i: (i, 0)), pl.BlockSpec((tm, LANES), lambda i: (i, 0))],
        out_shape=[jax.ShapeDtypeStruct((t, d), BF16), jax.ShapeDtypeStruct((t, LANES), BF16)],
        compiler_params=_params(("parallel",)),
        name="mix_norm",
    )(xs, mod, norm_w, w_t)


P_TILES = P_TOTAL // P_TILE
P_SKIP = 2 * GLA_RANK
P_GATE_TILE = P_GATE // P_TILE


def _proj_row0(n):
    sublanes = 8
    first = W_GLOW // P_TILE
    after = (n + (P_TILES - first)) // P_TILES
    return (n * (P_TILE // sublanes) + after * (P_SKIP // sublanes)) * sublanes


def _proj_kernel(xn_ref, w_ref, o_ref, wb_ref):
    @pl.when(pl.program_id(1) == 0)
    def _():
        wb_ref[...] = w_ref[...].astype(BF16)

    @pl.when(pl.program_id(0) < P_GATE_TILE)
    def _():
        o_ref[...] = _mm_nt(xn_ref[...], wb_ref[...]).astype(BF16)

    @pl.when(pl.program_id(0) >= P_GATE_TILE)
    def _():
        o_ref[...] = _sigmoid(_mm_nt(xn_ref[...], wb_ref[...])).astype(BF16)


def _proj_call(xn, w_t, *, layer):
    t, d = xn.shape
    tm = min(WS_TM, t)
    return pl.pallas_call(
        _proj_kernel,
        grid=(P_TILES, t // tm),
        in_specs=[
            pl.BlockSpec((tm, d), lambda n, i: (i, 0)),
            pl.BlockSpec((None, pl.Element(P_TILE), pl.Element(d)), lambda n, i: (layer, _proj_row0(n), 0)),
        ],
        out_specs=pl.BlockSpec((tm, P_TILE), lambda n, i: (i, n)),
        out_shape=jax.ShapeDtypeStruct((t, P_TOTAL), BF16),
        scratch_shapes=[pltpu.VMEM((P_TILE, d), BF16)],
        compiler_params=_params(("arbitrary", "arbitrary")),
        name="mix_proj",
    )(xn, w_t)


def _log_sigmoid(x):
    return jnp.minimum(x, 0.0) - jnp.log(1.0 + jnp.exp(-jnp.abs(x)))


def _chunk_scan(x, reverse):
    n, w = x.shape
    sub = 8
    tiles = GLA_CHUNK // sub
    x3 = x.reshape(n // sub, sub, w)
    pos = lax.broadcasted_iota(jnp.int32, x3.shape, 1)
    s = 1
    while s < sub:
        if reverse:
            x3 = x3 + jnp.where(pos < sub - s, pltpu.roll(x3, sub - s, axis=1), 0.0)
        else:
            x3 = x3 + jnp.where(pos >= s, pltpu.roll(x3, s, axis=1), 0.0)
        s *= 2
    tot = jnp.broadcast_to(x3[:, 0:1, :] if reverse else x3[:, sub - 1:sub, :], x3.shape)
    tpos = lax.broadcasted_iota(jnp.int32, x3.shape, 0) % tiles
    acc = tot
    s = 1
    while s < tiles:
        pad = jnp.zeros((s, sub, w), F32)
        if reverse:
            acc = acc + jnp.where(tpos < tiles - s, jnp.concatenate([acc[s:], pad], axis=0), 0.0)
        else:
            acc = acc + jnp.where(tpos >= s, jnp.concatenate([pad, acc[:-s]], axis=0), 0.0)
        s *= 2
    return (x3 + (acc - tot)).reshape(n, w)


def _gla_kernel(q_ref, k_ref, v_ref, r_ref, gl_ref, w2f_ref, w2b_ref, gbf_ref, gbb_ref, nw_ref,
                s0f_ref, s0b_ref, y_ref, sf_ref, sb_ref, qd_ref, kd_ref, ke_ref, dec_ref, of_ref, ob_ref,
                att_ref, sp_ref, *, seq):
    c = GLA_CHUNK
    n_chunks = seq // c
    glow = gl_ref[...]
    q = q_ref[...].astype(F32) * GLA_DK ** -0.5
    k = k_ref[...].astype(F32)
    for d, (w2_ref, gb_ref) in enumerate(((w2f_ref, gbf_ref), (w2b_ref, gbb_ref))):
        lg = _log_sigmoid(_mm(glow, w2_ref[...]) + gb_ref[...]) * (1.0 / GLA_TAU)
        gc = _chunk_scan(lg, reverse=d == 1)
        gc3 = gc.reshape(n_chunks, c, GLA_DK)
        gt = gc3[:, 0:1, :] if d == 1 else gc3[:, c - 1:c, :]
        qd_ref[d] = (q * jnp.exp(gc)).astype(BF16)
        kd_ref[d] = (k * jnp.exp(-gc)).astype(BF16)
        ke_ref[d] = (k.reshape(n_chunks, c, GLA_DK) * jnp.exp(gt - gc3)).reshape(seq, GLA_DK).astype(BF16)
        dec_ref[d] = jnp.exp(gt)
    sf_ref[...] = s0f_ref[...]
    sb_ref[...] = s0b_ref[...]

    ri = lax.broadcasted_iota(jnp.int32, (c, c), 0)
    ci = lax.broadcasted_iota(jnp.int32, (c, c), 1)

    unroll = min(GLA_UNROLL, n_chunks)
    dirs = ((0, ri >= ci, sf_ref, of_ref), (1, ri <= ci, sb_ref, ob_ref))

    def local_body(i, carry):
        for d, mask, st_ref, _ in dirs:
            n = i if d == 0 else n_chunks - 1 - i
            r0 = pl.multiple_of(n * c, c)
            scores = _mm_nt(qd_ref[d, pl.ds(r0, c), :], kd_ref[d, pl.ds(r0, c), :])
            att_ref[d, n] = jnp.where(mask, scores, 0.0).astype(BF16)
            st = st_ref[...]
            sp_ref[d, n] = st.astype(BF16)
            st_ref[...] = st * dec_ref[d, n] + _mm_tn(v_ref[pl.ds(r0, c), :], ke_ref[d, pl.ds(r0, c), :])
        return carry

    lax.fori_loop(0, n_chunks, local_body, 0, unroll=unroll)

    def out_body(n, carry):
        r0 = pl.multiple_of(n * c, c)
        v = v_ref[pl.ds(r0, c), :]
        for d, _, _, o_ref in dirs:
            o_ref[pl.ds(r0, c), :] = _mm(att_ref[d, n], v) + _mm_nt(qd_ref[d, pl.ds(r0, c), :], sp_ref[d, n])
        return carry

    lax.fori_loop(0, n_chunks, out_body, 0, unroll=unroll)

    o = of_ref[...] + ob_ref[...]
    ms = jnp.mean(o * o, axis=-1, keepdims=True)
    y = o * lax.rsqrt(ms + EPS) * nw_ref[...] * _silu(r_ref[...].astype(F32))
    y_ref[...] = y.astype(BF16)


def _gla_call(proj, glow, w2f, w2b, gbf, gbb, gla_nw, s0f, s0b, *, batch, seq):
    t = proj.shape[0]
    h = GLA_HEADS
    qb, kb = P_Q // GLA_DK, P_K // GLA_DK
    vb, rb = P_V // GLA_DV, P_R // GLA_DV
    st_spec = pl.BlockSpec((None, None, GLA_DV, GLA_DK), lambda b, hh: (b, hh, 0, 0))
    st_shape = jax.ShapeDtypeStruct((batch, h, GLA_DV, GLA_DK), F32)
    return pl.pallas_call(
        functools.partial(_gla_kernel, seq=seq),
        grid=(batch, h),
        in_specs=[
            pl.BlockSpec((seq, GLA_DK), lambda b, hh: (b, qb + hh)),
            pl.BlockSpec((seq, GLA_DK), lambda b, hh: (b, kb + hh)),
            pl.BlockSpec((seq, GLA_DV), lambda b, hh: (b, vb + hh)),
            pl.BlockSpec((seq, GLA_DV), lambda b, hh: (b, rb + hh)),
            pl.BlockSpec((seq, LANES), lambda b, hh: (b, 0)),
            pl.BlockSpec((None, LANES, GLA_DK), lambda b, hh: (hh, 0, 0)),
            pl.BlockSpec((None, LANES, GLA_DK), lambda b, hh: (hh, 0, 0)),
            pl.BlockSpec((None, 1, GLA_DK), lambda b, hh: (hh, 0, 0)),
            pl.BlockSpec((None, 1, GLA_DK), lambda b, hh: (hh, 0, 0)),
            pl.BlockSpec((None, 1, GLA_DV), lambda b, hh: (hh, 0, 0)),
            st_spec,
            st_spec,
        ],
        out_specs=[pl.BlockSpec((seq, GLA_DV), lambda b, hh: (b, hh)), st_spec, st_spec],
        out_shape=[jax.ShapeDtypeStruct((t, GLA_VT), BF16), st_shape, st_shape],
        scratch_shapes=[
            pltpu.VMEM((2, seq, GLA_DK), BF16),
            pltpu.VMEM((2, seq, GLA_DK), BF16),
            pltpu.VMEM((2, seq, GLA_DK), BF16),
            pltpu.VMEM((2, seq // GLA_CHUNK, 1, GLA_DK), F32),
            pltpu.VMEM((seq, GLA_DV), F32),
            pltpu.VMEM((seq, GLA_DV), F32),
            pltpu.VMEM((2, seq // GLA_CHUNK, GLA_CHUNK, GLA_CHUNK), BF16),
            pltpu.VMEM((2, seq // GLA_CHUNK, GLA_DV, GLA_DK), BF16),
        ],
        compiler_params=_params(("parallel", "parallel")),
        name="gla",
    )(proj, proj, proj, proj, glow, w2f, w2b, gbf, gbb, gla_nw, s0f, s0b)


def _fnet_kernel(u_ref, cl_ref, sl_ref, cg_ref, sg_ref, o_ref, *, scale):
    u = u_ref[...]
    p = _mm(cl_ref[...], u).astype(BF16)
    q = _mm(sl_ref[...], u).astype(BF16)
    y = _mm(p, cg_ref[...]) - _mm(q, sg_ref[...])
    o_ref[...] = (y * scale).astype(BF16)


def _fnet_call(proj, cl, sl, cg, sg, *, batch, seq):
    t = proj.shape[0]
    tr = min(DFT_TILE, seq)
    nr = seq // tr
    ub = P_FN // FNET_W
    return pl.pallas_call(
        functools.partial(_fnet_kernel, scale=1.0 / math.sqrt(seq * FNET_GW)),
        grid=(nr, batch),
        in_specs=[
            pl.BlockSpec((seq, FNET_W), lambda r, b: (b, ub)),
            pl.BlockSpec((tr, seq), lambda r, b: (r, 0)),
            pl.BlockSpec((tr, seq), lambda r, b: (r, 0)),
            pl.BlockSpec((FNET_W, FNET_W), lambda r, b: (0, 0)),
            pl.BlockSpec((FNET_W, FNET_W), lambda r, b: (0, 0)),
        ],
        out_specs=pl.BlockSpec((tr, FNET_W), lambda r, b: (b * nr + r, 0)),
        out_shape=jax.ShapeDtypeStruct((t, FNET_W), BF16),
        compiler_params=_params(("arbitrary", "arbitrary")),
        name="fnet",
    )(proj, cl, sl, cg, sg)


def _hy_posenc_kernel(c_ref, s_ref, *, seq):
    ti = lax.broadcasted_iota(jnp.int32, (seq, LANES), 0).astype(F32)
    lane = lax.broadcasted_iota(jnp.int32, (seq, LANES), 1)
    band_step = (HY_BANDS - 1 - 1e-4) / (HY_BANDS - 1)
    bands = jnp.where(lane < HY_BANDS, 1e-4 + lane.astype(F32) * band_step, 0.0)
    ang = 2.0 * math.pi * ti * bands / seq
    c_ref[...] = jnp.cos(ang)
    s_ref[...] = -jnp.sin(ang)


def _hy_posenc_call(seq):
    shape = jax.ShapeDtypeStruct((seq, LANES), F32)
    return pl.pallas_call(
        functools.partial(_hy_posenc_kernel, seq=seq),
        out_shape=[shape, shape],
        compiler_params=pltpu.CompilerParams(vmem_limit_bytes=VMEM_LIMIT_BYTES),
        name="hy_posenc",
    )()


def _hy_filter_kernel(zc_ref, zs_ref, w1t_ref, w1c_ref, w1s_ref, b1_ref, w2_ref, b2_ref, w3_ref, fr_ref, o_ref,
                      h2_ref, *, seq):
    t_col = lax.broadcasted_iota(jnp.int32, (seq, 1), 0).astype(F32) / (seq - 1.0)

    @pl.when(pl.program_id(0) == 0)
    def _():
        fr = fr_ref[...]
        pre1 = t_col * w1t_ref[...] + _mm3(zc_ref[...], w1c_ref[...]) + _mm3(zs_ref[...], w1s_ref[...]) + b1_ref[...]
        h1 = jnp.sin(fr * pre1)
        h2_ref[...] = jnp.sin(fr * (_mm3(h1, w2_ref[...]) + b2_ref[...]))

    h2 = h2_ref[...]
    ch = lax.broadcasted_iota(jnp.int32, (1, HY_W), 1).astype(F32)
    d0 = math.log(HY_TARGET) / HY_FAST
    d1 = math.log(HY_TARGET) / HY_SLOW
    deltas = jnp.abs(d0 + ch * ((d1 - d0) / (HY_W - 1)))
    win = jnp.exp(-t_col * deltas)
    ss = jnp.zeros((1, HY_W), F32)
    for lo in (0, HY_W):
        hd = _mm3(h2, w3_ref[:, lo:lo + HY_W]) * win
        o_ref[:, lo:lo + HY_W] = hd
        ss = ss + jnp.sum(hd * hd, axis=0, keepdims=True)
    inv = lax.rsqrt(ss + EPS)
    for lo in (0, HY_W):
        o_ref[:, lo:lo + HY_W] = o_ref[:, lo:lo + HY_W] * inv


def _hy_filter_call(zc, zs, w1t, w1c, w1s, b1, w2, b2, w3, fr, *, seq):
    hid = HY_HID
    full = lambda shape: pl.BlockSpec(shape, lambda o: (0,) * len(shape))
    return pl.pallas_call(
        functools.partial(_hy_filter_kernel, seq=seq),
        grid=(2,),
        in_specs=[
            full((seq, LANES)), full((seq, LANES)),
            full((1, hid)), full((LANES, hid)), full((LANES, hid)), full((1, hid)),
            full((hid, hid)), full((1, hid)),
            pl.BlockSpec((hid, 2 * HY_W), lambda o: (0, o)),
            full((1, hid)),
        ],
        out_specs=pl.BlockSpec((seq, 2 * HY_W), lambda o: (0, o)),
        out_shape=jax.ShapeDtypeStruct((seq, 4 * HY_W), F32),
        scratch_shapes=[pltpu.VMEM((seq, hid), F32)],
        compiler_params=_params(("arbitrary",)),
        name="hy_filter",
    )(zc, zs, w1t, w1c, w1s, b1, w2, b2, w3, fr)


def _hy_spec_kernel(h_ref, c_ref, s_ref, rc_ref, rs_ref, o_ref, hb_ref, *, seq, tf):
    f = pl.program_id(1)

    @pl.when(f == 0)
    def _():
        hb_ref[...] = h_ref[...].astype(BF16)

    cm = c_ref[...]
    sm = s_ref[...]
    hfw = hb_ref[:, :HY_W]
    hbw = hb_ref[:, HY_W:]
    pf = _mm(cm, hfw)
    pb = _mm(cm, hbw)
    qf = _mm(sm, hfw)
    qb = _mm(sm, hbw)
    reps = HY_W // LANES
    cf = jnp.concatenate([rc_ref[...]] * reps, axis=1)
    sf = jnp.concatenate([rs_ref[...]] * reps, axis=1)
    kre = pf + cf * pb - sf * qb
    kim = cf * qb + sf * pb - qf
    row0 = (lax.broadcasted_iota(jnp.int32, (tf, HY_W), 0) + f * tf) == 0
    wgt = jnp.where(row0, 0.5 / seq, 1.0 / seq)
    k1 = kre * wgt
    o_ref[0] = k1
    o_ref[1] = jnp.where(row0, 0.0, kim * wgt)
    o_ref[2] = jnp.where(row0, (qf - qb) * (0.5 / seq), k1)


def _hy_spec_call(h, cmat, smat, rot_c, rot_s, *, seq):
    tf = min(DFT_TILE, seq)
    nf = seq // tf
    return pl.pallas_call(
        functools.partial(_hy_spec_kernel, seq=seq, tf=tf),
        grid=(2, nf),
        in_specs=[
            pl.BlockSpec((seq, 2 * HY_W), lambda o, f: (0, o)),
            pl.BlockSpec((tf, seq), lambda o, f: (f, 0)),
            pl.BlockSpec((tf, seq), lambda o, f: (f, 0)),
            pl.BlockSpec((tf, LANES), lambda o, f: (f, 0)),
            pl.BlockSpec((tf, LANES), lambda o, f: (f, 0)),
        ],
        out_specs=pl.BlockSpec((None, 3, tf, HY_W), lambda o, f: (o, 0, f, 0)),
        out_shape=jax.ShapeDtypeStruct((2, 3, seq, HY_W), F32),
        scratch_shapes=[pltpu.VMEM((seq, 2 * HY_W), BF16)],
        compiler_params=_params(("arbitrary", "arbitrary")),
        name="hy_spectrum",
    )(h, cmat, smat, rot_c, rot_s)


def _short_conv(u, w, b, seg):
    n = u.shape[0]
    pos = lax.broadcasted_iota(jnp.int32, u.shape, 0) % seg
    prev = jnp.where(pos == 0, 0.0, pltpu.roll(u, 1, axis=0))
    nxt = jnp.where(pos == seg - 1, 0.0, pltpu.roll(u, n - 1, axis=0))
    return prev * w[0:1, :] + u * w[1:2, :] + nxt * w[2:3, :] + b


def _hy_conv_kernel(u_ref, m_ref, wu_ref, bu_ref, wm_ref, bm_ref, bias_ref, k_ref, cr_ref, sr_ref, cc_ref, sc_ref,
                    o_ref, ub_ref, acc_ref, *, conv_u, seg, nf, batch, seq):
    f = pl.program_id(1)
    tc = o_ref.shape[1]

    def load_u(b):
        u = u_ref[b * seq:(b + 1) * seq, :].astype(F32)
        if conv_u:
            u = _short_conv(u, wu_ref[...], bu_ref[...], seg)
        return u

    @pl.when(f == 0)
    def _():
        for b in range(batch):
            ub_ref[:, b * tc:(b + 1) * tc] = load_u(b).astype(BF16)
        acc_ref[...] = jnp.zeros(acc_ref.shape, F32)

    ub = ub_ref[...]
    p = _mm(cr_ref[...], ub)
    q = _mm(sr_ref[...], ub)
    k1, k2, k3 = (jnp.concatenate([k_ref[j]] * batch, axis=1) for j in range(3))
    av = (p * k1 + q * k2).astype(BF16)
    bv = (q * k3 - p * k2).astype(BF16)
    acc_ref[...] += _mm(cc_ref[...], av) + _mm(sc_ref[...], bv)

    @pl.when(f == nf - 1)
    def _():
        for b in range(batch):
            m = _short_conv(m_ref[b * seq:(b + 1) * seq, :].astype(F32), wm_ref[...], bm_ref[...], seg)
            y = acc_ref[:, b * tc:(b + 1) * tc] + bias_ref[...] * ub_ref[:, b * tc:(b + 1) * tc].astype(F32)
            o_ref[b * seq:(b + 1) * seq, :] = (m * y).astype(o_ref.dtype)


def _hy_conv_call(u_arr, u_col, m_arr, m_col, conv_w, conv_b, u_sect, m_sect, bias, kspec, order, cmat, smat,
                  smat_inv, *, batch, seq, seg, conv_u):
    t = u_arr.shape[0]
    tc = CONV_TC
    nch = HY_W // tc
    tf = min(CONV_TF, seq)
    nf = seq // tf
    ucb, mcb = u_col // tc, m_col // tc
    usb, msb = u_sect * nch, m_sect * nch
    once = pl.Buffered(1)
    return pl.pallas_call(
        functools.partial(_hy_conv_kernel, conv_u=conv_u, seg=seg, nf=nf, batch=batch, seq=seq),
        grid=(nch, nf),
        in_specs=[
            pl.BlockSpec((t, tc), lambda ch, f: (0, ucb + ch), pipeline_mode=once),
            pl.BlockSpec((t, tc), lambda ch, f: (0, mcb + ch), pipeline_mode=once),
            pl.BlockSpec((3, tc), lambda ch, f: (0, usb + ch)),
            pl.BlockSpec((1, tc), lambda ch, f: (0, usb + ch)),
            pl.BlockSpec((3, tc), lambda ch, f: (0, msb + ch)),
            pl.BlockSpec((1, tc), lambda ch, f: (0, msb + ch)),
            pl.BlockSpec((1, tc), lambda ch, f: (0, ch)),
            pl.BlockSpec((None, 3, tf, tc), lambda ch, f: (order, 0, f, ch)),
            pl.BlockSpec((tf, seq), lambda ch, f: (f, 0)),
            pl.BlockSpec((tf, seq), lambda ch, f: (f, 0)),
            pl.BlockSpec((seq, tf), lambda ch, f: (0, f)),
            pl.BlockSpec((seq, tf), lambda ch, f: (0, f)),
        ],
        out_specs=pl.BlockSpec((t, tc), lambda ch, f: (0, ch)),
        out_shape=jax.ShapeDtypeStruct((t, HY_W), BF16),
        scratch_shapes=[pltpu.VMEM((seq, batch * tc), BF16), pltpu.VMEM((seq, batch * tc), F32)],
        compiler_params=_params(("arbitrary", "arbitrary")),
        name="hy_conv",
    )(u_arr, m_arr, conv_w, conv_b, conv_w, conv_b, bias, kspec, cmat, smat, cmat, smat_inv)


MERGE_CHUNK = 512


def _merge_kernel(x_ref, mod_ref, nw_ref, yg_ref, yf_ref, yh_ref, g0_ref, g1_ref, g2_ref, wg_ref, wf_ref, wh_ref,
                  wo_ref, o_ref, xn_ref, mg_ref):
    d = o_ref.shape[1]
    yg = yg_ref[...]
    yf = yf_ref[...]
    yh = yh_ref[...]
    for c0 in range(0, d, MERGE_CHUNK):
        cols = slice(c0, c0 + MERGE_CHUNK)
        m = g0_ref[:, cols].astype(F32) * _mm(yg, wg_ref[:, cols])
        m = m + g1_ref[:, cols].astype(F32) * _mm(yf, wf_ref[:, cols])
        m = m + g2_ref[:, cols].astype(F32) * _mm(yh, wh_ref[:, cols])
        mg_ref[:, cols] = m.astype(BF16)
    mg = mg_ref[...]
    for c0 in range(0, d, MERGE_CHUNK):
        cols = slice(c0, c0 + MERGE_CHUNK)
        o_ref[:, cols] = x_ref[:, cols] + mod_ref[5:6, cols] * _mm(mg, wo_ref[:, cols])
    xn_ref[...] = _modnorm(o_ref[...], nw_ref[2:3, :], mod_ref[6:7, :], mod_ref[7:8, :]).astype(BF16)


def _merge_call(xs, mod, norm_w, proj, y_gla, y_fnet, y_hy, wbg, wbf, wbh, wo, *, layer, rows_per_mod, fixed_row):
    t, d = xs.shape
    tm = MERGE_TM
    row = _mod_row_map(rows_per_mod, tm, fixed_row)
    gate = lambda j: pl.BlockSpec((pl.Element(tm), pl.Element(d)), lambda i: (i * tm, P_GATE + j * d))
    weight = lambda k: pl.BlockSpec((None, k, d), lambda i: (layer, 0, 0), pipeline_mode=pl.Buffered(1))
    return pl.pallas_call(
        _merge_kernel,
        grid=(t // tm,),
        in_specs=[
            pl.BlockSpec((tm, d), lambda i: (i, 0)),
            pl.BlockSpec((None, None, N_ADA, d), lambda i: (layer, row(i), 0, 0)),
            pl.BlockSpec((None, 3, d), lambda i: (layer, 0, 0)),
            pl.BlockSpec((tm, GLA_VT), lambda i: (i, 0)),
            pl.BlockSpec((tm, FNET_W), lambda i: (i, 0)),
            pl.BlockSpec((tm, HY_W), lambda i: (i, 0)),
            gate(0),
            gate(1),
            gate(2),
            weight(GLA_VT),
            weight(FNET_W),
            weight(HY_W),
            weight(d),
        ],
        out_specs=[pl.BlockSpec((tm, d), lambda i: (i, 0)), pl.BlockSpec((tm, d), lambda i: (i, 0))],
        out_shape=[jax.ShapeDtypeStruct((t, d), F32), jax.ShapeDtypeStruct((t, d), BF16)],
        scratch_shapes=[pltpu.VMEM((tm, d), BF16)],
        compiler_params=_params(("parallel",)),
        name="merge_out",
    )(xs, mod, norm_w, y_gla, y_fnet, y_hy, proj, proj, proj, wbg, wbf, wbh, wo)


def _trig_tables(n, period):
    split = 32
    c = lax.broadcasted_iota(jnp.int32, (1, n), 1)

    def rows(r):
        ang = ((r * c) % period).astype(F32) * (2.0 * math.pi / period)
        return jnp.cos(ang), jnp.sin(ang)

    c_lo, s_lo = rows(lax.broadcasted_iota(jnp.int32, (split, 1), 0))
    c_hi, s_hi = rows(lax.broadcasted_iota(jnp.int32, (n // split, 1), 0) * split)
    cos = c_hi[:, None, :] * c_lo[None] - s_hi[:, None, :] * s_lo[None]
    sin = s_hi[:, None, :] * c_lo[None] + c_hi[:, None, :] * s_lo[None]
    return cos.reshape(n, n), sin.reshape(n, n)


def _fnet_tables(seq):
    cl, sl = _trig_tables(seq, seq)
    cg, sg = _trig_tables(FNET_GW, FNET_GW)
    eye = jnp.eye(FNET_GROUPS, dtype=F32)
    return cl.astype(BF16), sl.astype(BF16), jnp.kron(eye, cg).astype(BF16), jnp.kron(eye, sg).astype(BF16)


def _hyena_tables(seq):
    cm, sm = _trig_tables(seq, 2 * seq)
    r = lax.broadcasted_iota(jnp.int32, (seq, seq), 0)
    c = lax.broadcasted_iota(jnp.int32, (seq, seq), 1)
    sm_fwd = jnp.where(r == 0, (1 - 2 * (c % 2)).astype(F32), sm)
    sm_inv = jnp.where(c == 0, (1 - 2 * (r % 2)).astype(F32), sm)
    fr = lax.broadcasted_iota(jnp.int32, (seq, LANES), 0).astype(F32) * (math.pi / seq)
    return cm.astype(BF16), sm_fwd.astype(BF16), sm_inv.astype(BF16), jnp.cos(fr), jnp.sin(fr)


def _gla_gate_weights(gla_w2, gla_gb):
    depth = gla_w2.shape[0]
    w = gla_w2.reshape(depth, 2, GLA_RANK, GLA_HEADS, GLA_DK).transpose(0, 1, 3, 2, 4)
    zf = jnp.zeros((depth, GLA_HEADS, LANES, GLA_DK), F32)
    w2f = zf.at[:, :, :GLA_RANK].set(w[:, 0]).astype(BF16)
    w2b = zf.at[:, :, GLA_RANK:2 * GLA_RANK].set(w[:, 1]).astype(BF16)
    gb = gla_gb.reshape(depth, 2, GLA_HEADS, 1, GLA_DK)
    return w2f, w2b, gb[:, 0], gb[:, 1]


def kernel(x, c, ctx, c_ctx, ada_w, ada_b, norm_w, ffn1_wi, ffn1_wo, ffn2_wi, ffn2_wo, w_in, gla_w2, gla_gb,
           gla_norm_w, hy_conv_w, hy_conv_b, hy_f1_w, hy_f1_b, hy_f2_w, hy_f2_b, hy_f3_w, hy_freq, hy_bias,
           w_br_gla, w_br_fnet, w_br_hy, w_o, final_norm_w):
    batch, seq, d = x.shape
    ctx_len = ctx.shape[1]
    xs = x.reshape(batch * seq, d)
    cs = ctx.reshape(batch * ctx_len, d)

    cond = jnp.concatenate([c, c_ctx[None, :], jnp.zeros((8 - batch - 1, d), F32)], axis=0)
    mod = _ada_call(cond, ada_w, ada_b)
    ctx_row = batch

    w_t = jnp.swapaxes(w_in, 1, 2)
    wbg, wbf, wbh, wob = (w.astype(BF16) for w in (w_br_gla, w_br_fnet, w_br_hy, w_o))
    w2f, w2b, gbf, gbb = _gla_gate_weights(gla_w2, gla_gb)
    gla_nw = gla_norm_w.reshape(DEPTH, GLA_HEADS, 1, GLA_DV)

    fnet_tab = {n: _fnet_tables(n) for n in (seq, ctx_len)}
    hy_tab = {n: _hyena_tables(n) for n in (seq, ctx_len)}
    hy_pos = {n: _hy_posenc_call(n) for n in (seq, ctx_len)}

    def mixers(proj, glow, layer, n, seg, s0f, s0b, need_y):
        y_gla, sf, sb = _gla_call(proj, glow, w2f[layer], w2b[layer], gbf[layer], gbb[layer], gla_nw[layer],
                                  s0f, s0b, batch=batch, seq=n)
        if not need_y:
            return None, sf, sb
        y_fnet = _fnet_call(proj, *fnet_tab[n], batch=batch, seq=n)
        cm, sm, sm_inv, rot_c, rot_s = hy_tab[n]
        w1 = hy_f1_w[layer]
        zpad = jnp.zeros((LANES - HY_BANDS, HY_HID), F32)
        h = _hy_filter_call(*hy_pos[n], w1[0:1], jnp.concatenate([w1[1:1 + HY_BANDS], zpad], 0),
                            jnp.concatenate([w1[1 + HY_BANDS:], zpad], 0), hy_f1_b[layer][None, :],
                            hy_f2_w[layer], hy_f2_b[layer][None, :], hy_f3_w[layer], hy_freq[layer][None, :], seq=n)
        kspec = _hy_spec_call(h, cm, sm, rot_c, rot_s, seq=n)
        cw, cb = hy_conv_w[layer], hy_conv_b[layer][None, :]
        z = _hy_conv_call(proj, P_HY, proj, P_HY + HY_W, cw, cb, 0, 1, hy_bias[layer, 0][None, :], kspec, 0, cm, sm,
                          sm_inv, batch=batch, seq=n, seg=seg, conv_u=True)
        y_hy = _hy_conv_call(z, 0, proj, P_HY + 2 * HY_W, cw, cb, 0, 2, hy_bias[layer, 1][None, :], kspec, 1, cm, sm,
                             sm_inv, batch=batch, seq=n, seg=seg, conv_u=False)
        return (y_gla, y_fnet, y_hy), sf, sb

    s_zero = jnp.zeros((batch, GLA_HEADS, GLA_DV, GLA_DK), F32)
    for layer in range(DEPTH):
        last = layer == DEPTH - 1
        x_kw = dict(layer=layer, rows_per_mod=seq, fixed_row=None)
        c_kw = dict(layer=layer, rows_per_mod=ctx_len, fixed_row=ctx_row)
        xs = _ffn(xs, _norm_call(xs, mod, norm_w, sub=0, **x_kw), mod, ffn1_wi, ffn1_wo, sub=0, **x_kw)
        cs = _ffn(cs, _norm_call(cs, mod, norm_w, sub=0, **c_kw), mod, ffn1_wi, ffn1_wo, sub=0, **c_kw)
        cn, glow_c = _mixnorm_call(cs, mod, norm_w, w_t, **c_kw)
        proj_c = _proj_call(cn, w_t, layer=layer)
        ys_c, sf, sb = mixers(proj_c, glow_c, layer, ctx_len, ctx_len, s_zero, s_zero, not last)
        if not last:
            cs, cn2 = _merge_call(cs, mod, norm_w, proj_c, *ys_c, wbg, wbf, wbh, wob, **c_kw)
            cs = _ffn(cs, cn2, mod, ffn2_wi, ffn2_wo, sub=2, **c_kw)
        xn, glow_x = _mixnorm_call(xs, mod, norm_w, w_t, **x_kw)
        proj_x = _proj_call(xn, w_t, layer=layer)
        ys_x, _, _ = mixers(proj_x, glow_x, layer, seq, GRID_W, sf, sb, True)
        xs, xn2 = _merge_call(xs, mod, norm_w, proj_x, *ys_x, wbg, wbf, wbh, wob, **x_kw)
        xs = _ffn(xs, xn2, mod, ffn2_wi, ffn2_wo, sub=2, **x_kw)
    return _final_norm_call(xs, final_norm_w.reshape(1, d)).reshape(batch, seq, d)
```
